```python
import jax, jax.numpy as jnp
from jax import lax
import numpy as np

D_MODEL = 2048
BATCH = 2
SEQ = 16384
DEPTH = 2

HEAD_DIM = 128
ROT_DIM = HEAD_DIM // 4
ROPE_THETA = 500000.0
NORM_EPS = 1e-6

DIL_GROUPS = ((128, 1), (512, 4), (2048, 16))
A_SLOTS = D_MODEL // 512
A_HEADS = A_SLOTS * len(DIL_GROUPS)
A_WIDTH = A_SLOTS * HEAD_DIM
B_Q_HEADS = D_MODEL // 256
B_KV_HEADS = B_Q_HEADS // 4
B_HALF_WINDOW = 128
B_WIDTH = B_Q_HEADS * HEAD_DIM
C_HEADS = D_MODEL // 256
C_DK = 128
C_DV = 128
C_WIDTH = C_HEADS * C_DV
C_CHUNK = 64
N_BRANCHES = 3

SPLIT_SIZES = (
    3 * A_HEADS * HEAD_DIM,
    A_WIDTH,
    B_Q_HEADS * HEAD_DIM,
    B_KV_HEADS * HEAD_DIM,
    B_KV_HEADS * HEAD_DIM,
    B_WIDTH,
    C_HEADS * C_DK,
    C_HEADS * C_DK,
    C_HEADS * C_DK,
    C_HEADS * C_DV,
    C_WIDTH,
    N_BRANCHES * D_MODEL,
)
IN_COLS = sum(SPLIT_SIZES)

kernel_name = "hybrid_dilated_window_hgrn2_encoder"


def rms_norm(t, gain):
    t32 = t.astype(jnp.float32)
    y = t32 * lax.rsqrt(jnp.mean(t32 * t32, axis=-1, keepdims=True) + NORM_EPS)
    return (y * gain.astype(jnp.float32)).astype(t.dtype)


def rope_tables(seq):
    pos = jnp.arange(seq, dtype=jnp.float32)
    inv_freq = ROPE_THETA ** (-jnp.arange(0, ROT_DIM, 2, dtype=jnp.float32) / ROT_DIM)
    ang = pos[:, None] * inv_freq[None, :]
    return jnp.cos(ang), jnp.sin(ang)


def apply_partial_rope(t, cos, sin):
    half = ROT_DIM // 2
    t32 = t.astype(jnp.float32)
    x1, x2, rest = t32[..., :half], t32[..., half:ROT_DIM], t32[..., ROT_DIM:]
    c, s = cos[:, None, :], sin[:, None, :]
    return jnp.concatenate([x1 * c - x2 * s, x2 * c + x1 * s, rest], axis=-1).astype(t.dtype)


def banded_window_attention(q, k, v, half_window, sink=None):
    bsz, L, hq, dh = q.shape
    hkv = k.shape[2]
    grp = hq // hkv
    blk = half_window
    nb = -(-L // blk)
    lp = nb * blk
    pad = lp - L
    q = jnp.pad(q, ((0, 0), (0, pad), (0, 0), (0, 0)))
    kpad = ((0, 0), (blk, pad + blk), (0, 0), (0, 0))
    k = jnp.pad(k, kpad)
    v = jnp.pad(v, kpad)

    def three_blocks(t):
        t = t.reshape(bsz, nb + 2, blk, hkv, dh)
        return jnp.concatenate([t[:, :-2], t[:, 1:-1], t[:, 2:]], axis=2)

    kb, vb = three_blocks(k), three_blocks(v)
    qb = q.reshape(bsz, nb, blk, hkv, grp, dh)
    s = jnp.einsum('bnqhgd,bnkhd->bnhgqk', qb, kb,
                   preferred_element_type=jnp.float32) * (dh ** -0.5)
    qpos = jnp.arange(lp).reshape(nb, blk)
    kpos = (jnp.arange(nb)[:, None] - 1) * blk + jnp.arange(3 * blk)[None, :]
    rel = kpos[:, None, :] - qpos[:, :, None]
    valid = (jnp.abs(rel) <= half_window) & (kpos[:, None, :] >= 0) & (kpos[:, None, :] < L)
    s = jnp.where(valid[None, :, None, None], s, -jnp.inf)
    m = jnp.max(s, axis=-1, keepdims=True)
    if sink is not None:
        sink_b = sink.astype(jnp.float32).reshape(1, 1, hkv, grp, 1, 1)
        m = jnp.maximum(m, sink_b)
    p = jnp.exp(s - m)
    denom = jnp.sum(p, axis=-1, keepdims=True)
    if sink is not None:
        denom = denom + jnp.exp(sink_b - m)
    o = jnp.einsum('bnhgqk,bnkhd->bnqhgd', p / denom, vb.astype(jnp.float32))
    lse = (m + jnp.log(denom))[..., 0].transpose(0, 1, 4, 2, 3)
    o = o.reshape(bsz, lp, hq, dh)[:, :L]
    lse = lse.reshape(bsz, lp, hq)[:, :L]
    return o, lse


def dilated_attention(q, k, v):
    bsz, s, _, dh = q.shape
    outs, lses = [], []
    for g, (window, dil) in enumerate(DIL_GROUPS):
        hs = slice(g * A_SLOTS, (g + 1) * A_SLOTS)

        def gather(t):
            t = t[:, :, hs].reshape(bsz, s // dil, dil, A_SLOTS, dh)
            return t.transpose(0, 2, 1, 3, 4).reshape(bsz * dil, s // dil, A_SLOTS, dh)

        o, lse = banded_window_attention(gather(q), gather(k), gather(v), window // (2 * dil))
        o = o.reshape(bsz, dil, s // dil, A_SLOTS, dh).transpose(0, 2, 1, 3, 4).reshape(bsz, s, A_SLOTS, dh)
        lse = lse.reshape(bsz, dil, s // dil, A_SLOTS).transpose(0, 2, 1, 3).reshape(bsz, s, A_SLOTS)
        outs.append(o)
        lses.append(lse)
    w = jax.nn.softmax(jnp.stack(lses), axis=0)
    out = jnp.einsum('gbsh,gbshd->bshd', w, jnp.stack(outs))
    return out.reshape(bsz, s, A_WIDTH)


def gla_chunk_scan(q, k, log_f, v):
    bsz, s, h, dk = q.shape
    dv = v.shape[-1]
    n = s // C_CHUNK

    def to_chunks(t):
        return t.reshape(bsz, n, C_CHUNK, h, t.shape[-1]).transpose(1, 0, 3, 2, 4)

    causal = jnp.tril(jnp.ones((C_CHUNK, C_CHUNK), dtype=bool))[:, :, None]

    def step(state, inp):
        qx, kx, lfx, vx = inp
        b = jnp.cumsum(lfx, axis=2)
        rel = b[:, :, :, None, :] - b[:, :, None, :, :]
        decay = jnp.exp(jnp.where(causal, rel, -jnp.inf))
        scores = jnp.einsum('bhtk,bhsk,bhtsk->bhts', qx, kx, decay)
        o = (jnp.einsum('bhts,bhsv->bhtv', scores, vx)
             + jnp.einsum('bhtk,bhkv->bhtv', qx * jnp.exp(b), state))
        b_last = b[:, :, -1:, :]
        state = (state * jnp.exp(b_last)[:, :, 0, :, None]
                 + jnp.einsum('bhsk,bhsv->bhkv', kx * jnp.exp(b_last - b), vx))
        return state, o

    state0 = jnp.zeros((bsz, h, dk, dv), jnp.float32)
    _, o = lax.scan(step, state0, (to_chunks(q), to_chunks(k), to_chunks(log_f), to_chunks(v)))
    return o.transpose(1, 0, 3, 2, 4).reshape(bsz, s, h, dv)


def hgrn2_bidirectional(q, f_fwd, f_bwd, inp, gate, lb, norm_gain):
    bsz, s, _ = q.shape

    def heads(t, d):
        return t.reshape(bsz, s, C_HEADS, d).astype(jnp.float32)

    qh = jax.nn.silu(heads(q, C_DK))
    vh = heads(inp, C_DV)
    lb = lb.reshape(C_HEADS, C_DK)
    log_lb = jnp.log(lb)

    def gates(fz):
        fz = heads(fz, C_DK)
        log_f = jnp.logaddexp(log_lb, jnp.log1p(-lb) + jax.nn.log_sigmoid(fz))
        k = (1.0 - lb) * jax.nn.sigmoid(-fz)
        return log_f, k

    lf_f, k_f = gates(f_fwd)
    lf_b, k_b = gates(f_bwd)
    o_f = gla_chunk_scan(qh, k_f, lf_f, vh)
    flip = lambda t: jnp.flip(t, axis=1)
    o_b = flip(gla_chunk_scan(flip(qh), flip(k_b), flip(lf_b), flip(vh)))
    o = o_f + o_b
    o = o * lax.rsqrt(jnp.mean(o * o, axis=-1, keepdims=True) + NORM_EPS)
    o = o * norm_gain.astype(jnp.float32).reshape(C_HEADS, C_DV)
    return o.reshape(bsz, s, C_WIDTH) * jax.nn.silu(gate.astype(jnp.float32))


def setup_inputs(seed: int = 0) -> dict:
    key = jax.random.key(seed)
    ks = jax.random.split(key, 12)
    f32 = jnp.float32

    def dense(k, shape, fan_in):
        return jax.random.normal(k, shape, f32) * (fan_in ** -0.5)

    return {
        "x": jax.random.normal(ks[0], (BATCH, SEQ, D_MODEL), f32),
        "pre_norm": 1.0 + 0.05 * jax.random.normal(ks[1], (DEPTH, D_MODEL), f32),
        "post_norm": 1.0 + 0.05 * jax.random.normal(ks[2], (DEPTH, D_MODEL), f32),
        "w_in": dense(ks[3], (DEPTH, D_MODEL, IN_COLS), D_MODEL),
        "sink_logits": 0.5 * jax.random.normal(ks[4], (DEPTH, B_Q_HEADS), f32),
        "hgrn_lower_bounds": 0.5 * jax.random.normal(ks[5], (DEPTH, C_HEADS * C_DK), f32),
        "hgrn_norm": 1.0 + 0.05 * jax.random.normal(ks[6], (DEPTH, C_WIDTH), f32),
        "w_branch_a": dense(ks[7], (DEPTH, A_WIDTH, D_MODEL), A_WIDTH),
        "w_branch_b": dense(ks[8], (DEPTH, B_WIDTH, D_MODEL), B_WIDTH),
        "w_branch_c": dense(ks[9], (DEPTH, C_WIDTH, D_MODEL), C_WIDTH),
        "w_out": dense(ks[10], (DEPTH, D_MODEL, D_MODEL), D_MODEL),
    }


def reference(x, pre_norm, post_norm, w_in, sink_logits, hgrn_lower_bounds, hgrn_norm,
              w_branch_a, w_branch_b, w_branch_c, w_out):
    bsz, s, _ = x.shape
    cos, sin = rope_tables(s)
    lb_all = jnp.cumsum(jax.nn.softmax(hgrn_lower_bounds.astype(jnp.float32), axis=0), axis=0)
    lb_all = lb_all - lb_all[0:1]
    split_points = [int(p) for p in np.cumsum(SPLIT_SIZES)[:-1]]
    for layer in range(DEPTH):
        xn = rms_norm(x, pre_norm[layer])
        proj = xn @ w_in[layer]
        (a_qkv, a_gate, b_q, b_k, b_v, b_gate,
         c_q, c_ff, c_fb, c_i, c_gate, merge) = jnp.split(proj, split_points, axis=-1)

        a_qkv = a_qkv.reshape(bsz, s, 3, A_HEADS, HEAD_DIM)
        qa = apply_partial_rope(a_qkv[:, :, 0], cos, sin)
        ka = apply_partial_rope(a_qkv[:, :, 1], cos, sin)
        ya = dilated_attention(qa, ka, a_qkv[:, :, 2]) * jax.nn.silu(a_gate.astype(jnp.float32))

        qb = apply_partial_rope(b_q.reshape(bsz, s, B_Q_HEADS, HEAD_DIM), cos, sin)
        kb = apply_partial_rope(b_k.reshape(bsz, s, B_KV_HEADS, HEAD_DIM), cos, sin)
        vb = b_v.reshape(bsz, s, B_KV_HEADS, HEAD_DIM)
        ob, _ = banded_window_attention(qb, kb, vb, B_HALF_WINDOW, sink=sink_logits[layer])
        yb = ob.reshape(bsz, s, B_WIDTH) * jax.nn.silu(b_gate.astype(jnp.float32))

        yc = hgrn2_bidirectional(c_q, c_ff, c_fb, c_i, c_gate, lb_all[layer], hgrn_norm[layer])

        g = jax.nn.sigmoid(merge.astype(jnp.float32)).reshape(bsz, s, N_BRANCHES, D_MODEL)
        dt = x.dtype
        y = (g[:, :, 0] * (ya.astype(dt) @ w_branch_a[layer])
             + g[:, :, 1] * (yb.astype(dt) @ w_branch_b[layer])
             + g[:, :, 2] * (yc.astype(dt) @ w_branch_c[layer]))
        out = y.astype(dt) @ w_out[layer]
        x = x + rms_norm(out, post_norm[layer])
    return x
```

```python
import functools

import jax
import jax.numpy as jnp
from jax import lax
from jax.experimental import pallas as pl
from jax.experimental.pallas import tpu as pltpu

F32 = jnp.float32
BF16 = jnp.bfloat16

LANE = 128
D_MODEL = 2048
DEPTH = 2
HEAD_DIM = 128
ROT_DIM = HEAD_DIM // 4
ROT_HALF = ROT_DIM // 2
ROPE_THETA = 500000.0
NORM_EPS = 1e-6

DIL_GROUPS = ((128, 1), (512, 4), (2048, 16))
A_SLOTS = 4
A_HEADS = A_SLOTS * len(DIL_GROUPS)
A_WIDTH = A_SLOTS * HEAD_DIM
B_Q_HEADS = 8
B_KV_HEADS = 2
B_HALF_WINDOW = 128
B_WIDTH = B_Q_HEADS * HEAD_DIM
C_HEADS = 8
C_WIDTH = C_HEADS * HEAD_DIM
C_CHUNK = 64
N_BRANCHES = 3

CB_AQ = 0
CB_AK = CB_AQ + A_HEADS
CB_AV = CB_AK + A_HEADS
CB_AGATE = CB_AV + A_HEADS
CB_BQ = CB_AGATE + A_SLOTS
CB_BK = CB_BQ + B_Q_HEADS
CB_BV = CB_BK + B_KV_HEADS
CB_BGATE = CB_BV + B_KV_HEADS
CB_CQ = CB_BGATE + B_Q_HEADS
CB_CFF = CB_CQ + C_HEADS
CB_CFB = CB_CFF + C_HEADS
CB_CI = CB_CFB + C_HEADS
CB_CGATE = CB_CI + C_HEADS
CB_MERGE = CB_CGATE + C_HEADS
N_CB = CB_MERGE + N_BRANCHES * D_MODEL // LANE
IN_COLS = N_CB * LANE

MASK_VALUE = -1e30
HGRN_FACTOR_RANGE = 80.0

VMEM_LIMIT = 56 * 1024 * 1024


def _cparams(sem):
    return pltpu.CompilerParams(dimension_semantics=sem, vmem_limit_bytes=VMEM_LIMIT)


def _in_projection_body(x_ref, g_ref, w_ref, o_ref, xn_ref, *, n_blk):
    @pl.when(pl.program_id(1) == 0)
    def _():
        x = x_ref[...]
        y = x * lax.rsqrt(jnp.mean(x * x, axis=-1, keepdims=True) + NORM_EPS)
        xn_ref[...] = (y * g_ref[...]).astype(BF16)

    acc = jnp.dot(xn_ref[...], w_ref[...], preferred_element_type=F32)
    for c in range(n_blk):
        o_ref[c] = acc[:, c * LANE:(c + 1) * LANE]


def _in_projection(x2, gain, w_bf16, *, tm, tn):
    rows, d = x2.shape
    cols = w_bf16.shape[1]
    n_blk = tn // LANE
    return pl.pallas_call(
        functools.partial(_in_projection_body, n_blk=n_blk),
        grid=(rows // tm, cols // tn),
        in_specs=[
            pl.BlockSpec((tm, d), lambda i, j: (i, 0)),
            pl.BlockSpec((1, d), lambda i, j: (0, 0)),
            pl.BlockSpec((d, tn), lambda i, j: (0, j)),
        ],
        out_specs=pl.BlockSpec((n_blk, tm, LANE), lambda i, j: (j, i, 0)),
        out_shape=jax.ShapeDtypeStruct((cols // LANE, rows, LANE), F32),
        scratch_shapes=[pltpu.VMEM((tm, d), BF16)],
        compiler_params=_cparams(("parallel", "arbitrary")),
        name="in_projection",
    )(x2, gain.reshape(1, d), w_bf16)


def _rope_tables(seq):
    pos = jnp.arange(seq, dtype=F32)
    inv_freq = ROPE_THETA ** (-jnp.arange(0, ROT_DIM, 2, dtype=F32) / ROT_DIM)
    ang = pos[:, None] * inv_freq[None, :]
    cos, sin = jnp.cos(ang), jnp.sin(ang)
    rest = HEAD_DIM - ROT_DIM
    c_full = jnp.concatenate([cos, cos, jnp.ones((seq, rest), F32)], axis=-1)
    s_full = jnp.concatenate([-sin, sin, jnp.zeros((seq, rest), F32)], axis=-1)
    return jnp.stack([c_full, s_full])


def _apply_rope(t, cos_full, sin_signed):
    lane = lax.broadcasted_iota(jnp.int32, t.shape, 1)
    swapped = jnp.where(lane < ROT_HALF,
                        pltpu.roll(t, LANE - ROT_HALF, 1),
                        pltpu.roll(t, ROT_HALF, 1))
    return t * cos_full + swapped * sin_signed


def _attention_body(*refs, dil, hw, tq, n_sub, n_q, has_sink, want_lse):
    q_ref, kp_ref, kc_ref, kn_ref, vp_ref, vc_ref, vn_ref, rq_ref, rp_ref, rn_ref = refs[:10]
    rest = list(refs[10:])
    sink_ref = rest.pop(0) if has_sink else None
    o_ref = rest.pop(0)
    lse_ref = rest.pop(0) if want_lse else None

    t = pl.program_id(2)
    nk = tq + 2 * hw
    qi = lax.broadcasted_iota(jnp.int32, (tq, nk), 0)
    kj = lax.broadcasted_iota(jnp.int32, (tq, nk), 1)
    kpos = t * tq - hw + kj
    valid = (jnp.abs(kj - hw - qi) <= hw) & (kpos >= 0) & (kpos < n_sub)
    scale = HEAD_DIM ** -0.5

    def rows(n, ph):
        return slice(None) if dil == 1 else pl.ds(ph, n, stride=dil)

    def roped(ref, rope_ref, n, ph, lead=()):
        r = rows(n, ph)
        return _apply_rope(ref[lead + (r, slice(None))], rope_ref[0, r, :], rope_ref[1, r, :])

    def phase(ph, carry):
        k = jnp.concatenate([roped(kp_ref, rp_ref, hw, ph),
                             roped(kc_ref, rq_ref, tq, ph),
                             roped(kn_ref, rn_ref, hw, ph)], axis=0).astype(BF16)
        v = jnp.concatenate([vp_ref[rows(hw, ph), :],
                             vc_ref[rows(tq, ph), :],
                             vn_ref[rows(hw, ph), :]], axis=0).astype(BF16)
        for g in range(n_q):
            q = (roped(q_ref, rq_ref, tq, ph, lead=(g,)) * scale).astype(BF16)
            s = lax.dot_general(q, k, (((1,), (1,)), ((), ())), preferred_element_type=F32)
            s = jnp.where(valid, s, MASK_VALUE)
            m = jnp.max(s, axis=-1, keepdims=True)
            if has_sink:
                sink = sink_ref[g][:, :1]
                m = jnp.maximum(m, sink)
            p = jnp.exp(s - m)
            denom = jnp.sum(p, axis=-1, keepdims=True)
            if has_sink:
                denom = denom + jnp.exp(sink - m)
            o = jnp.dot(p.astype(BF16), v, preferred_element_type=F32) / denom
            o_ref[g, rows(tq, ph), :] = o
            if want_lse:
                lse_ref[g, rows(tq, ph), :] = jnp.broadcast_to(m + jnp.log(denom), (tq, LANE))
        return carry

    if dil == 1:
        phase(0, 0)
    else:
        lax.fori_loop(0, dil, phase, 0)


def _banded_attention(p4, rope, sink, *, dil, hw, q_cb, k_cb, v_cb, n_kv, n_q, want_lse, tq):
    _, bsz, seq, _ = p4.shape
    n_sub = seq // dil
    tq = min(tq, n_sub)
    w = tq * dil
    h = hw * dil
    per = w // h
    n_halo = seq // h
    assert q_cb % n_q == 0 and seq % w == 0 and w % h == 0
    has_sink = sink is not None

    def cur(cb):
        return pl.BlockSpec((None, None, w, LANE), lambda b, hd, t: (cb + hd, b, t, 0))

    def prev(cb):
        return pl.BlockSpec((None, None, h, LANE),
                            lambda b, hd, t: (cb + hd, b, jnp.maximum(t * per - 1, 0), 0))

    def nxt(cb):
        return pl.BlockSpec((None, None, h, LANE),
                            lambda b, hd, t: (cb + hd, b, jnp.minimum((t + 1) * per, n_halo - 1), 0))

    in_specs = [
        pl.BlockSpec((n_q, None, w, LANE), lambda b, hd, t: (q_cb // n_q + hd, b, t, 0)),
        prev(k_cb), cur(k_cb), nxt(k_cb),
        prev(v_cb), cur(v_cb), nxt(v_cb),
        pl.BlockSpec((2, w, LANE), lambda b, hd, t: (0, t, 0)),
        pl.BlockSpec((2, h, LANE), lambda b, hd, t: (0, jnp.maximum(t * per - 1, 0), 0)),
        pl.BlockSpec((2, h, LANE), lambda b, hd, t: (0, jnp.minimum((t + 1) * per, n_halo - 1), 0)),
    ]
    args = [p4, p4, p4, p4, p4, p4, p4, rope, rope, rope]
    if has_sink:
        in_specs.append(pl.BlockSpec((n_q, 1, LANE), lambda b, hd, t: (hd, 0, 0)))
        args.append(sink)
    o_spec = pl.BlockSpec((n_q, None, w, LANE), lambda b, hd, t: (hd, b, t, 0))
    o_shape = jax.ShapeDtypeStruct((n_kv * n_q, bsz, seq, LANE), F32)
    out_specs = [o_spec, o_spec] if want_lse else o_spec
    out_shape = [o_shape, o_shape] if want_lse else o_shape
    return pl.pallas_call(
        functools.partial(_attention_body, dil=dil, hw=hw, tq=tq, n_sub=n_sub, n_q=n_q,
                          has_sink=has_sink, want_lse=want_lse),
        grid=(bsz, n_kv, seq // w),
        in_specs=in_specs,
        out_specs=out_specs,
        out_shape=out_shape,
        compiler_params=_cparams(("parallel", "parallel", "arbitrary")),
        name=f"banded_attention_d{dil}_w{hw}",
    )(*args)


def _hgrn_body(*refs, layer, reverse, tb, finalize):
    q_ref, f_ref, v_ref, lbp_ref = refs[:4]
    rest = list(refs[4:])
    if finalize:
        of_ref, gate_ref, gain_ref = rest[:3]
        rest = rest[3:]
    o_ref, st_ref, qs_ref, ks_ref, bs_ref, sc_ref = rest
    c = C_CHUNK
    n_chunks = tb // c

    @pl.when(pl.program_id(2) == 0)
    def _():
        st_ref[...] = jnp.zeros_like(st_ref)

    lbp = lbp_ref[...]
    e_l = jnp.exp(lbp - jnp.max(lbp, axis=0, keepdims=True))
    sm = e_l / jnp.sum(e_l, axis=0, keepdims=True)
    lb = jnp.zeros((1, LANE), F32)
    for i in range(1, layer + 1):
        lb = lb + sm[i]
    one_m_lb = 1.0 - lb
    tiny = jnp.float32(1e-37)
    log_lb = jnp.where(lb > 0.0, jnp.log(jnp.maximum(lb, tiny)), -jnp.inf)
    log_1m_lb = jnp.where(one_m_lb > 0.0, jnp.log(jnp.maximum(one_m_lb, tiny)), -jnp.inf)

    row = lax.broadcasted_iota(jnp.int32, (c, c), 0)
    col = lax.broadcasted_iota(jnp.int32, (c, c), 1)
    causal = (row <= col) if reverse else (row >= col)
    tri = jnp.where(causal, 1.0, 0.0).astype(BF16)
    nt_dims = (((1,), (1,)), ((), ()))
    tn_dims = (((0,), (0,)), ((), ()))

    def chunk(i, carry):
        ci = (n_chunks - 1 - i) if reverse else i
        rs = pl.ds(pl.multiple_of(ci * c, c), c)
        qz = q_ref[rs, :]
        fz = f_ref[rs, :]
        v = v_ref[rs, :].astype(BF16)

        e = jnp.exp(-jnp.abs(fz))
        log_sig = jnp.minimum(fz, 0.0) - jnp.log(1.0 + e)
        sig_neg = jnp.where(fz >= 0.0, e, 1.0) / (1.0 + e)
        cand = log_1m_lb + log_sig
        log_f = jnp.maximum(log_lb, cand) + jnp.log(1.0 + jnp.exp(-jnp.abs(log_lb - cand)))
        kk = one_m_lb * sig_neg
        qh = qz / (1.0 + jnp.exp(-qz))

        hi = log_f.astype(BF16)
        r1 = log_f - hi.astype(F32)
        mid = r1.astype(BF16)
        lo = (r1 - mid.astype(F32)).astype(BF16)
        b = (jnp.dot(tri, hi, preferred_element_type=F32)
             + jnp.dot(tri, mid, preferred_element_type=F32)
             + jnp.dot(tri, lo, preferred_element_type=F32))
        b_end = b[0:1] if reverse else b[c - 1:c]
        d = b - b[c // 2 - 1:c // 2]
        span = jnp.max(jnp.abs(d))

        @pl.when(span <= HGRN_FACTOR_RANGE)
        def _():
            qt = (qh * jnp.exp(d)).astype(BF16)
            kt = (kk * jnp.exp(-d)).astype(BF16)
            sc_ref[...] = lax.dot_general(qt, kt, nt_dims, preferred_element_type=F32)

        @pl.when(span > HGRN_FACTOR_RANGE)
        def _():
            qs_ref[...] = qh
            ks_ref[...] = kk
            bs_ref[...] = b

            def pair(s, sc):
                rel = jnp.minimum(bs_ref[...] - bs_ref[pl.ds(s, 1), :], 0.0)
                w = qs_ref[...] * ks_ref[pl.ds(s, 1), :] * jnp.exp(rel)
                return jnp.where(col == s, jnp.sum(w, axis=-1, keepdims=True), sc)

            sc_ref[...] = lax.fori_loop(0, c, pair, jnp.zeros((c, c), F32))

        scores = jnp.where(causal, sc_ref[...], 0.0).astype(BF16)
        st = st_ref[...]
        o = (jnp.dot(scores, v, preferred_element_type=F32)
             + lax.dot_general((qh * jnp.exp(b)).astype(BF16), st.astype(BF16), nt_dims,
                               preferred_element_type=F32))
        k_in = (kk * jnp.exp(b_end - b)).astype(BF16)
        st_ref[...] = st * jnp.exp(b_end) + lax.dot_general(v, k_in, tn_dims,
                                                            preferred_element_type=F32)
        if finalize:
            o = o + of_ref[rs, :]
            o = o * lax.rsqrt(jnp.mean(o * o, axis=-1, keepdims=True) + NORM_EPS)
            g = gate_ref[rs, :]
            o = o * gain_ref[...] * (g / (1.0 + jnp.exp(-g)))
        o_ref[rs, :] = o.astype(o_ref.dtype)
        return carry

    lax.fori_loop(0, n_chunks, chunk, 0)


def _hgrn_scan(p4, lower_bounds, *, layer, reverse, f_cb, tb, o_fwd=None, gain=None):
    _, bsz, seq, _ = p4.shape
    tb = min(tb, seq)
    n_t = seq // tb
    finalize = o_fwd is not None

    def tt(t):
        return (n_t - 1 - t) if reverse else t

    def blk(cb):
        return pl.BlockSpec((None, None, tb, LANE), lambda b, hd, t: (cb + hd, b, tt(t), 0))

    in_specs = [blk(CB_CQ), blk(f_cb), blk(CB_CI),
                pl.BlockSpec((DEPTH, None, 1, LANE), lambda b, hd, t: (0, hd, 0, 0))]
    args = [p4, p4, p4, lower_bounds.reshape(DEPTH, C_HEADS, 1, LANE)]
    if finalize:
        in_specs += [blk(0), blk(CB_CGATE),
                     pl.BlockSpec((None, 1, LANE), lambda b, hd, t: (hd, 0, 0))]
        args += [o_fwd, p4, gain.reshape(C_HEADS, 1, LANE)]
    out_dtype = BF16 if finalize else F32
    return pl.pallas_call(
        functools.partial(_hgrn_body, layer=layer, reverse=reverse, tb=tb, finalize=finalize),
        grid=(bsz, C_HEADS, n_t),
        in_specs=in_specs,
        out_specs=blk(0),
        out_shape=jax.ShapeDtypeStruct((C_HEADS, bsz, seq, LANE), out_dtype),
        scratch_shapes=[pltpu.VMEM((LANE, LANE), F32),
                        pltpu.VMEM((C_CHUNK, LANE), F32),
                        pltpu.VMEM((C_CHUNK, LANE), F32),
                        pltpu.VMEM((C_CHUNK, LANE), F32),
                        pltpu.VMEM((C_CHUNK, C_CHUNK), F32)],
        compiler_params=_cparams(("parallel", "parallel", "arbitrary")),
        name="hgrn_scan_bwd" if reverse else "hgrn_scan_fwd",
    )(*args)


def _silu(g):
    return g / (1.0 + jnp.exp(-g))


def _sigmoid(g):
    return 1.0 / (1.0 + jnp.exp(-g))


def _lane_concat(ref, n):
    return jnp.concatenate([ref[i] for i in range(n)], axis=-1)


def _merge_out_body(*refs):
    (oa0_ref, oa1_ref, oa2_ref, l0_ref, l1_ref, l2_ref, ag_ref, ob_ref, bg0_ref, bg1_ref,
     yc_ref) = refs[:11]
    m_refs = refs[11:23]
    x_ref, wa_ref, wb_ref, wc_ref, wo_ref, pg_ref, out_ref = refs[23:]

    ya = []
    for s in range(A_SLOTS):
        l0, l1, l2 = l0_ref[s], l1_ref[s], l2_ref[s]
        mx = jnp.maximum(jnp.maximum(l0, l1), l2)
        e0, e1, e2 = jnp.exp(l0 - mx), jnp.exp(l1 - mx), jnp.exp(l2 - mx)
        mixed = (e0 * oa0_ref[s] + e1 * oa1_ref[s] + e2 * oa2_ref[s]) / (e0 + e1 + e2)
        ya.append((mixed * _silu(ag_ref[s])).astype(BF16))
    ya = jnp.concatenate(ya, axis=-1)
    yb = []
    for hd in range(B_Q_HEADS):
        g_ref = bg0_ref if hd < 4 else bg1_ref
        yb.append((ob_ref[hd] * _silu(g_ref[hd % 4])).astype(BF16))
    yb = jnp.concatenate(yb, axis=-1)
    yc = jnp.concatenate([yc_ref[hd] for hd in range(C_HEADS)], axis=-1)

    proj = (jnp.dot(ya, wa_ref[...], preferred_element_type=F32),
            jnp.dot(yb, wb_ref[...], preferred_element_type=F32),
            jnp.dot(yc, wc_ref[...], preferred_element_type=F32))
    per_branch = D_MODEL // (4 * LANE)
    y = []
    for j in range(per_branch):
        cols = slice(j * 4 * LANE, (j + 1) * 4 * LANE)
        acc = None
        for br in range(N_BRANCHES):
            gate = _sigmoid(_lane_concat(m_refs[br * per_branch + j], 4))
            term = gate * proj[br][:, cols]
            acc = term if acc is None else acc + term
        y.append(acc.astype(BF16))
    y = jnp.concatenate(y, axis=-1)
    out = jnp.dot(y, wo_ref[...], preferred_element_type=F32)
    out = out * lax.rsqrt(jnp.mean(out * out, axis=-1, keepdims=True) + NORM_EPS)
    out_ref[...] = x_ref[...] + out * pg_ref[...]


def _merge_out(x2, p3, oa, lse, ob, yc, wa, wb, wc, wo, post_gain, *, tm):
    rows, d = x2.shape

    def heads(n, first=0):
        return pl.BlockSpec((n, tm, LANE), lambda i: (first // n, i, 0))

    def const(shape):
        return pl.BlockSpec(shape, lambda i: (0,) * len(shape), pipeline_mode=pl.Buffered(1))

    n_merge = N_BRANCHES * D_MODEL // (4 * LANE)
    in_specs = ([heads(A_SLOTS)] * 6
                + [heads(A_SLOTS, CB_AGATE), heads(B_Q_HEADS),
                   heads(4, CB_BGATE), heads(4, CB_BGATE + 4), heads(C_HEADS)]
                + [heads(4, CB_MERGE + 4 * j) for j in range(n_merge)]
                + [pl.BlockSpec((tm, d), lambda i: (i, 0)),
                   const(wa.shape), const(wb.shape), const(wc.shape), const(wo.shape),
                   const((1, d))])
    args = (list(oa) + list(lse) + [p3, ob, p3, p3, yc] + [p3] * n_merge
            + [x2, wa, wb, wc, wo, post_gain.reshape(1, d)])
    return pl.pallas_call(
        _merge_out_body,
        grid=(rows // tm,),
        in_specs=in_specs,
        out_specs=pl.BlockSpec((tm, d), lambda i: (i, 0)),
        out_shape=jax.ShapeDtypeStruct((rows, d), F32),
        compiler_params=_cparams(("parallel",)),
        name="merge_out",
    )(*args)


def kernel(x, pre_norm, post_norm, w_in, sink_logits, hgrn_lower_bounds, hgrn_norm,
           w_branch_a, w_branch_b, w_branch_c, w_out):
    bsz, seq, d = x.shape
    rows = bsz * seq
    assert d == D_MODEL and w_in.shape[-1] == IN_COLS
    rope = _rope_tables(seq)
    x2 = x.reshape(rows, d)
    tm_in = min(1024, rows)
    tm_out = min(256, rows)
    for layer in range(DEPTH):
        p3 = _in_projection(x2, pre_norm[layer], w_in[layer].astype(BF16), tm=tm_in, tn=4 * LANE)
        p4 = p3.reshape(N_CB, bsz, seq, LANE)

        oa, lse = [], []
        for g, (window, dil) in enumerate(DIL_GROUPS):
            o_g, lse_g = _banded_attention(
                p4, rope, None, dil=dil, hw=window // (2 * dil),
                q_cb=CB_AQ + g * A_SLOTS, k_cb=CB_AK + g * A_SLOTS, v_cb=CB_AV + g * A_SLOTS,
                n_kv=A_SLOTS, n_q=1, want_lse=True, tq=256)
            oa.append(o_g.reshape(A_SLOTS, rows, LANE))
            lse.append(lse_g.reshape(A_SLOTS, rows, LANE))

        sink = jnp.broadcast_to(sink_logits[layer].astype(F32)[:, None, None], (B_Q_HEADS, 1, LANE))
        ob = _banded_attention(
            p4, rope, sink, dil=1, hw=B_HALF_WINDOW, q_cb=CB_BQ, k_cb=CB_BK, v_cb=CB_BV,
            n_kv=B_KV_HEADS, n_q=B_Q_HEADS // B_KV_HEADS, want_lse=False, tq=256)
        ob = ob.reshape(B_Q_HEADS, rows, LANE)

        o_fwd = _hgrn_scan(p4, hgrn_lower_bounds.astype(F32), layer=layer, reverse=False,
                           f_cb=CB_CFF, tb=512)
        yc = _hgrn_scan(p4, hgrn_lower_bounds.astype(F32), layer=layer, reverse=True,
                        f_cb=CB_CFB, tb=512, o_fwd=o_fwd, gain=hgrn_norm[layer].astype(F32))
        yc = yc.reshape(C_HEADS, rows, LANE)

        x2 = _merge_out(x2, p3, oa, lse, ob, yc,
                        w_branch_a[layer].astype(BF16), w_branch_b[layer].astype(BF16),
                        w_branch_c[layer].astype(BF16), w_out[layer].astype(BF16),
                        post_norm[layer].astype(F32), tm=tm_out)
    return x2.reshape(bsz, seq, d)
```

```python
import functools

import jax
import jax.numpy as jnp
from jax import lax
from jax.experimental import pallas as pl
from jax.experimental.pallas import tpu as pltpu

F32 = jnp.float32
BF16 = jnp.bfloat16

LANE = 128
D_MODEL = 2048
DEPTH = 2
HEAD_DIM = 128
ROT_DIM = HEAD_DIM // 4
ROT_HALF = ROT_DIM // 2
ROPE_THETA = 500000.0
NORM_EPS = 1e-6

DIL_GROUPS = ((128, 1), (512, 4), (2048, 16))
A_SLOTS = 4
A_HEADS = A_SLOTS * len(DIL_GROUPS)
A_WIDTH = A_SLOTS * HEAD_DIM
B_Q_HEADS = 8
B_KV_HEADS = 2
B_HALF_WINDOW = 128
B_WIDTH = B_Q_HEADS * HEAD_DIM
C_HEADS = 8
C_WIDTH = C_HEADS * HEAD_DIM
C_CHUNK = 64
N_BRANCHES = 3

CB_AQ = 0
CB_AK = CB_AQ + A_HEADS
CB_AV = CB_AK + A_HEADS
CB_AGATE = CB_AV + A_HEADS
CB_BQ = CB_AGATE + A_SLOTS
CB_BK = CB_BQ + B_Q_HEADS
CB_BV = CB_BK + B_KV_HEADS
CB_BGATE = CB_BV + B_KV_HEADS
CB_CQ = CB_BGATE + B_Q_HEADS
CB_CFF = CB_CQ + C_HEADS
CB_CFB = CB_CFF + C_HEADS
CB_CI = CB_CFB + C_HEADS
CB_CGATE = CB_CI + C_HEADS
CB_MERGE = CB_CGATE + C_HEADS
N_CB = CB_MERGE + N_BRANCHES * D_MODEL // LANE
IN_COLS = N_CB * LANE

MASK_VALUE = -1e30
HGRN_FACTOR_RANGE = 80.0

VMEM_LIMIT = 56 * 1024 * 1024


def _cparams(sem):
    return pltpu.CompilerParams(dimension_semantics=sem, vmem_limit_bytes=VMEM_LIMIT)


def _in_projection_body(x_ref, g_ref, w_ref, o_ref, xn_ref, *, n_blk):
    @pl.when(pl.program_id(1) == 0)
    def _():
        x = x_ref[...]
        y = x * lax.rsqrt(jnp.mean(x * x, axis=-1, keepdims=True) + NORM_EPS)
        xn_ref[...] = (y * g_ref[...]).astype(BF16)

    acc = jnp.dot(xn_ref[...], w_ref[...], preferred_element_type=F32)
    for c in range(n_blk):
        o_ref[c] = acc[:, c * LANE:(c + 1) * LANE]


def _in_projection(x2, gain, w_bf16, *, tm, tn):
    rows, d = x2.shape
    cols = w_bf16.shape[1]
    n_blk = tn // LANE
    return pl.pallas_call(
        functools.partial(_in_projection_body, n_blk=n_blk),
        grid=(rows // tm, cols // tn),
        in_specs=[
            pl.BlockSpec((tm, d), lambda i, j: (i, 0)),
            pl.BlockSpec((1, d), lambda i, j: (0, 0)),
            pl.BlockSpec((d, tn), lambda i, j: (0, j)),
        ],
        out_specs=pl.BlockSpec((n_blk, tm, LANE), lambda i, j: (j, i, 0)),
        out_shape=jax.ShapeDtypeStruct((cols // LANE, rows, LANE), F32),
        scratch_shapes=[pltpu.VMEM((tm, d), BF16)],
        compiler_params=_cparams(("parallel", "arbitrary")),
        name="in_projection",
    )(x2, gain.reshape(1, d), w_bf16)


def _rope_tables(seq):
    pos = jnp.arange(seq, dtype=F32)
    inv_freq = ROPE_THETA ** (-jnp.arange(0, ROT_DIM, 2, dtype=F32) / ROT_DIM)
    ang = pos[:, None] * inv_freq[None, :]
    cos, sin = jnp.cos(ang), jnp.sin(ang)
    rest = HEAD_DIM - ROT_DIM
    c_full = jnp.concatenate([cos, cos, jnp.ones((seq, rest), F32)], axis=-1)
    s_full = jnp.concatenate([-sin, sin, jnp.zeros((seq, rest), F32)], axis=-1)
    return jnp.stack([c_full, s_full])


def _apply_rope(t, cos_full, sin_signed):
    lane = lax.broadcasted_iota(jnp.int32, t.shape, 1)
    swapped = jnp.where(lane < ROT_HALF,
                        pltpu.roll(t, LANE - ROT_HALF, 1),
                        pltpu.roll(t, ROT_HALF, 1))
    return t * cos_full + swapped * sin_signed


def _attention_body(*refs, dil, hw, tq, n_sub, n_q, has_sink, want_lse):
    q_ref, kp_ref, kc_ref, kn_ref, vp_ref, vc_ref, vn_ref, rq_ref, rp_ref, rn_ref = refs[:10]
    rest = list(refs[10:])
    sink_ref = rest.pop(0) if has_sink else None
    o_ref = rest.pop(0)
    lse_ref = rest.pop(0) if want_lse else None

    t = pl.program_id(2)
    nk = tq + 2 * hw
    qi = lax.broadcasted_iota(jnp.int32, (tq, nk), 0)
    kj = lax.broadcasted_iota(jnp.int32, (tq, nk), 1)
    kpos = t * tq - hw + kj
    valid = (jnp.abs(kj - hw - qi) <= hw) & (kpos >= 0) & (kpos < n_sub)
    scale = HEAD_DIM ** -0.5

    def rows(n, ph):
        return slice(None) if dil == 1 else pl.ds(ph, n, stride=dil)

    def roped(ref, rope_ref, n, ph, lead=()):
        r = rows(n, ph)
        return _apply_rope(ref[lead + (r, slice(None))], rope_ref[0, r, :], rope_ref[1, r, :])

    def phase(ph, carry):
        k = jnp.concatenate([roped(kp_ref, rp_ref, hw, ph),
                             roped(kc_ref, rq_ref, tq, ph),
                             roped(kn_ref, rn_ref, hw, ph)], axis=0).astype(BF16)
        v = jnp.concatenate([vp_ref[rows(hw, ph), :],
                             vc_ref[rows(tq, ph), :],
                             vn_ref[rows(hw, ph), :]], axis=0).astype(BF16)
        for g in range(n_q):
            q = (roped(q_ref, rq_ref, tq, ph, lead=(g,)) * scale).astype(BF16)
            s = lax.dot_general(q, k, (((1,), (1,)), ((), ())), preferred_element_type=F32)
            s = jnp.where(valid, s, MASK_VALUE)
            m = jnp.max(s, axis=-1, keepdims=True)
            if has_sink:
                sink = sink_ref[g][:, :1]
                m = jnp.maximum(m, sink)
            p = jnp.exp(s - m)
            denom = jnp.sum(p, axis=-1, keepdims=True)
            if has_sink:
                denom = denom + jnp.exp(sink - m)
            o = jnp.dot(p.astype(BF16), v, preferred_element_type=F32) / denom
            o_ref[g, rows(tq, ph), :] = o
            if want_lse:
                lse_ref[g, rows(tq, ph), :] = jnp.broadcast_to(m + jnp.log(denom), (tq, LANE))
        return carry

    if dil == 1:
        phase(0, 0)
    else:
        lax.fori_loop(0, dil, phase, 0)


def _banded_attention(p4, rope, sink, *, dil, hw, q_cb, k_cb, v_cb, n_kv, n_q, want_lse, tq):
    _, bsz, seq, _ = p4.shape
    n_sub = seq // dil
    tq = min(tq, n_sub)
    w = tq * dil
    h = hw * dil
    per = w // h
    n_halo = seq // h
    assert q_cb % n_q == 0 and seq % w == 0 and w % h == 0
    has_sink = sink is not None

    def cur(cb):
        return pl.BlockSpec((None, None, w, LANE), lambda b, hd, t: (cb + hd, b, t, 0))

    def prev(cb):
        return pl.BlockSpec((None, None, h, LANE),
                            lambda b, hd, t: (cb + hd, b, jnp.maximum(t * per - 1, 0), 0))

    def nxt(cb):
        return pl.BlockSpec((None, None, h, LANE),
                            lambda b, hd, t: (cb + hd, b, jnp.minimum((t + 1) * per, n_halo - 1), 0))

    in_specs = [
        pl.BlockSpec((n_q, None, w, LANE), lambda b, hd, t: (q_cb // n_q + hd, b, t, 0)),
        prev(k_cb), cur(k_cb), nxt(k_cb),
        prev(v_cb), cur(v_cb), nxt(v_cb),
        pl.BlockSpec((2, w, LANE), lambda b, hd, t: (0, t, 0)),
        pl.BlockSpec((2, h, LANE), lambda b, hd, t: (0, jnp.maximum(t * per - 1, 0), 0)),
        pl.BlockSpec((2, h, LANE), lambda b, hd, t: (0, jnp.minimum((t + 1) * per, n_halo - 1), 0)),
    ]
    args = [p4, p4, p4, p4, p4, p4, p4, rope, rope, rope]
    if has_sink:
        in_specs.append(pl.BlockSpec((n_q, 1, LANE), lambda b, hd, t: (hd, 0, 0)))
        args.append(sink)
    o_spec = pl.BlockSpec((n_q, None, w, LANE), lambda b, hd, t: (hd, b, t, 0))
    o_shape = jax.ShapeDtypeStruct((n_kv * n_q, bsz, seq, LANE), F32)
    out_specs = [o_spec, o_spec] if want_lse else o_spec
    out_shape = [o_shape, o_shape] if want_lse else o_shape
    return pl.pallas_call(
        functools.partial(_attention_body, dil=dil, hw=hw, tq=tq, n_sub=n_sub, n_q=n_q,
                          has_sink=has_sink, want_lse=want_lse),
        grid=(bsz, n_kv, seq // w),
        in_specs=in_specs,
        out_specs=out_specs,
        out_shape=out_shape,
        compiler_params=_cparams(("parallel", "parallel", "arbitrary")),
        name=f"banded_attention_d{dil}_w{hw}",
    )(*args)


def _hgrn_body(*refs, layer, reverse, tb, finalize):
    q_ref, f_ref, v_ref, lbp_ref = refs[:4]
    rest = list(refs[4:])
    if finalize:
        of_ref, gate_ref, gain_ref = rest[:3]
        rest = rest[3:]
    o_ref, st_ref, qs_ref, ks_ref, bs_ref, sc_ref = rest
    c = C_CHUNK
    n = tb // c

    @pl.when(pl.program_id(2) == 0)
    def _():
        st_ref[...] = jnp.zeros_like(st_ref)

    lbp = lbp_ref[...]
    e_l = jnp.exp(lbp - jnp.max(lbp, axis=0, keepdims=True))
    sm = e_l / jnp.sum(e_l, axis=0, keepdims=True)
    lb = jnp.zeros((1, LANE), F32)
    for i in range(1, layer + 1):
        lb = lb + sm[i]
    one_m_lb = 1.0 - lb
    tiny = jnp.float32(1e-37)
    log_lb = jnp.where(lb > 0.0, jnp.log(jnp.maximum(lb, tiny)), -jnp.inf)
    log_1m_lb = jnp.where(one_m_lb > 0.0, jnp.log(jnp.maximum(one_m_lb, tiny)), -jnp.inf)

    row = lax.broadcasted_iota(jnp.int32, (c, c), 0)
    col = lax.broadcasted_iota(jnp.int32, (c, c), 1)
    causal = (row <= col) if reverse else (row >= col)
    tri = jnp.where(causal, 1.0, 0.0).astype(BF16)

    qz = q_ref[...]
    fz = f_ref[...]
    v3 = v_ref[...].astype(BF16).reshape(n, c, LANE)
    e = jnp.exp(-jnp.abs(fz))
    log_sig = jnp.minimum(fz, 0.0) - jnp.log(1.0 + e)
    sig_neg = jnp.where(fz >= 0.0, e, 1.0) / (1.0 + e)
    cand = log_1m_lb + log_sig
    log_f = jnp.maximum(log_lb, cand) + jnp.log(1.0 + jnp.exp(-jnp.abs(log_lb - cand)))
    kk = one_m_lb * sig_neg
    qh = qz / (1.0 + jnp.exp(-qz))

    def widen(a):
        return jnp.concatenate([a[i * c:(i + 1) * c] for i in range(n)], axis=1)

    hi = log_f.astype(BF16)
    r1 = log_f - hi.astype(F32)
    mid = r1.astype(BF16)
    lo = (r1 - mid.astype(F32)).astype(BF16)
    b_wide = (jnp.dot(tri, widen(hi), preferred_element_type=F32)
              + jnp.dot(tri, widen(mid), preferred_element_type=F32)
              + jnp.dot(tri, widen(lo), preferred_element_type=F32))
    b3 = jnp.stack([b_wide[:, i * LANE:(i + 1) * LANE] for i in range(n)])
    b_end = b3[:, 0:1] if reverse else b3[:, c - 1:c]
    d3 = b3 - b3[:, c // 2 - 1:c // 2]
    span = jnp.max(jnp.abs(d3))
    qh3 = qh.reshape(n, c, LANE)
    kk3 = kk.reshape(n, c, LANE)

    @pl.when(span <= HGRN_FACTOR_RANGE)
    def _():
        qt = (qh3 * jnp.exp(d3)).astype(BF16)
        kt = (kk3 * jnp.exp(-d3)).astype(BF16)
        sc_ref[...] = jnp.einsum('ntk,nsk->nts', qt, kt, preferred_element_type=F32)

    @pl.when(span > HGRN_FACTOR_RANGE)
    def _():
        qs_ref[...] = qh3
        ks_ref[...] = kk3
        bs_ref[...] = b3

        def per_chunk(i, carry):
            def pair(s, sc):
                rel = jnp.minimum(bs_ref[i] - bs_ref[i, pl.ds(s, 1), :], 0.0)
                w = qs_ref[i] * ks_ref[i, pl.ds(s, 1), :] * jnp.exp(rel)
                return jnp.where(col == s, jnp.sum(w, axis=-1, keepdims=True), sc)

            sc_ref[i] = lax.fori_loop(0, c, pair, jnp.zeros((c, c), F32))
            return carry

        lax.fori_loop(0, n, per_chunk, 0)

    scores = jnp.where(causal, sc_ref[...], 0.0).astype(BF16)
    k_in = (kk3 * jnp.exp(b_end - b3)).astype(BF16)
    kv = jnp.einsum('nsv,nsk->nvk', v3, k_in, preferred_element_type=F32)
    decay_end = jnp.exp(b_end)
    st = st_ref[...]
    incoming = [None] * n
    for i in (range(n - 1, -1, -1) if reverse else range(n)):
        incoming[i] = st.astype(BF16)
        st = st * decay_end[i] + kv[i]
    st_ref[...] = st
    o = (jnp.einsum('nts,nsv->ntv', scores, v3, preferred_element_type=F32)
         + jnp.einsum('ntk,nvk->ntv', (qh3 * jnp.exp(b3)).astype(BF16), jnp.stack(incoming),
                      preferred_element_type=F32))
    o = o.reshape(tb, LANE)
    if finalize:
        o = o + of_ref[...]
        o = o * lax.rsqrt(jnp.mean(o * o, axis=-1, keepdims=True) + NORM_EPS)
        g = gate_ref[...]
        o = o * gain_ref[...] * (g / (1.0 + jnp.exp(-g)))
    o_ref[...] = o.astype(o_ref.dtype)


def _hgrn_scan(p4, lower_bounds, *, layer, reverse, f_cb, tb, o_fwd=None, gain=None):
    _, bsz, seq, _ = p4.shape
    tb = min(tb, seq)
    n_t = seq // tb
    n_chunks = tb // C_CHUNK
    finalize = o_fwd is not None

    def tt(t):
        return (n_t - 1 - t) if reverse else t

    def blk(cb):
        return pl.BlockSpec((None, None, tb, LANE), lambda b, hd, t: (cb + hd, b, tt(t), 0))

    in_specs = [blk(CB_CQ), blk(f_cb), blk(CB_CI),
                pl.BlockSpec((DEPTH, None, 1, LANE), lambda b, hd, t: (0, hd, 0, 0))]
    args = [p4, p4, p4, lower_bounds.reshape(DEPTH, C_HEADS, 1, LANE)]
    if finalize:
        in_specs += [blk(0), blk(CB_CGATE),
                     pl.BlockSpec((None, 1, LANE), lambda b, hd, t: (hd, 0, 0))]
        args += [o_fwd, p4, gain.reshape(C_HEADS, 1, LANE)]
    out_dtype = BF16 if finalize else F32
    return pl.pallas_call(
        functools.partial(_hgrn_body, layer=layer, reverse=reverse, tb=tb, finalize=finalize),
        grid=(bsz, C_HEADS, n_t),
        in_specs=in_specs,
        out_specs=blk(0),
        out_shape=jax.ShapeDtypeStruct((C_HEADS, bsz, seq, LANE), out_dtype),
        scratch_shapes=[pltpu.VMEM((LANE, LANE), F32),
                        pltpu.VMEM((n_chunks, C_CHUNK, LANE), F32),
                        pltpu.VMEM((n_chunks, C_CHUNK, LANE), F32),
                        pltpu.VMEM((n_chunks, C_CHUNK, LANE), F32),
                        pltpu.VMEM((n_chunks, C_CHUNK, C_CHUNK), F32)],
        compiler_params=_cparams(("parallel", "parallel", "arbitrary")),
        name="hgrn_scan_bwd" if reverse else "hgrn_scan_fwd",
    )(*args)


def _silu(g):
    return g / (1.0 + jnp.exp(-g))


def _sigmoid(g):
    return 1.0 / (1.0 + jnp.exp(-g))


def _lane_concat(ref, n):
    return jnp.concatenate([ref[i] for i in range(n)], axis=-1)


def _merge_out_body(*refs):
    (oa0_ref, oa1_ref, oa2_ref, l0_ref, l1_ref, l2_ref, ag_ref, ob_ref, bg0_ref, bg1_ref,
     yc_ref) = refs[:11]
    m_refs = refs[11:23]
    x_ref, wa_ref, wb_ref, wc_ref, wo_ref, pg_ref, out_ref = refs[23:]

    ya = []
    for s in range(A_SLOTS):
        l0, l1, l2 = l0_ref[s], l1_ref[s], l2_ref[s]
        mx = jnp.maximum(jnp.maximum(l0, l1), l2)
        e0, e1, e2 = jnp.exp(l0 - mx), jnp.exp(l1 - mx), jnp.exp(l2 - mx)
        mixed = (e0 * oa0_ref[s] + e1 * oa1_ref[s] + e2 * oa2_ref[s]) / (e0 + e1 + e2)
        ya.append((mixed * _silu(ag_ref[s])).astype(BF16))
    ya = jnp.concatenate(ya, axis=-1)
    yb = []
    for hd in range(B_Q_HEADS):
        g_ref = bg0_ref if hd < 4 else bg1_ref
        yb.append((ob_ref[hd] * _silu(g_ref[hd % 4])).astype(BF16))
    yb = jnp.concatenate(yb, axis=-1)
    yc = jnp.concatenate([yc_ref[hd] for hd in range(C_HEADS)], axis=-1)

    proj = (jnp.dot(ya, wa_ref[...], preferred_element_type=F32),
            jnp.dot(yb, wb_ref[...], preferred_element_type=F32),
            jnp.dot(yc, wc_ref[...], preferred_element_type=F32))
    per_branch = D_MODEL // (4 * LANE)
    y = []
    for j in range(per_branch):
        cols = slice(j * 4 * LANE, (j + 1) * 4 * LANE)
        acc = None
        for br in range(N_BRANCHES):
            gate = _sigmoid(_lane_concat(m_refs[br * per_branch + j], 4))
            term = gate * proj[br][:, cols]
            acc = term if acc is None else acc + term
        y.append(acc.astype(BF16))
    y = jnp.concatenate(y, axis=-1)
    out = jnp.dot(y, wo_ref[...], preferred_element_type=F32)
    out = out * lax.rsqrt(jnp.mean(out * out, axis=-1, keepdims=True) + NORM_EPS)
    out_ref[...] = x_ref[...] + out * pg_ref[...]


def _merge_out(x2, p3, oa, lse, ob, yc, wa, wb, wc, wo, post_gain, *, tm):
    rows, d = x2.shape

    def heads(n, first=0):
        return pl.BlockSpec((n, tm, LANE), lambda i: (first // n, i, 0))

    def const(shape):
        return pl.BlockSpec(shape, lambda i: (0,) * len(shape), pipeline_mode=pl.Buffered(1))

    n_merge = N_BRANCHES * D_MODEL // (4 * LANE)
    in_specs = ([heads(A_SLOTS)] * 6
                + [heads(A_SLOTS, CB_AGATE), heads(B_Q_HEADS),
                   heads(4, CB_BGATE), heads(4, CB_BGATE + 4), heads(C_HEADS)]
                + [heads(4, CB_MERGE + 4 * j) for j in range(n_merge)]
                + [pl.BlockSpec((tm, d), lambda i: (i, 0)),
                   const(wa.shape), const(wb.shape), const(wc.shape), const(wo.shape),
                   const((1, d))])
    args = (list(oa) + list(lse) + [p3, ob, p3, p3, yc] + [p3] * n_merge
            + [x2, wa, wb, wc, wo, post_gain.reshape(1, d)])
    return pl.pallas_call(
        _merge_out_body,
        grid=(rows // tm,),
        in_specs=in_specs,
        out_specs=pl.BlockSpec((tm, d), lambda i: (i, 0)),
        out_shape=jax.ShapeDtypeStruct((rows, d), F32),
        compiler_params=_cparams(("parallel",)),
        name="merge_out",
    )(*args)


def kernel(x, pre_norm, post_norm, w_in, sink_logits, hgrn_lower_bounds, hgrn_norm,
           w_branch_a, w_branch_b, w_branch_c, w_out):
    bsz, seq, d = x.shape
    rows = bsz * seq
    assert d == D_MODEL and w_in.shape[-1] == IN_COLS
    rope = _rope_tables(seq)
    x2 = x.reshape(rows, d)
    tm_in = min(1024, rows)
    tm_out = min(256, rows)
    for layer in range(DEPTH):
        p3 = _in_projection(x2, pre_norm[layer], w_in[layer].astype(BF16), tm=tm_in, tn=4 * LANE)
        p4 = p3.reshape(N_CB, bsz, seq, LANE)

        oa, lse = [], []
        for g, (window, dil) in enumerate(DIL_GROUPS):
            o_g, lse_g = _banded_attention(
                p4, rope, None, dil=dil, hw=window // (2 * dil),
                q_cb=CB_AQ + g * A_SLOTS, k_cb=CB_AK + g * A_SLOTS, v_cb=CB_AV + g * A_SLOTS,
                n_kv=A_SLOTS, n_q=1, want_lse=True, tq=256)
            oa.append(o_g.reshape(A_SLOTS, rows, LANE))
            lse.append(lse_g.reshape(A_SLOTS, rows, LANE))

        sink = jnp.broadcast_to(sink_logits[layer].astype(F32)[:, None, None], (B_Q_HEADS, 1, LANE))
        ob = _banded_attention(
            p4, rope, sink, dil=1, hw=B_HALF_WINDOW, q_cb=CB_BQ, k_cb=CB_BK, v_cb=CB_BV,
            n_kv=B_KV_HEADS, n_q=B_Q_HEADS // B_KV_HEADS, want_lse=False, tq=256)
        ob = ob.reshape(B_Q_HEADS, rows, LANE)

        o_fwd = _hgrn_scan(p4, hgrn_lower_bounds.astype(F32), layer=layer, reverse=False,
                           f_cb=CB_CFF, tb=512)
        yc = _hgrn_scan(p4, hgrn_lower_bounds.astype(F32), layer=layer, reverse=True,
                        f_cb=CB_CFB, tb=512, o_fwd=o_fwd, gain=hgrn_norm[layer].astype(F32))
        yc = yc.reshape(C_HEADS, rows, LANE)

        x2 = _merge_out(x2, p3, oa, lse, ob, yc,
                        w_branch_a[layer].astype(BF16), w_branch_b[layer].astype(BF16),
                        w_branch_c[layer].astype(BF16), w_out[layer].astype(BF16),
                        post_norm[layer].astype(F32), tm=tm_out)
    return x2.reshape(bsz, seq, d)
```

```python
import functools

import jax
import jax.numpy as jnp
from jax import lax
from jax.experimental import pallas as pl
from jax.experimental.pallas import tpu as pltpu

F32 = jnp.float32
BF16 = jnp.bfloat16

LANE = 128
D_MODEL = 2048
DEPTH = 2
HEAD_DIM = 128
ROT_DIM = HEAD_DIM // 4
ROT_HALF = ROT_DIM // 2
ROPE_THETA = 500000.0
NORM_EPS = 1e-6

DIL_GROUPS = ((128, 1), (512, 4), (2048, 16))
A_SLOTS = 4
A_HEADS = A_SLOTS * len(DIL_GROUPS)
B_Q_HEADS = 8
B_KV_HEADS = 2
B_HALF_WINDOW = 128
C_HEADS = 8
C_CHUNK = 64
N_BRANCHES = 3

CB_AQ = 0
CB_AK = CB_AQ + A_HEADS
CB_AV = CB_AK + A_HEADS
CB_AGATE = CB_AV + A_HEADS
CB_BQ = CB_AGATE + A_SLOTS
CB_BK = CB_BQ + B_Q_HEADS
CB_BV = CB_BK + B_KV_HEADS
CB_BGATE = CB_BV + B_KV_HEADS
CB_CQ = CB_BGATE + B_Q_HEADS
CB_CFF = CB_CQ + C_HEADS
CB_CFB = CB_CFF + C_HEADS
CB_CI = CB_CFB + C_HEADS
CB_CGATE = CB_CI + C_HEADS
CB_MERGE = CB_CGATE + C_HEADS
N_CB = CB_MERGE + N_BRANCHES * D_MODEL // LANE
IN_COLS = N_CB * LANE
TILE_CB = 4

SEG_A, SEG_F, SEG_C = (0, CB_CFF), (CB_CFF, CB_CI), (CB_CI, N_CB)

PROJ_TILE_ROWS = 2048
NO_ROPE, ROPE_K, ROPE_Q = 0, 1, 2

MASK_VALUE = -1e30
HGRN_FACTOR_RANGE = 80.0

VMEM_LIMIT = 56 * 1024 * 1024


def _cparams(sem):
    return pltpu.CompilerParams(dimension_semantics=sem, vmem_limit_bytes=VMEM_LIMIT)


def _projection_plans():
    codes = [NO_ROPE] * SEG_A[1]
    dils = [1] * SEG_A[1]
    for g, (_, dil) in enumerate(DIL_GROUPS):
        for s in range(A_SLOTS):
            codes[CB_AQ + g * A_SLOTS + s] = ROPE_Q
            codes[CB_AK + g * A_SLOTS + s] = ROPE_K
            for base in (CB_AQ, CB_AK, CB_AV):
                dils[base + g * A_SLOTS + s] = dil
    for h in range(B_Q_HEADS):
        codes[CB_BQ + h] = ROPE_Q
    for h in range(B_KV_HEADS):
        codes[CB_BK + h] = ROPE_K
    plans = []
    for t in range(SEG_A[1] // TILE_CB):
        sl = slice(t * TILE_CB, (t + 1) * TILE_CB)
        assert len(set(dils[sl])) == 1
        plans.append((tuple(codes[sl]), dils[sl][0]))
    return tuple(plans)


def _rms(x, gain):
    return x * lax.rsqrt(jnp.mean(x * x, axis=-1, keepdims=True) + NORM_EPS) * gain


def _pre_norm_body(x_ref, g_ref, o_ref):
    o_ref[...] = _rms(x_ref[...], g_ref[...]).astype(o_ref.dtype)


def _pre_norm(x2, gain, *, tm):
    rows, d = x2.shape
    return pl.pallas_call(
        _pre_norm_body,
        grid=(rows // tm,),
        in_specs=[pl.BlockSpec((tm, d), lambda i: (i, 0)), pl.BlockSpec((1, d), lambda i: (0, 0))],
        out_specs=pl.BlockSpec((tm, d), lambda i: (i, 0)),
        out_shape=jax.ShapeDtypeStruct((rows, d), BF16),
        compiler_params=_cparams(("parallel",)),
        name="pre_norm",
    )(x2, gain.reshape(1, d))


def _rope_tables(seq):
    pos = jnp.arange(seq, dtype=F32)
    inv_freq = ROPE_THETA ** (-jnp.arange(0, ROT_DIM, 2, dtype=F32) / ROT_DIM)
    ang = pos[:, None] * inv_freq[None, :]
    cos, sin = jnp.cos(ang), jnp.sin(ang)
    rest = HEAD_DIM - ROT_DIM
    c_full = jnp.concatenate([cos, cos, jnp.ones((seq, rest), F32)], axis=-1)
    s_full = jnp.concatenate([-sin, sin, jnp.zeros((seq, rest), F32)], axis=-1)
    return jnp.stack([c_full, s_full])


def _apply_rope(t, cos_full, sin_signed):
    lane = lax.broadcasted_iota(jnp.int32, t.shape, 1)
    swapped = jnp.where(lane < ROT_HALF,
                        pltpu.roll(t, LANE - ROT_HALF, 1),
                        pltpu.roll(t, ROT_HALF, 1))
    return t * cos_full + swapped * sin_signed


def _in_projection_body(*refs, plans, tm):
    if plans is None:
        x_ref, w_ref, o_ref = refs
    else:
        x_ref, w_ref, rope_ref, o_ref, stage_ref = refs
    acc = jnp.dot(x_ref[...], w_ref[...], preferred_element_type=F32)

    def block(c):
        return acc[:, c * LANE:(c + 1) * LANE]

    if plans is None:
        for c in range(TILE_CB):
            o_ref[c] = block(c).astype(o_ref.dtype)
        return

    j = pl.program_id(1)
    groups = {}
    for tile, plan in enumerate(plans):
        groups.setdefault(plan, []).append(tile)
    scale = HEAD_DIM ** -0.5
    for (codes, dil), tiles in groups.items():
        cond = functools.reduce(jnp.logical_or, [j == t for t in tiles])

        @pl.when(cond)
        def _(codes=codes, dil=dil):
            for c in range(TILE_CB):
                t = block(c)
                if codes[c] != NO_ROPE:
                    t = _apply_rope(t, rope_ref[0], rope_ref[1])
                if codes[c] == ROPE_Q:
                    t = t * scale
                if dil == 1:
                    o_ref[c] = t.astype(o_ref.dtype)
                else:
                    stage_ref[c] = t
            if dil > 1:
                cs = tm // dil
                for c in range(TILE_CB):
                    def phase(p, carry, c=c):
                        dst = pl.ds(pl.multiple_of(p * cs, cs), cs)
                        src = stage_ref[c, pl.ds(p, cs, stride=dil), :]
                        o_ref[c, dst, :] = src.astype(o_ref.dtype)
                        return carry

                    lax.fori_loop(0, dil, phase, 0)


def _in_projection(xn, w_bf16, rope, *, seq, seg, out_dtype, plans, tm):
    rows, d = xn.shape
    cb0, cb1 = seg
    n_tiles = (cb1 - cb0) // TILE_CB
    tn = TILE_CB * LANE
    tile0 = cb0 // TILE_CB
    in_specs = [pl.BlockSpec((tm, d), lambda i, j: (i, 0)),
                pl.BlockSpec((d, tn), lambda i, j: (0, tile0 + j))]
    args = [xn, w_bf16]
    scratch = []
    if plans is not None:
        per_seq = seq // tm
        in_specs.append(pl.BlockSpec((2, tm, LANE), lambda i, j: (0, i % per_seq, 0)))
        args.append(rope)
        scratch.append(pltpu.VMEM((TILE_CB, tm, LANE), F32))
    return pl.pallas_call(
        functools.partial(_in_projection_body, plans=plans, tm=tm),
        grid=(rows // tm, n_tiles),
        in_specs=in_specs,
        out_specs=pl.BlockSpec((TILE_CB, tm, LANE), lambda i, j: (j, i, 0)),
        out_shape=jax.ShapeDtypeStruct((cb1 - cb0, rows, LANE), out_dtype),
        scratch_shapes=scratch,
        compiler_params=_cparams(("parallel", "arbitrary")),
        name=f"in_projection_{cb0}",
    )(*args)


def _attention_body(*refs, dil, hw, cs, qb, n_sub, n_q, has_sink, want_lse, unroll):
    q_ref, kp_ref, kc_ref, kn_ref, vp_ref, vc_ref, vn_ref = refs[:7]
    rest = list(refs[7:])
    sink_ref = rest.pop(0) if has_sink else None
    o_ref = rest.pop(0)
    lse_ref = rest.pop(0) if want_lse else None
    kbuf, vbuf, band_ref, kbias_ref = rest

    t = pl.program_id(2)
    nk = qb + 2 * hw
    nks = cs + 2 * hw
    n_u = cs // qb
    for buf, p_ref, c_ref, n_ref in ((kbuf, kp_ref, kc_ref, kn_ref), (vbuf, vp_ref, vc_ref, vn_ref)):
        buf[:, 0:hw] = p_ref[...]
        buf[:, hw:hw + cs] = c_ref[...]
        buf[:, hw + cs:] = n_ref[...]

    qi = lax.broadcasted_iota(jnp.int32, (qb, nk), 0)
    kj = lax.broadcasted_iota(jnp.int32, (qb, nk), 1)
    band_ref[...] = jnp.where(jnp.abs(kj - hw - qi) <= hw, 0.0, MASK_VALUE)
    kpos = t * cs - hw + lax.broadcasted_iota(jnp.int32, (1, nks), 1)
    kbias_ref[...] = jnp.where((kpos >= 0) & (kpos < n_sub), 0.0, MASK_VALUE)

    def block(idx, carry):
        g = idx % n_q
        u = (idx // n_q) % n_u
        ph = idx // (n_q * n_u)
        u0 = pl.multiple_of(u * qb, qb)
        q = q_ref[g, ph, pl.ds(u0, qb), :]
        k = kbuf[ph, pl.ds(u0, nk), :]
        v = vbuf[ph, pl.ds(u0, nk), :]
        s = lax.dot_general(q, k, (((1,), (1,)), ((), ())), preferred_element_type=F32)
        s = s + band_ref[...] + kbias_ref[:, pl.ds(u0, nk)]
        m = jnp.max(s, axis=-1, keepdims=True)
        if has_sink:
            sink = sink_ref[g][:, :1]
            m = jnp.maximum(m, sink)
        p = jnp.exp(s - m)
        denom = jnp.sum(p, axis=-1, keepdims=True)
        if has_sink:
            denom = denom + jnp.exp(sink - m)
        o = jnp.dot(p.astype(BF16), v, preferred_element_type=F32) / denom
        rows = pl.ds(u0, qb) if dil == 1 else pl.ds(ph + u * (qb * dil), qb, stride=dil)
        o_ref[g, rows, :] = o
        if want_lse:
            lse_ref[g, rows, :] = jnp.broadcast_to(m + jnp.log(denom), (qb, LANE))
        return carry

    lax.fori_loop(0, dil * n_u * n_q, block, 0, unroll=unroll)


def _banded_attention(pa, sink, *, seq, tile, dil, hw, qb, q_cb, k_cb, v_cb, n_kv, n_q, want_lse,
                      unroll=4):
    n_cb, rows, _ = pa.shape
    bsz = rows // seq
    n_t = seq // tile
    cs = tile // dil
    per = cs // hw
    assert q_cb % n_q == 0 and cs % qb == 0 and cs % hw == 0
    view = pa.reshape(n_cb, bsz, n_t, dil, cs, LANE)
    has_sink = sink is not None

    def cur(cb, n=None):
        return pl.BlockSpec((n, None, None, dil, cs, LANE),
                            lambda b, hd, t: ((cb if n is None else cb // n) + hd, b, t, 0, 0, 0))

    def prev(cb):
        return pl.BlockSpec((None, None, None, dil, hw, LANE),
                            lambda b, hd, t: (cb + hd, b, jnp.maximum(t - 1, 0), 0, per - 1, 0))

    def nxt(cb):
        return pl.BlockSpec((None, None, None, dil, hw, LANE),
                            lambda b, hd, t: (cb + hd, b, jnp.minimum(t + 1, n_t - 1), 0, 0, 0))

    in_specs = [cur(q_cb, n_q), prev(k_cb), cur(k_cb), nxt(k_cb), prev(v_cb), cur(v_cb), nxt(v_cb)]
    args = [view] * 7
    if has_sink:
        in_specs.append(pl.BlockSpec((n_q, 1, LANE), lambda b, hd, t: (hd, 0, 0)))
        args.append(sink)
    o_spec = pl.BlockSpec((n_q, None, tile, LANE), lambda b, hd, t: (hd, b, t, 0))
    o_shape = jax.ShapeDtypeStruct((n_kv * n_q, bsz, seq, LANE), F32)
    return pl.pallas_call(
        functools.partial(_attention_body, dil=dil, hw=hw, cs=cs, qb=qb, n_sub=seq // dil, n_q=n_q,
                          has_sink=has_sink, want_lse=want_lse, unroll=unroll),
        grid=(bsz, n_kv, n_t),
        in_specs=in_specs,
        out_specs=[o_spec, o_spec] if want_lse else o_spec,
        out_shape=[o_shape, o_shape] if want_lse else o_shape,
        scratch_shapes=[pltpu.VMEM((dil, cs + 2 * hw, LANE), BF16),
                        pltpu.VMEM((dil, cs + 2 * hw, LANE), BF16),
                        pltpu.VMEM((qb, qb + 2 * hw), F32),
                        pltpu.VMEM((1, cs + 2 * hw), F32)],
        compiler_params=_cparams(("parallel", "parallel", "arbitrary")),
        name=f"banded_attention_d{dil}_w{hw}",
    )(*args)


def _hgrn_body(*refs, layer, reverse, tb, finalize):
    q_ref, f_ref, v_ref, lbp_ref = refs[:4]
    rest = list(refs[4:])
    if finalize:
        of_ref, gate_ref, gain_ref = rest[:3]
        rest = rest[3:]
    o_ref, st_ref, qs_ref, ks_ref, bs_ref, sc_ref = rest
    c = C_CHUNK
    n = tb // c

    @pl.when(pl.program_id(2) == 0)
    def _():
        st_ref[...] = jnp.zeros_like(st_ref)

    lbp = lbp_ref[...]
    e_l = jnp.exp(lbp - jnp.max(lbp, axis=0, keepdims=True))
    sm = e_l / jnp.sum(e_l, axis=0, keepdims=True)
    lb = jnp.zeros((1, LANE), F32)
    for i in range(1, layer + 1):
        lb = lb + sm[i]
    one_m_lb = 1.0 - lb
    tiny = jnp.float32(1e-37)
    log_lb = jnp.where(lb > 0.0, jnp.log(jnp.maximum(lb, tiny)), -jnp.inf)
    log_1m_lb = jnp.where(one_m_lb > 0.0, jnp.log(jnp.maximum(one_m_lb, tiny)), -jnp.inf)

    row = lax.broadcasted_iota(jnp.int32, (c, c), 0)
    col = lax.broadcasted_iota(jnp.int32, (c, c), 1)
    causal = (row <= col) if reverse else (row >= col)
    tri = jnp.where(causal, 1.0, 0.0).astype(BF16)

    qz = q_ref[...].astype(F32)
    fz = f_ref[...]
    v3 = v_ref[...].reshape(n, c, LANE)
    e = jnp.exp(-jnp.abs(fz))
    log_sig = jnp.minimum(fz, 0.0) - jnp.log(1.0 + e)
    sig_neg = jnp.where(fz >= 0.0, e, 1.0) / (1.0 + e)
    cand = log_1m_lb + log_sig
    log_f = jnp.maximum(log_lb, cand) + jnp.log(1.0 + jnp.exp(-jnp.abs(log_lb - cand)))
    kk = one_m_lb * sig_neg
    qh = qz / (1.0 + jnp.exp(-qz))

    def widen(a):
        return jnp.concatenate([a[i * c:(i + 1) * c] for i in range(n)], axis=1)

    hi = log_f.astype(BF16)
    r1 = log_f - hi.astype(F32)
    mid = r1.astype(BF16)
    lo = (r1 - mid.astype(F32)).astype(BF16)
    b_wide = (jnp.dot(tri, widen(hi), preferred_element_type=F32)
              + jnp.dot(tri, widen(mid), preferred_element_type=F32)
              + jnp.dot(tri, widen(lo), preferred_element_type=F32))
    b3 = jnp.stack([b_wide[:, i * LANE:(i + 1) * LANE] for i in range(n)])
    b_end = b3[:, 0:1] if reverse else b3[:, c - 1:c]
    d3 = b3 - b3[:, c // 2 - 1:c // 2]
    span = jnp.max(jnp.abs(d3))
    qh3 = qh.reshape(n, c, LANE)
    kk3 = kk.reshape(n, c, LANE)

    @pl.when(span <= HGRN_FACTOR_RANGE)
    def _():
        qt = (qh3 * jnp.exp(d3)).astype(BF16)
        kt = (kk3 * jnp.exp(-d3)).astype(BF16)
        sc_ref[...] = jnp.einsum('ntk,nsk->nts', qt, kt, preferred_element_type=F32)

    @pl.when(span > HGRN_FACTOR_RANGE)
    def _():
        qs_ref[...] = qh3
        ks_ref[...] = kk3
        bs_ref[...] = b3

        def per_chunk(i, carry):
            def pair(s, sc):
                rel = jnp.minimum(bs_ref[i] - bs_ref[i, pl.ds(s, 1), :], 0.0)
                w = qs_ref[i] * ks_ref[i, pl.ds(s, 1), :] * jnp.exp(rel)
                return jnp.where(col == s, jnp.sum(w, axis=-1, keepdims=True), sc)

            sc_ref[i] = lax.fori_loop(0, c, pair, jnp.zeros((c, c), F32))
            return carry

        lax.fori_loop(0, n, per_chunk, 0)

    scores = jnp.where(causal, sc_ref[...], 0.0).astype(BF16)
    k_in = (kk3 * jnp.exp(b_end - b3)).astype(BF16)
    kv = jnp.einsum('nsv,nsk->nvk', v3, k_in, preferred_element_type=F32)
    decay_end = jnp.exp(b_end)
    st = st_ref[...]
    incoming = [None] * n
    for i in (range(n - 1, -1, -1) if reverse else range(n)):
        incoming[i] = st.astype(BF16)
        st = st * decay_end[i] + kv[i]
    st_ref[...] = st
    o = (jnp.einsum('nts,nsv->ntv', scores, v3, preferred_element_type=F32)
         + jnp.einsum('ntk,nvk->ntv', (qh3 * jnp.exp(b3)).astype(BF16), jnp.stack(incoming),
                      preferred_element_type=F32))
    o = o.reshape(tb, LANE)
    if finalize:
        o = _rms(o + of_ref[...], gain_ref[...]) * _silu(gate_ref[...].astype(F32))
    o_ref[...] = o.astype(o_ref.dtype)


def _hgrn_scan(pa, pf, pc, lower_bounds, *, seq, layer, reverse, tb, o_fwd=None, gain=None):
    bsz = pa.shape[1] // seq
    tb = min(tb, seq)
    n_t = seq // tb
    n_chunks = tb // C_CHUNK
    finalize = o_fwd is not None

    def blk(cb):
        return pl.BlockSpec((None, None, tb, LANE),
                            lambda b, hd, t: (cb + hd, b, (n_t - 1 - t) if reverse else t, 0))

    def view(p):
        return p.reshape(p.shape[0], bsz, seq, LANE)

    f_cb = (CB_CFB if reverse else CB_CFF) - SEG_F[0]
    in_specs = [blk(CB_CQ - SEG_A[0]), blk(f_cb), blk(CB_CI - SEG_C[0]),
                pl.BlockSpec((DEPTH, None, 1, LANE), lambda b, hd, t: (0, hd, 0, 0))]
    args = [view(pa), view(pf), view(pc), lower_bounds.reshape(DEPTH, C_HEADS, 1, LANE)]
    if finalize:
        in_specs += [blk(0), blk(CB_CGATE - SEG_C[0]),
                     pl.BlockSpec((None, 1, LANE), lambda b, hd, t: (hd, 0, 0))]
        args += [o_fwd, view(pc), gain.reshape(C_HEADS, 1, LANE)]
    out_dtype = BF16 if finalize else F32
    return pl.pallas_call(
        functools.partial(_hgrn_body, layer=layer, reverse=reverse, tb=tb, finalize=finalize),
        grid=(bsz, C_HEADS, n_t),
        in_specs=in_specs,
        out_specs=blk(0),
        out_shape=jax.ShapeDtypeStruct((C_HEADS, bsz, seq, LANE), out_dtype),
        scratch_shapes=[pltpu.VMEM((LANE, LANE), F32),
                        pltpu.VMEM((n_chunks, C_CHUNK, LANE), F32),
                        pltpu.VMEM((n_chunks, C_CHUNK, LANE), F32),
                        pltpu.VMEM((n_chunks, C_CHUNK, LANE), F32),
                        pltpu.VMEM((n_chunks, C_CHUNK, C_CHUNK), F32)],
        compiler_params=_cparams(("parallel", "parallel", "arbitrary")),
        name="hgrn_scan_bwd" if reverse else "hgrn_scan_fwd",
    )(*args)


def _silu(g):
    return g / (1.0 + jnp.exp(-g))


def _sigmoid(g):
    return 1.0 / (1.0 + jnp.exp(-g))


def _lane_concat(ref, n):
    return jnp.concatenate([ref[i] for i in range(n)], axis=-1)


def _merge_out_body(*refs, emit_next):
    (oa0_ref, oa1_ref, oa2_ref, l0_ref, l1_ref, l2_ref, ag_ref, ob_ref, bg0_ref, bg1_ref,
     yc_ref) = refs[:11]
    m_refs = refs[11:23]
    x_ref, wa_ref, wb_ref, wc_ref, wo_ref, pg_ref = refs[23:29]
    if emit_next:
        ng_ref, out_ref, xn_ref = refs[29:]
    else:
        (out_ref,) = refs[29:]

    ya = []
    for s in range(A_SLOTS):
        l0, l1, l2 = l0_ref[s], l1_ref[s], l2_ref[s]
        mx = jnp.maximum(jnp.maximum(l0, l1), l2)
        e0, e1, e2 = jnp.exp(l0 - mx), jnp.exp(l1 - mx), jnp.exp(l2 - mx)
        mixed = (e0 * oa0_ref[s] + e1 * oa1_ref[s] + e2 * oa2_ref[s]) / (e0 + e1 + e2)
        ya.append((mixed * _silu(ag_ref[s].astype(F32))).astype(BF16))
    ya = jnp.concatenate(ya, axis=-1)
    yb = []
    for hd in range(B_Q_HEADS):
        g_ref = bg0_ref if hd < 4 else bg1_ref
        yb.append((ob_ref[hd] * _silu(g_ref[hd % 4].astype(F32))).astype(BF16))
    yb = jnp.concatenate(yb, axis=-1)
    yc = jnp.concatenate([yc_ref[hd] for hd in range(C_HEADS)], axis=-1)

    proj = (jnp.dot(ya, wa_ref[...], preferred_element_type=F32),
            jnp.dot(yb, wb_ref[...], preferred_element_type=F32),
            jnp.dot(yc, wc_ref[...], preferred_element_type=F32))
    per_branch = D_MODEL // (TILE_CB * LANE)
    y = []
    for j in range(per_branch):
        cols = slice(j * TILE_CB * LANE, (j + 1) * TILE_CB * LANE)
        acc = None
        for br in range(N_BRANCHES):
            gate = _sigmoid(_lane_concat(m_refs[br * per_branch + j], TILE_CB).astype(F32))
            term = gate * proj[br][:, cols]
            acc = term if acc is None else acc + term
        y.append(acc.astype(BF16))
    y = jnp.concatenate(y, axis=-1)
    out = jnp.dot(y, wo_ref[...], preferred_element_type=F32)
    x_new = x_ref[...] + _rms(out, pg_ref[...])
    out_ref[...] = x_new
    if emit_next:
        xn_ref[...] = _rms(x_new, ng_ref[...]).astype(xn_ref.dtype)


def _merge_out(x2, pa, pc, oa, lse, ob, yc, wa, wb, wc, wo, post_gain, next_gain, *, tm):
    rows, d = x2.shape
    emit_next = next_gain is not None

    def heads(n, first=0):
        return pl.BlockSpec((n, tm, LANE), lambda i: (first // n, i, 0))

    def const(shape):
        return pl.BlockSpec(shape, lambda i: (0,) * len(shape), pipeline_mode=pl.Buffered(1))

    n_merge = N_BRANCHES * D_MODEL // (TILE_CB * LANE)
    in_specs = ([heads(A_SLOTS)] * 6
                + [heads(A_SLOTS, CB_AGATE), heads(B_Q_HEADS),
                   heads(4, CB_BGATE), heads(4, CB_BGATE + 4), heads(C_HEADS)]
                + [heads(TILE_CB, CB_MERGE - SEG_C[0] + TILE_CB * j) for j in range(n_merge)]
                + [pl.BlockSpec((tm, d), lambda i: (i, 0)),
                   const(wa.shape), const(wb.shape), const(wc.shape), const(wo.shape),
                   const((1, d))])
    args = (list(oa) + list(lse) + [pa, ob, pa, pa, yc] + [pc] * n_merge
            + [x2, wa, wb, wc, wo, post_gain.reshape(1, d)])
    row_spec = pl.BlockSpec((tm, d), lambda i: (i, 0))
    out_specs, out_shape = row_spec, jax.ShapeDtypeStruct((rows, d), F32)
    if emit_next:
        in_specs.append(const((1, d)))
        args.append(next_gain.reshape(1, d))
        out_specs = [row_spec, row_spec]
        out_shape = [out_shape, jax.ShapeDtypeStruct((rows, d), BF16)]
    return pl.pallas_call(
        functools.partial(_merge_out_body, emit_next=emit_next),
        grid=(rows // tm,),
        in_specs=in_specs,
        out_specs=out_specs,
        out_shape=out_shape,
        compiler_params=_cparams(("parallel",)),
        name="merge_out",
    )(*args)


def kernel(x, pre_norm, post_norm, w_in, sink_logits, hgrn_lower_bounds, hgrn_norm,
           w_branch_a, w_branch_b, w_branch_c, w_out):
    bsz, seq, d = x.shape
    rows = bsz * seq
    assert d == D_MODEL and w_in.shape[-1] == IN_COLS
    tile = min(PROJ_TILE_ROWS, seq)
    assert seq % tile == 0
    rope = _rope_tables(seq)
    plans = _projection_plans()
    x2 = x.reshape(rows, d)
    xn = _pre_norm(x2, pre_norm[0].astype(F32), tm=min(512, rows))
    for layer in range(DEPTH):
        w = w_in[layer].astype(BF16)
        pa = _in_projection(xn, w, rope, seq=seq, seg=SEG_A, out_dtype=BF16, plans=plans, tm=tile)
        pf = _in_projection(xn, w, None, seq=seq, seg=SEG_F, out_dtype=F32, plans=None, tm=tile)
        pc = _in_projection(xn, w, None, seq=seq, seg=SEG_C, out_dtype=BF16, plans=None, tm=tile)

        oa, lse = [], []
        for g, (window, dil) in enumerate(DIL_GROUPS):
            o_g, lse_g = _banded_attention(
                pa, None, seq=seq, tile=tile, dil=dil, hw=window // (2 * dil), qb=128,
                q_cb=CB_AQ + g * A_SLOTS, k_cb=CB_AK + g * A_SLOTS, v_cb=CB_AV + g * A_SLOTS,
                n_kv=A_SLOTS, n_q=1, want_lse=True)
            oa.append(o_g.reshape(A_SLOTS, rows, LANE))
            lse.append(lse_g.reshape(A_SLOTS, rows, LANE))

        sink = jnp.broadcast_to(sink_logits[layer].astype(F32)[:, None, None], (B_Q_HEADS, 1, LANE))
        ob = _banded_attention(
            pa, sink, seq=seq, tile=tile, dil=1, hw=B_HALF_WINDOW, qb=128,
            q_cb=CB_BQ, k_cb=CB_BK, v_cb=CB_BV,
            n_kv=B_KV_HEADS, n_q=B_Q_HEADS // B_KV_HEADS, want_lse=False)
        ob = ob.reshape(B_Q_HEADS, rows, LANE)

        lower = hgrn_lower_bounds.astype(F32)
        o_fwd = _hgrn_scan(pa, pf, pc, lower, seq=seq, layer=layer, reverse=False, tb=512)
        yc = _hgrn_scan(pa, pf, pc, lower, seq=seq, layer=layer, reverse=True, tb=512,
                        o_fwd=o_fwd, gain=hgrn_norm[layer].astype(F32))
        yc = yc.reshape(C_HEADS, rows, LANE)

        next_gain = pre_norm[layer + 1].astype(F32) if layer + 1 < DEPTH else None
        res = _merge_out(x2, pa, pc, oa, lse, ob, yc,
                         w_branch_a[layer].astype(BF16), w_branch_b[layer].astype(BF16),
                         w_branch_c[layer].astype(BF16), w_out[layer].astype(BF16),
                         post_norm[layer].astype(F32), next_gain, tm=min(256, rows))
        x2, xn = res if next_gain is not None else (res, None)
    return x2.reshape(bsz, seq, d)
```

```python
import functools

import jax
import jax.numpy as jnp
from jax import lax
from jax.experimental import pallas as pl
from jax.experimental.pallas import tpu as pltpu

F32 = jnp.float32
BF16 = jnp.bfloat16

LANE = 128
D_MODEL = 2048
DEPTH = 2
HEAD_DIM = 128
ROT_DIM = HEAD_DIM // 4
ROT_HALF = ROT_DIM // 2
ROPE_THETA = 500000.0
NORM_EPS = 1e-6

DIL_GROUPS = ((128, 1), (512, 4), (2048, 16))
A_SLOTS = 4
A_HEADS = A_SLOTS * len(DIL_GROUPS)
B_Q_HEADS = 8
B_KV_HEADS = 2
B_HALF_WINDOW = 128
C_HEADS = 8
C_CHUNK = 64
N_BRANCHES = 3

CB_AQ = 0
CB_AK = CB_AQ + A_HEADS
CB_AV = CB_AK + A_HEADS
CB_AGATE = CB_AV + A_HEADS
CB_BQ = CB_AGATE + A_SLOTS
CB_BK = CB_BQ + B_Q_HEADS
CB_BV = CB_BK + B_KV_HEADS
CB_BGATE = CB_BV + B_KV_HEADS
CB_CQ = CB_BGATE + B_Q_HEADS
CB_CFF = CB_CQ + C_HEADS
CB_CFB = CB_CFF + C_HEADS
CB_CI = CB_CFB + C_HEADS
CB_CGATE = CB_CI + C_HEADS
CB_MERGE = CB_CGATE + C_HEADS
N_CB = CB_MERGE + N_BRANCHES * D_MODEL // LANE
IN_COLS = N_CB * LANE
TILE_CB = 4

SEG_A, SEG_F, SEG_C = (0, CB_CFF), (CB_CFF, CB_CI), (CB_CI, N_CB)

PROJ_TILE_ROWS = 2048
NO_ROPE, ROPE_K, ROPE_Q = 0, 1, 2

MASK_VALUE = -1e30
HGRN_FACTOR_RANGE = 80.0

VMEM_LIMIT = 56 * 1024 * 1024


def _cparams(sem):
    return pltpu.CompilerParams(dimension_semantics=sem, vmem_limit_bytes=VMEM_LIMIT)


def _projection_plans():
    codes = [NO_ROPE] * SEG_A[1]
    dils = [1] * SEG_A[1]
    for g, (_, dil) in enumerate(DIL_GROUPS):
        for s in range(A_SLOTS):
            codes[CB_AQ + g * A_SLOTS + s] = ROPE_Q
            codes[CB_AK + g * A_SLOTS + s] = ROPE_K
            for base in (CB_AQ, CB_AK, CB_AV):
                dils[base + g * A_SLOTS + s] = dil
    for h in range(B_Q_HEADS):
        codes[CB_BQ + h] = ROPE_Q
    for h in range(B_KV_HEADS):
        codes[CB_BK + h] = ROPE_K
    plans = []
    for t in range(SEG_A[1] // TILE_CB):
        sl = slice(t * TILE_CB, (t + 1) * TILE_CB)
        assert len(set(dils[sl])) == 1
        plans.append((tuple(codes[sl]), dils[sl][0]))
    return tuple(plans)


def _rms(x, gain):
    return x * lax.rsqrt(jnp.mean(x * x, axis=-1, keepdims=True) + NORM_EPS) * gain


def _pre_norm_body(x_ref, g_ref, o_ref):
    o_ref[...] = _rms(x_ref[...], g_ref[...]).astype(o_ref.dtype)


def _pre_norm(x2, gain, *, tm):
    rows, d = x2.shape
    return pl.pallas_call(
        _pre_norm_body,
        grid=(rows // tm,),
        in_specs=[pl.BlockSpec((tm, d), lambda i: (i, 0)), pl.BlockSpec((1, d), lambda i: (0, 0))],
        out_specs=pl.BlockSpec((tm, d), lambda i: (i, 0)),
        out_shape=jax.ShapeDtypeStruct((rows, d), BF16),
        compiler_params=_cparams(("parallel",)),
        name="pre_norm",
    )(x2, gain.reshape(1, d))


def _rope_tables(seq):
    pos = jnp.arange(seq, dtype=F32)
    inv_freq = ROPE_THETA ** (-jnp.arange(0, ROT_DIM, 2, dtype=F32) / ROT_DIM)
    ang = pos[:, None] * inv_freq[None, :]
    cos, sin = jnp.cos(ang), jnp.sin(ang)
    rest = HEAD_DIM - ROT_DIM
    c_full = jnp.concatenate([cos, cos, jnp.ones((seq, rest), F32)], axis=-1)
    s_full = jnp.concatenate([-sin, sin, jnp.zeros((seq, rest), F32)], axis=-1)
    return jnp.stack([c_full, s_full])


def _rotate_half_matrix():
    src = lax.broadcasted_iota(jnp.int32, (LANE, LANE), 0)
    dst = lax.broadcasted_iota(jnp.int32, (LANE, LANE), 1)
    hit = ((dst < ROT_HALF) & (src == dst + ROT_HALF)) | (
        (dst >= ROT_HALF) & (dst < ROT_DIM) & (src == dst - ROT_HALF))
    return jnp.where(hit, 1.0, 0.0).astype(BF16)


def _apply_rope(t, cos_full, sin_signed, swap):
    swapped = jnp.dot(t.astype(BF16), swap, preferred_element_type=F32)
    return t * cos_full + swapped * sin_signed


def _in_projection_body(*refs, plans, tm):
    if plans is None:
        x_ref, w_ref, o_ref = refs
    else:
        x_ref, w_ref, rope_ref, o_ref, stage_ref = refs
    acc = jnp.dot(x_ref[...], w_ref[...], preferred_element_type=F32)

    def block(c):
        return acc[:, c * LANE:(c + 1) * LANE]

    if plans is None:
        for c in range(TILE_CB):
            o_ref[c] = block(c).astype(o_ref.dtype)
        return

    j = pl.program_id(1)
    groups = {}
    for tile, plan in enumerate(plans):
        groups.setdefault(plan, []).append(tile)
    scale = HEAD_DIM ** -0.5
    swap = _rotate_half_matrix()
    for (codes, dil), tiles in groups.items():
        cond = functools.reduce(jnp.logical_or, [j == t for t in tiles])

        @pl.when(cond)
        def _(codes=codes, dil=dil):
            for c in range(TILE_CB):
                t = block(c)
                if codes[c] != NO_ROPE:
                    t = _apply_rope(t, rope_ref[0], rope_ref[1], swap)
                if codes[c] == ROPE_Q:
                    t = t * scale
                if dil == 1:
                    o_ref[c] = t.astype(o_ref.dtype)
                else:
                    stage_ref[c] = t
            if dil > 1:
                cs = tm // dil
                for c in range(TILE_CB):
                    def phase(p, carry, c=c):
                        dst = pl.ds(pl.multiple_of(p * cs, cs), cs)
                        src = stage_ref[c, pl.ds(p, cs, stride=dil), :]
                        o_ref[c, dst, :] = src.astype(o_ref.dtype)
                        return carry

                    lax.fori_loop(0, dil, phase, 0)


def _in_projection(xn, w_bf16, rope, *, seq, seg, out_dtype, plans, tm):
    rows, d = xn.shape
    cb0, cb1 = seg
    n_tiles = (cb1 - cb0) // TILE_CB
    tn = TILE_CB * LANE
    tile0 = cb0 // TILE_CB
    in_specs = [pl.BlockSpec((tm, d), lambda i, j: (i, 0)),
                pl.BlockSpec((d, tn), lambda i, j: (0, tile0 + j))]
    args = [xn, w_bf16]
    scratch = []
    if plans is not None:
        per_seq = seq // tm
        in_specs.append(pl.BlockSpec((2, tm, LANE), lambda i, j: (0, i % per_seq, 0)))
        args.append(rope)
        scratch.append(pltpu.VMEM((TILE_CB, tm, LANE), F32))
    return pl.pallas_call(
        functools.partial(_in_projection_body, plans=plans, tm=tm),
        grid=(rows // tm, n_tiles),
        in_specs=in_specs,
        out_specs=pl.BlockSpec((TILE_CB, tm, LANE), lambda i, j: (j, i, 0)),
        out_shape=jax.ShapeDtypeStruct((cb1 - cb0, rows, LANE), out_dtype),
        scratch_shapes=scratch,
        compiler_params=_cparams(("parallel", "arbitrary")),
        name=f"in_projection_{cb0}",
    )(*args)


def _attention_body(*refs, dil, hw, cs, qb, n_sub, n_q, has_sink, want_lse, unroll):
    q_ref, kp_ref, kc_ref, kn_ref, vp_ref, vc_ref, vn_ref = refs[:7]
    rest = list(refs[7:])
    sink_ref = rest.pop(0) if has_sink else None
    o_ref = rest.pop(0)
    lse_ref = rest.pop(0) if want_lse else None
    kbuf, vbuf, band_ref, kbias_ref = rest

    t = pl.program_id(2)
    nk = qb + 2 * hw
    nks = cs + 2 * hw
    n_u = cs // qb
    kbuf[:, 0:hw] = kp_ref[...]
    kbuf[:, hw:hw + cs] = kc_ref[...]
    kbuf[:, hw + cs:] = kn_ref[...]
    vbuf[:, 0:hw, 0:LANE] = vp_ref[...]
    vbuf[:, hw:hw + cs, 0:LANE] = vc_ref[...]
    vbuf[:, hw + cs:, 0:LANE] = vn_ref[...]
    vbuf[:, :, LANE:] = jnp.ones((dil, nks, LANE), BF16)

    qi = lax.broadcasted_iota(jnp.int32, (qb, nk), 0)
    kj = lax.broadcasted_iota(jnp.int32, (qb, nk), 1)
    band_ref[...] = jnp.where(jnp.abs(kj - hw - qi) <= hw, 0.0, MASK_VALUE)
    kpos = t * cs - hw + lax.broadcasted_iota(jnp.int32, (1, nks), 1)
    kbias_ref[...] = jnp.where((kpos >= 0) & (kpos < n_sub), 0.0, MASK_VALUE)

    n_blocks = dil * n_u
    group_size = min(unroll, n_blocks)
    assert n_blocks % group_size == 0
    if has_sink:
        sink = sink_ref[...][:, :, :1]

    def group(i, carry):
        units = []
        for j in range(group_size):
            idx = i * group_size + j
            u = idx % n_u
            ph = idx // n_u
            units.append((u, ph, pl.multiple_of(u * qb, qb)))
        scores = []
        for u, ph, u0 in units:
            q = q_ref[:, ph, pl.ds(u0, qb), :].reshape(n_q * qb, LANE)
            s = lax.dot_general(q, kbuf[ph, pl.ds(u0, nk), :],
                                (((1,), (1,)), ((), ())), preferred_element_type=F32)
            scores.append(s.reshape(n_q, qb, nk) + band_ref[...] + kbias_ref[:, pl.ds(u0, nk)])
        probs, maxes = [], []
        for s in scores:
            m = jnp.max(s, axis=-1, keepdims=True)
            if has_sink:
                m = jnp.maximum(m, sink)
            maxes.append(m)
            probs.append(jnp.exp(s - m).astype(BF16).reshape(n_q * qb, nk))
        for (u, ph, u0), p, m in zip(units, probs, maxes):
            o2 = jnp.dot(p, vbuf[ph, pl.ds(u0, nk), :], preferred_element_type=F32)
            o2 = o2.reshape(n_q, qb, 2 * LANE)
            denom = o2[:, :, LANE:]
            if has_sink:
                denom = denom + jnp.exp(sink - m)
            rows = pl.ds(u0, qb) if dil == 1 else pl.ds(ph + u * (qb * dil), qb, stride=dil)
            o_ref[:, rows, :] = o2[:, :, :LANE] / denom
            if want_lse:
                lse_ref[:, rows, :] = m + jnp.log(denom)
        return carry

    lax.fori_loop(0, n_blocks // group_size, group, 0)


def _banded_attention(pa, sink, *, seq, tile, dil, hw, qb, q_cb, k_cb, v_cb, n_kv, n_q, want_lse,
                      unroll=8):
    n_cb, rows, _ = pa.shape
    bsz = rows // seq
    n_t = seq // tile
    cs = tile // dil
    per = cs // hw
    assert q_cb % n_q == 0 and cs % qb == 0 and cs % hw == 0
    view = pa.reshape(n_cb, bsz, n_t, dil, cs, LANE)
    has_sink = sink is not None

    def cur(cb, n=None):
        return pl.BlockSpec((n, None, None, dil, cs, LANE),
                            lambda b, hd, t: ((cb if n is None else cb // n) + hd, b, t, 0, 0, 0))

    def prev(cb):
        return pl.BlockSpec((None, None, None, dil, hw, LANE),
                            lambda b, hd, t: (cb + hd, b, jnp.maximum(t - 1, 0), 0, per - 1, 0))

    def nxt(cb):
        return pl.BlockSpec((None, None, None, dil, hw, LANE),
                            lambda b, hd, t: (cb + hd, b, jnp.minimum(t + 1, n_t - 1), 0, 0, 0))

    in_specs = [cur(q_cb, n_q), prev(k_cb), cur(k_cb), nxt(k_cb), prev(v_cb), cur(v_cb), nxt(v_cb)]
    args = [view] * 7
    if has_sink:
        in_specs.append(pl.BlockSpec((n_q, 1, LANE), lambda b, hd, t: (hd, 0, 0)))
        args.append(sink)
    o_spec = pl.BlockSpec((n_q, None, tile, LANE), lambda b, hd, t: (hd, b, t, 0))
    o_shape = jax.ShapeDtypeStruct((n_kv * n_q, bsz, seq, LANE), F32)
    return pl.pallas_call(
        functools.partial(_attention_body, dil=dil, hw=hw, cs=cs, qb=qb, n_sub=seq // dil, n_q=n_q,
                          has_sink=has_sink, want_lse=want_lse, unroll=unroll),
        grid=(bsz, n_kv, n_t),
        in_specs=in_specs,
        out_specs=[o_spec, o_spec] if want_lse else o_spec,
        out_shape=[o_shape, o_shape] if want_lse else o_shape,
        scratch_shapes=[pltpu.VMEM((dil, cs + 2 * hw, LANE), BF16),
                        pltpu.VMEM((dil, cs + 2 * hw, 2 * LANE), BF16),
                        pltpu.VMEM((qb, qb + 2 * hw), F32),
                        pltpu.VMEM((1, cs + 2 * hw), F32)],
        compiler_params=_cparams(("parallel", "parallel", "arbitrary")),
        name=f"banded_attention_d{dil}_w{hw}",
    )(*args)


def _hgrn_body(*refs, layer, reverse, tb, finalize):
    q_ref, f_ref, v_ref, lbp_ref = refs[:4]
    rest = list(refs[4:])
    if finalize:
        of_ref, gate_ref, gain_ref = rest[:3]
        rest = rest[3:]
    o_ref, st_ref, qs_ref, ks_ref, bs_ref, sc_ref = rest
    c = C_CHUNK
    n = tb // c

    @pl.when(pl.program_id(2) == 0)
    def _():
        st_ref[...] = jnp.zeros_like(st_ref)

    lbp = lbp_ref[...]
    e_l = jnp.exp(lbp - jnp.max(lbp, axis=0, keepdims=True))
    sm = e_l / jnp.sum(e_l, axis=0, keepdims=True)
    lb = jnp.zeros((1, LANE), F32)
    for i in range(1, layer + 1):
        lb = lb + sm[i]
    one_m_lb = 1.0 - lb
    tiny = jnp.float32(1e-37)
    log_lb = jnp.where(lb > 0.0, jnp.log(jnp.maximum(lb, tiny)), -jnp.inf)
    log_1m_lb = jnp.where(one_m_lb > 0.0, jnp.log(jnp.maximum(one_m_lb, tiny)), -jnp.inf)

    row = lax.broadcasted_iota(jnp.int32, (c, c), 0)
    col = lax.broadcasted_iota(jnp.int32, (c, c), 1)
    causal = (row <= col) if reverse else (row >= col)
    tri = jnp.where(causal, 1.0, 0.0).astype(BF16)

    qz = q_ref[...].astype(F32)
    fz = f_ref[...]
    v3 = v_ref[...].reshape(n, c, LANE)
    e = jnp.exp(-jnp.abs(fz))
    log_sig = jnp.minimum(fz, 0.0) - jnp.log(1.0 + e)
    sig_neg = jnp.where(fz >= 0.0, e, 1.0) / (1.0 + e)
    cand = log_1m_lb + log_sig
    log_f = jnp.maximum(log_lb, cand) + jnp.log(1.0 + jnp.exp(-jnp.abs(log_lb - cand)))
    kk = one_m_lb * sig_neg
    qh = qz / (1.0 + jnp.exp(-qz))

    def widen(a):
        return jnp.concatenate([a[i * c:(i + 1) * c] for i in range(n)], axis=1)

    hi = log_f.astype(BF16)
    r1 = log_f - hi.astype(F32)
    mid = r1.astype(BF16)
    lo = (r1 - mid.astype(F32)).astype(BF16)
    b_wide = (jnp.dot(tri, widen(hi), preferred_element_type=F32)
              + jnp.dot(tri, widen(mid), preferred_element_type=F32)
              + jnp.dot(tri, widen(lo), preferred_element_type=F32))
    b3 = jnp.stack([b_wide[:, i * LANE:(i + 1) * LANE] for i in range(n)])
    b_end = b3[:, 0:1] if reverse else b3[:, c - 1:c]
    d3 = b3 - b3[:, c // 2 - 1:c // 2]
    span = jnp.max(jnp.abs(d3))
    qh3 = qh.reshape(n, c, LANE)
    kk3 = kk.reshape(n, c, LANE)

    @pl.when(span <= HGRN_FACTOR_RANGE)
    def _():
        qt = (qh3 * jnp.exp(d3)).astype(BF16)
        kt = (kk3 * jnp.exp(-d3)).astype(BF16)
        sc_ref[...] = jnp.einsum('ntk,nsk->nts', qt, kt, preferred_element_type=F32)

    @pl.when(span > HGRN_FACTOR_RANGE)
    def _():
        qs_ref[...] = qh3
        ks_ref[...] = kk3
        bs_ref[...] = b3

        def per_chunk(i, carry):
            def pair(s, sc):
                rel = jnp.minimum(bs_ref[i] - bs_ref[i, pl.ds(s, 1), :], 0.0)
                w = qs_ref[i] * ks_ref[i, pl.ds(s, 1), :] * jnp.exp(rel)
                return jnp.where(col == s, jnp.sum(w, axis=-1, keepdims=True), sc)

            sc_ref[i] = lax.fori_loop(0, c, pair, jnp.zeros((c, c), F32))
            return carry

        lax.fori_loop(0, n, per_chunk, 0)

    scores = jnp.where(causal, sc_ref[...], 0.0).astype(BF16)
    k_in = (kk3 * jnp.exp(b_end - b3)).astype(BF16)
    kv = jnp.einsum('nsv,nsk->nvk', v3, k_in, preferred_element_type=F32)
    decay_end = jnp.exp(b_end)
    st = st_ref[...]
    incoming = [None] * n
    for i in (range(n - 1, -1, -1) if reverse else range(n)):
        incoming[i] = st.astype(BF16)
        st = st * decay_end[i] + kv[i]
    st_ref[...] = st
    o = (jnp.einsum('nts,nsv->ntv', scores, v3, preferred_element_type=F32)
         + jnp.einsum('ntk,nvk->ntv', (qh3 * jnp.exp(b3)).astype(BF16), jnp.stack(incoming),
                      preferred_element_type=F32))
    o = o.reshape(tb, LANE)
    if finalize:
        o = _rms(o + of_ref[...], gain_ref[...]) * _silu(gate_ref[...].astype(F32))
    o_ref[...] = o.astype(o_ref.dtype)


def _hgrn_scan(pa, pf, pc, lower_bounds, *, seq, layer, reverse, tb, o_fwd=None, gain=None):
    bsz = pa.shape[1] // seq
    tb = min(tb, seq)
    n_t = seq // tb
    n_chunks = tb // C_CHUNK
    finalize = o_fwd is not None

    def blk(cb):
        return pl.BlockSpec((None, None, tb, LANE),
                            lambda b, hd, t: (cb + hd, b, (n_t - 1 - t) if reverse else t, 0))

    def view(p):
        return p.reshape(p.shape[0], bsz, seq, LANE)

    f_cb = (CB_CFB if reverse else CB_CFF) - SEG_F[0]
    in_specs = [blk(CB_CQ - SEG_A[0]), blk(f_cb), blk(CB_CI - SEG_C[0]),
                pl.BlockSpec((DEPTH, None, 1, LANE), lambda b, hd, t: (0, hd, 0, 0))]
    args = [view(pa), view(pf), view(pc), lower_bounds.reshape(DEPTH, C_HEADS, 1, LANE)]
    if finalize:
        in_specs += [blk(0), blk(CB_CGATE - SEG_C[0]),
                     pl.BlockSpec((None, 1, LANE), lambda b, hd, t: (hd, 0, 0))]
        args += [o_fwd, view(pc), gain.reshape(C_HEADS, 1, LANE)]
    out_dtype = BF16 if finalize else F32
    return pl.pallas_call(
        functools.partial(_hgrn_body, layer=layer, reverse=reverse, tb=tb, finalize=finalize),
        grid=(bsz, C_HEADS, n_t),
        in_specs=in_specs,
        out_specs=blk(0),
        out_shape=jax.ShapeDtypeStruct((C_HEADS, bsz, seq, LANE), out_dtype),
        scratch_shapes=[pltpu.VMEM((LANE, LANE), F32),
                        pltpu.VMEM((n_chunks, C_CHUNK, LANE), F32),
                        pltpu.VMEM((n_chunks, C_CHUNK, LANE), F32),
                        pltpu.VMEM((n_chunks, C_CHUNK, LANE), F32),
                        pltpu.VMEM((n_chunks, C_CHUNK, C_CHUNK), F32)],
        compiler_params=_cparams(("parallel", "parallel", "arbitrary")),
        name="hgrn_scan_bwd" if reverse else "hgrn_scan_fwd",
    )(*args)


def _silu(g):
    return g / (1.0 + jnp.exp(-g))


def _sigmoid(g):
    return 1.0 / (1.0 + jnp.exp(-g))


def _lane_concat(ref, n):
    return jnp.concatenate([ref[i] for i in range(n)], axis=-1)


def _merge_out_body(*refs, emit_next):
    (oa0_ref, oa1_ref, oa2_ref, l0_ref, l1_ref, l2_ref, ag_ref, ob_ref, bg0_ref, bg1_ref,
     yc_ref) = refs[:11]
    m_refs = refs[11:23]
    x_ref, wa_ref, wb_ref, wc_ref, wo_ref, pg_ref = refs[23:29]
    if emit_next:
        ng_ref, out_ref, xn_ref = refs[29:]
    else:
        (out_ref,) = refs[29:]

    ya = []
    for s in range(A_SLOTS):
        l0, l1, l2 = l0_ref[s], l1_ref[s], l2_ref[s]
        mx = jnp.maximum(jnp.maximum(l0, l1), l2)
        e0, e1, e2 = jnp.exp(l0 - mx), jnp.exp(l1 - mx), jnp.exp(l2 - mx)
        mixed = (e0 * oa0_ref[s] + e1 * oa1_ref[s] + e2 * oa2_ref[s]) / (e0 + e1 + e2)
        ya.append((mixed * _silu(ag_ref[s].astype(F32))).astype(BF16))
    ya = jnp.concatenate(ya, axis=-1)
    yb = []
    for hd in range(B_Q_HEADS):
        g_ref = bg0_ref if hd < 4 else bg1_ref
        yb.append((ob_ref[hd] * _silu(g_ref[hd % 4].astype(F32))).astype(BF16))
    yb = jnp.concatenate(yb, axis=-1)
    yc = jnp.concatenate([yc_ref[hd] for hd in range(C_HEADS)], axis=-1)

    proj = (jnp.dot(ya, wa_ref[...], preferred_element_type=F32),
            jnp.dot(yb, wb_ref[...], preferred_element_type=F32),
            jnp.dot(yc, wc_ref[...], preferred_element_type=F32))
    per_branch = D_MODEL // (TILE_CB * LANE)
    y = []
    for j in range(per_branch):
        cols = slice(j * TILE_CB * LANE, (j + 1) * TILE_CB * LANE)
        acc = None
        for br in range(N_BRANCHES):
            gate = _sigmoid(_lane_concat(m_refs[br * per_branch + j], TILE_CB).astype(F32))
            term = gate * proj[br][:, cols]
            acc = term if acc is None else acc + term
        y.append(acc.astype(BF16))
    y = jnp.concatenate(y, axis=-1)
    out = jnp.dot(y, wo_ref[...], preferred_element_type=F32)
    x_new = x_ref[...] + _rms(out, pg_ref[...])
    out_ref[...] = x_new
    if emit_next:
        xn_ref[...] = _rms(x_new, ng_ref[...]).astype(xn_ref.dtype)


def _merge_out(x2, pa, pc, oa, lse, ob, yc, wa, wb, wc, wo, post_gain, next_gain, *, tm):
    rows, d = x2.shape
    emit_next = next_gain is not None

    def heads(n, first=0):
        return pl.BlockSpec((n, tm, LANE), lambda i: (first // n, i, 0))

    def const(shape):
        return pl.BlockSpec(shape, lambda i: (0,) * len(shape), pipeline_mode=pl.Buffered(1))

    n_merge = N_BRANCHES * D_MODEL // (TILE_CB * LANE)
    in_specs = ([heads(A_SLOTS)] * 6
                + [heads(A_SLOTS, CB_AGATE), heads(B_Q_HEADS),
                   heads(4, CB_BGATE), heads(4, CB_BGATE + 4), heads(C_HEADS)]
                + [heads(TILE_CB, CB_MERGE - SEG_C[0] + TILE_CB * j) for j in range(n_merge)]
                + [pl.BlockSpec((tm, d), lambda i: (i, 0)),
                   const(wa.shape), const(wb.shape), const(wc.shape), const(wo.shape),
                   const((1, d))])
    args = (list(oa) + list(lse) + [pa, ob, pa, pa, yc] + [pc] * n_merge
            + [x2, wa, wb, wc, wo, post_gain.reshape(1, d)])
    row_spec = pl.BlockSpec((tm, d), lambda i: (i, 0))
    out_specs, out_shape = row_spec, jax.ShapeDtypeStruct((rows, d), F32)
    if emit_next:
        in_specs.append(const((1, d)))
        args.append(next_gain.reshape(1, d))
        out_specs = [row_spec, row_spec]
        out_shape = [out_shape, jax.ShapeDtypeStruct((rows, d), BF16)]
    return pl.pallas_call(
        functools.partial(_merge_out_body, emit_next=emit_next),
        grid=(rows // tm,),
        in_specs=in_specs,
        out_specs=out_specs,
        out_shape=out_shape,
        compiler_params=_cparams(("parallel",)),
        name="merge_out",
    )(*args)


def kernel(x, pre_norm, post_norm, w_in, sink_logits, hgrn_lower_bounds, hgrn_norm,
           w_branch_a, w_branch_b, w_branch_c, w_out):
    bsz, seq, d = x.shape
    rows = bsz * seq
    assert d == D_MODEL and w_in.shape[-1] == IN_COLS
    tile = min(PROJ_TILE_ROWS, seq)
    assert seq % tile == 0
    rope = _rope_tables(seq)
    plans = _projection_plans()
    x2 = x.reshape(rows, d)
    xn = _pre_norm(x2, pre_norm[0].astype(F32), tm=min(512, rows))
    for layer in range(DEPTH):
        w = w_in[layer].astype(BF16)
        pa = _in_projection(xn, w, rope, seq=seq, seg=SEG_A, out_dtype=BF16, plans=plans, tm=tile)
        pf = _in_projection(xn, w, None, seq=seq, seg=SEG_F, out_dtype=F32, plans=None, tm=tile)
        pc = _in_projection(xn, w, None, seq=seq, seg=SEG_C, out_dtype=BF16, plans=None, tm=tile)

        oa, lse = [], []
        for g, (window, dil) in enumerate(DIL_GROUPS):
            o_g, lse_g = _banded_attention(
                pa, None, seq=seq, tile=tile, dil=dil, hw=window // (2 * dil), qb=128,
                q_cb=CB_AQ + g * A_SLOTS, k_cb=CB_AK + g * A_SLOTS, v_cb=CB_AV + g * A_SLOTS,
                n_kv=A_SLOTS, n_q=1, want_lse=True)
            oa.append(o_g.reshape(A_SLOTS, rows, LANE))
            lse.append(lse_g.reshape(A_SLOTS, rows, LANE))

        sink = jnp.broadcast_to(sink_logits[layer].astype(F32)[:, None, None], (B_Q_HEADS, 1, LANE))
        ob = _banded_attention(
            pa, sink, seq=seq, tile=tile, dil=1, hw=B_HALF_WINDOW, qb=128,
            q_cb=CB_BQ, k_cb=CB_BK, v_cb=CB_BV,
            n_kv=B_KV_HEADS, n_q=B_Q_HEADS // B_KV_HEADS, want_lse=False, unroll=2)
        ob = ob.reshape(B_Q_HEADS, rows, LANE)

        lower = hgrn_lower_bounds.astype(F32)
        o_fwd = _hgrn_scan(pa, pf, pc, lower, seq=seq, layer=layer, reverse=False, tb=512)
        yc = _hgrn_scan(pa, pf, pc, lower, seq=seq, layer=layer, reverse=True, tb=512,
                        o_fwd=o_fwd, gain=hgrn_norm[layer].astype(F32))
        yc = yc.reshape(C_HEADS, rows, LANE)

        next_gain = pre_norm[layer + 1].astype(F32) if layer + 1 < DEPTH else None
        res = _merge_out(x2, pa, pc, oa, lse, ob, yc,
                         w_branch_a[layer].astype(BF16), w_branch_b[layer].astype(BF16),
                         w_branch_c[layer].astype(BF16), w_out[layer].astype(BF16),
                         post_norm[layer].astype(F32), next_gain, tm=min(256, rows))
        x2, xn = res if next_gain is not None else (res, None)
    return x2.reshape(bsz, seq, d)
```

```python
import functools

import jax
import jax.numpy as jnp
from jax import lax
from jax.experimental import pallas as pl
from jax.experimental.pallas import tpu as pltpu

F32 = jnp.float32
BF16 = jnp.bfloat16

LANE = 128
D_MODEL = 2048
DEPTH = 2
HEAD_DIM = 128
ROT_DIM = HEAD_DIM // 4
ROT_HALF = ROT_DIM // 2
ROPE_THETA = 500000.0
NORM_EPS = 1e-6

DIL_GROUPS = ((128, 1), (512, 4), (2048, 16))
A_SLOTS = 4
A_HEADS = A_SLOTS * len(DIL_GROUPS)
B_Q_HEADS = 8
B_KV_HEADS = 2
B_HALF_WINDOW = 128
C_HEADS = 8
C_CHUNK = 64
N_BRANCHES = 3

CB_AQ = 0
CB_AK = CB_AQ + A_HEADS
CB_AV = CB_AK + A_HEADS
CB_AGATE = CB_AV + A_HEADS
CB_BQ = CB_AGATE + A_SLOTS
CB_BK = CB_BQ + B_Q_HEADS
CB_BV = CB_BK + B_KV_HEADS
CB_BGATE = CB_BV + B_KV_HEADS
CB_CQ = CB_BGATE + B_Q_HEADS
CB_CFF = CB_CQ + C_HEADS
CB_CFB = CB_CFF + C_HEADS
CB_CI = CB_CFB + C_HEADS
CB_CGATE = CB_CI + C_HEADS
CB_MERGE = CB_CGATE + C_HEADS
N_CB = CB_MERGE + N_BRANCHES * D_MODEL // LANE
IN_COLS = N_CB * LANE
TILE_CB = 4

SEG_A, SEG_F, SEG_C = (0, CB_CFF), (CB_CFF, CB_CI), (CB_CI, N_CB)

PROJ_TILE_ROWS = 2048
NO_ROPE, ROPE_K, ROPE_Q = 0, 1, 2

MASK_VALUE = -1e30
HGRN_FACTOR_RANGE = 80.0

VMEM_LIMIT = 56 * 1024 * 1024


def _cparams(sem):
    return pltpu.CompilerParams(dimension_semantics=sem, vmem_limit_bytes=VMEM_LIMIT)


def _projection_plans():
    codes = [NO_ROPE] * SEG_A[1]
    dils = [1] * SEG_A[1]
    for g, (_, dil) in enumerate(DIL_GROUPS):
        for s in range(A_SLOTS):
            codes[CB_AQ + g * A_SLOTS + s] = ROPE_Q
            codes[CB_AK + g * A_SLOTS + s] = ROPE_K
            for base in (CB_AQ, CB_AK, CB_AV):
                dils[base + g * A_SLOTS + s] = dil
    for h in range(B_Q_HEADS):
        codes[CB_BQ + h] = ROPE_Q
    for h in range(B_KV_HEADS):
        codes[CB_BK + h] = ROPE_K
    plans = []
    for t in range(SEG_A[1] // TILE_CB):
        sl = slice(t * TILE_CB, (t + 1) * TILE_CB)
        assert len(set(dils[sl])) == 1
        plans.append((tuple(codes[sl]), dils[sl][0]))
    return tuple(plans)


def _rms(x, gain):
    return x * lax.rsqrt(jnp.mean(x * x, axis=-1, keepdims=True) + NORM_EPS) * gain


def _pre_norm_body(x_ref, g_ref, o_ref):
    o_ref[...] = _rms(x_ref[...], g_ref[...]).astype(o_ref.dtype)


def _pre_norm(x2, gain, *, tm):
    rows, d = x2.shape
    return pl.pallas_call(
        _pre_norm_body,
        grid=(rows // tm,),
        in_specs=[pl.BlockSpec((tm, d), lambda i: (i, 0)), pl.BlockSpec((1, d), lambda i: (0, 0))],
        out_specs=pl.BlockSpec((tm, d), lambda i: (i, 0)),
        out_shape=jax.ShapeDtypeStruct((rows, d), BF16),
        compiler_params=_cparams(("parallel",)),
        name="pre_norm",
    )(x2, gain.reshape(1, d))


def _rope_tables(seq):
    pos = jnp.arange(seq, dtype=F32)
    inv_freq = ROPE_THETA ** (-jnp.arange(0, ROT_DIM, 2, dtype=F32) / ROT_DIM)
    ang = pos[:, None] * inv_freq[None, :]
    cos, sin = jnp.cos(ang), jnp.sin(ang)
    rest = HEAD_DIM - ROT_DIM
    c_full = jnp.concatenate([cos, cos, jnp.ones((seq, rest), F32)], axis=-1)
    s_full = jnp.concatenate([-sin, sin, jnp.zeros((seq, rest), F32)], axis=-1)
    return jnp.stack([c_full, s_full])


def _rotate_half_matrix():
    src = lax.broadcasted_iota(jnp.int32, (LANE, LANE), 0)
    dst = lax.broadcasted_iota(jnp.int32, (LANE, LANE), 1)
    hit = ((dst < ROT_HALF) & (src == dst + ROT_HALF)) | (
        (dst >= ROT_HALF) & (dst < ROT_DIM) & (src == dst - ROT_HALF))
    return jnp.where(hit, 1.0, 0.0).astype(BF16)


def _apply_rope(t, cos_full, sin_signed, swap):
    swapped = jnp.dot(t.astype(BF16), swap, preferred_element_type=F32)
    return t * cos_full + swapped * sin_signed


def _in_projection_body(*refs, plans, tm):
    if plans is None:
        x_ref, w_ref, o_ref = refs
    else:
        x_ref, w_ref, rope_ref, o_ref, stage_ref = refs
    acc = jnp.dot(x_ref[...], w_ref[...], preferred_element_type=F32)

    def block(c):
        return acc[:, c * LANE:(c + 1) * LANE]

    if plans is None:
        for c in range(TILE_CB):
            o_ref[c] = block(c).astype(o_ref.dtype)
        return

    j = pl.program_id(1)
    groups = {}
    for tile, plan in enumerate(plans):
        groups.setdefault(plan, []).append(tile)
    scale = HEAD_DIM ** -0.5
    swap = _rotate_half_matrix()
    for (codes, dil), tiles in groups.items():
        cond = functools.reduce(jnp.logical_or, [j == t for t in tiles])

        @pl.when(cond)
        def _(codes=codes, dil=dil):
            for c in range(TILE_CB):
                t = block(c)
                if codes[c] != NO_ROPE:
                    t = _apply_rope(t, rope_ref[0], rope_ref[1], swap)
                if codes[c] == ROPE_Q:
                    t = t * scale
                if dil == 1:
                    o_ref[c] = t.astype(o_ref.dtype)
                else:
                    stage_ref[c] = t
            if dil > 1:
                cs = tm // dil
                for c in range(TILE_CB):
                    def phase(p, carry, c=c):
                        dst = pl.ds(pl.multiple_of(p * cs, cs), cs)
                        src = stage_ref[c, pl.ds(p, cs, stride=dil), :]
                        o_ref[c, dst, :] = src.astype(o_ref.dtype)
                        return carry

                    lax.fori_loop(0, dil, phase, 0)


def _in_projection(xn, w_bf16, rope, *, seq, seg, out_dtype, plans, tm):
    rows, d = xn.shape
    cb0, cb1 = seg
    n_tiles = (cb1 - cb0) // TILE_CB
    tn = TILE_CB * LANE
    tile0 = cb0 // TILE_CB
    in_specs = [pl.BlockSpec((tm, d), lambda i, j: (i, 0)),
                pl.BlockSpec((d, tn), lambda i, j: (0, tile0 + j))]
    args = [xn, w_bf16]
    scratch = []
    if plans is not None:
        per_seq = seq // tm
        in_specs.append(pl.BlockSpec((2, tm, LANE), lambda i, j: (0, i % per_seq, 0)))
        args.append(rope)
        scratch.append(pltpu.VMEM((TILE_CB, tm, LANE), F32))
    return pl.pallas_call(
        functools.partial(_in_projection_body, plans=plans, tm=tm),
        grid=(rows // tm, n_tiles),
        in_specs=in_specs,
        out_specs=pl.BlockSpec((TILE_CB, tm, LANE), lambda i, j: (j, i, 0)),
        out_shape=jax.ShapeDtypeStruct((cb1 - cb0, rows, LANE), out_dtype),
        scratch_shapes=scratch,
        compiler_params=_cparams(("parallel", "arbitrary")),
        name=f"in_projection_{cb0}",
    )(*args)


def _attention_body(*refs, dil, hw, cs, qb, n_sub, n_q, has_sink, want_lse, unroll):
    q_ref, kp_ref, kc_ref, kn_ref, vp_ref, vc_ref, vn_ref = refs[:7]
    rest = list(refs[7:])
    sink_ref = rest.pop(0) if has_sink else None
    o_ref = rest.pop(0)
    lse_ref = rest.pop(0) if want_lse else None
    kbuf, vbuf, band_ref, kbias_ref = rest

    t = pl.program_id(2)
    nk = qb + 2 * hw
    nks = cs + 2 * hw
    n_u = cs // qb
    kbuf[:, 0:hw] = kp_ref[...]
    kbuf[:, hw:hw + cs] = kc_ref[...]
    kbuf[:, hw + cs:] = kn_ref[...]
    vbuf[:, 0:hw, 0:LANE] = vp_ref[...]
    vbuf[:, hw:hw + cs, 0:LANE] = vc_ref[...]
    vbuf[:, hw + cs:, 0:LANE] = vn_ref[...]
    vbuf[:, :, LANE:] = jnp.ones((dil, nks, LANE), BF16)

    qi = lax.broadcasted_iota(jnp.int32, (qb, nk), 0)
    kj = lax.broadcasted_iota(jnp.int32, (qb, nk), 1)
    band_ref[...] = jnp.where(jnp.abs(kj - hw - qi) <= hw, 0.0, MASK_VALUE)
    kpos = t * cs - hw + lax.broadcasted_iota(jnp.int32, (1, nks), 1)
    kbias_ref[...] = jnp.where((kpos >= 0) & (kpos < n_sub), 0.0, MASK_VALUE)

    n_blocks = dil * n_u
    group_size = min(unroll, n_blocks)
    assert n_blocks % group_size == 0
    if has_sink:
        sink = sink_ref[...][:, :, :1]

    def group(i, carry):
        units = []
        for j in range(group_size):
            idx = i * group_size + j
            u = idx % n_u
            ph = idx // n_u
            units.append((u, ph, pl.multiple_of(u * qb, qb)))
        scores = []
        for u, ph, u0 in units:
            q = q_ref[:, ph, pl.ds(u0, qb), :].reshape(n_q * qb, LANE)
            s = lax.dot_general(q, kbuf[ph, pl.ds(u0, nk), :],
                                (((1,), (1,)), ((), ())), preferred_element_type=F32)
            scores.append(s.reshape(n_q, qb, nk) + band_ref[...] + kbias_ref[:, pl.ds(u0, nk)])
        probs, maxes = [], []
        for s in scores:
            m = jnp.max(s, axis=-1, keepdims=True)
            if has_sink:
                m = jnp.maximum(m, sink)
            maxes.append(m)
            probs.append(jnp.exp(s - m).astype(BF16).reshape(n_q * qb, nk))
        for (u, ph, u0), p, m in zip(units, probs, maxes):
            o2 = jnp.dot(p, vbuf[ph, pl.ds(u0, nk), :], preferred_element_type=F32)
            o2 = o2.reshape(n_q, qb, 2 * LANE)
            denom = o2[:, :, LANE:]
            if has_sink:
                denom = denom + jnp.exp(sink - m)
            rows = pl.ds(u0, qb) if dil == 1 else pl.ds(ph + u * (qb * dil), qb, stride=dil)
            o_ref[:, rows, :] = o2[:, :, :LANE] / denom
            if want_lse:
                lse_ref[:, rows, :] = m + jnp.log(denom)
        return carry

    lax.fori_loop(0, n_blocks // group_size, group, 0)


def _banded_attention(pa, sink, *, seq, tile, dil, hw, qb, q_cb, k_cb, v_cb, n_kv, n_q, want_lse,
                      unroll=8):
    n_cb, rows, _ = pa.shape
    bsz = rows // seq
    n_t = seq // tile
    cs = tile // dil
    per = cs // hw
    assert q_cb % n_q == 0 and cs % qb == 0 and cs % hw == 0
    view = pa.reshape(n_cb, bsz, n_t, dil, cs, LANE)
    has_sink = sink is not None

    def cur(cb, n=None):
        return pl.BlockSpec((n, None, None, dil, cs, LANE),
                            lambda b, hd, t: ((cb if n is None else cb // n) + hd, b, t, 0, 0, 0))

    def prev(cb):
        return pl.BlockSpec((None, None, None, dil, hw, LANE),
                            lambda b, hd, t: (cb + hd, b, jnp.maximum(t - 1, 0), 0, per - 1, 0))

    def nxt(cb):
        return pl.BlockSpec((None, None, None, dil, hw, LANE),
                            lambda b, hd, t: (cb + hd, b, jnp.minimum(t + 1, n_t - 1), 0, 0, 0))

    in_specs = [cur(q_cb, n_q), prev(k_cb), cur(k_cb), nxt(k_cb), prev(v_cb), cur(v_cb), nxt(v_cb)]
    args = [view] * 7
    if has_sink:
        in_specs.append(pl.BlockSpec((n_q, 1, LANE), lambda b, hd, t: (hd, 0, 0)))
        args.append(sink)
    o_spec = pl.BlockSpec((n_q, None, tile, LANE), lambda b, hd, t: (hd, b, t, 0))
    o_shape = jax.ShapeDtypeStruct((n_kv * n_q, bsz, seq, LANE), F32)
    return pl.pallas_call(
        functools.partial(_attention_body, dil=dil, hw=hw, cs=cs, qb=qb, n_sub=seq // dil, n_q=n_q,
                          has_sink=has_sink, want_lse=want_lse, unroll=unroll),
        grid=(bsz, n_kv, n_t),
        in_specs=in_specs,
        out_specs=[o_spec, o_spec] if want_lse else o_spec,
        out_shape=[o_shape, o_shape] if want_lse else o_shape,
        scratch_shapes=[pltpu.VMEM((dil, cs + 2 * hw, LANE), BF16),
                        pltpu.VMEM((dil, cs + 2 * hw, 2 * LANE), BF16),
                        pltpu.VMEM((qb, qb + 2 * hw), F32),
                        pltpu.VMEM((1, cs + 2 * hw), F32)],
        compiler_params=_cparams(("parallel", "parallel", "arbitrary")),
        name=f"banded_attention_d{dil}_w{hw}",
    )(*args)


def _hgrn_body(*refs, layer, reverse, tb, finalize, hp):
    q_ref, f_ref, v_ref, lbp_ref = refs[:4]
    rest = list(refs[4:])
    if finalize:
        of_ref, gate_ref, gain_ref = rest[:3]
        rest = rest[3:]
    o_ref, st_ref, qs_ref, ks_ref, bs_ref, sc_ref = rest
    c = C_CHUNK
    n = tb // c

    @pl.when(pl.program_id(2) == 0)
    def _():
        st_ref[...] = jnp.zeros_like(st_ref)

    row = lax.broadcasted_iota(jnp.int32, (c, c), 0)
    col = lax.broadcasted_iota(jnp.int32, (c, c), 1)
    causal = (row <= col) if reverse else (row >= col)
    tri = jnp.where(causal, 1.0, 0.0).astype(BF16)

    def widen(a):
        return jnp.concatenate([a[i * c:(i + 1) * c] for i in range(n)], axis=1)

    def gates(h):
        fz = f_ref[h]
        e = jnp.exp(-jnp.abs(fz))
        log_sig = jnp.minimum(fz, 0.0) - jnp.log(1.0 + e)
        sig_neg = jnp.where(fz >= 0.0, e, 1.0) / (1.0 + e)
        if layer == 0:
            log_f, kk = log_sig, sig_neg
        else:
            lbp = lbp_ref[:, h]
            e_l = jnp.exp(lbp - jnp.max(lbp, axis=0, keepdims=True))
            sm = e_l / jnp.sum(e_l, axis=0, keepdims=True)
            lb = sm[1]
            for i in range(2, layer + 1):
                lb = lb + sm[i]
            one_m_lb = 1.0 - lb
            tiny = jnp.float32(1e-37)
            log_lb = jnp.where(lb > 0.0, jnp.log(jnp.maximum(lb, tiny)), -jnp.inf)
            log_1m_lb = jnp.where(one_m_lb > 0.0, jnp.log(jnp.maximum(one_m_lb, tiny)), -jnp.inf)
            cand = log_1m_lb + log_sig
            log_f = jnp.maximum(log_lb, cand) + jnp.log(1.0 + jnp.exp(-jnp.abs(log_lb - cand)))
            kk = one_m_lb * sig_neg
        qh = _silu(q_ref[h].astype(F32))

        hi = log_f.astype(BF16)
        r1 = log_f - hi.astype(F32)
        mid = r1.astype(BF16)
        lo = (r1 - mid.astype(F32)).astype(BF16)
        b_wide = (jnp.dot(tri, widen(hi), preferred_element_type=F32)
                  + jnp.dot(tri, widen(mid), preferred_element_type=F32)
                  + jnp.dot(tri, widen(lo), preferred_element_type=F32))
        b3 = jnp.stack([b_wide[:, i * LANE:(i + 1) * LANE] for i in range(n)])
        return qh.reshape(n, c, LANE), kk.reshape(n, c, LANE), b3

    heads = [gates(h) for h in range(hp)]
    mids = [b3 - b3[:, c // 2 - 1:c // 2] for _, _, b3 in heads]
    span = functools.reduce(jnp.maximum, [jnp.max(jnp.abs(d3)) for d3 in mids])

    @pl.when(span <= HGRN_FACTOR_RANGE)
    def _():
        for h, ((qh3, kk3, _), d3) in enumerate(zip(heads, mids)):
            qt = (qh3 * jnp.exp(d3)).astype(BF16)
            kt = (kk3 * jnp.exp(-d3)).astype(BF16)
            sc_ref[h] = jnp.einsum('ntk,nsk->nts', qt, kt, preferred_element_type=F32)

    @pl.when(span > HGRN_FACTOR_RANGE)
    def _():
        for h, (qh3, kk3, b3) in enumerate(heads):
            qs_ref[h] = qh3
            ks_ref[h] = kk3
            bs_ref[h] = b3

        def per_chunk(i, carry):
            h, ci = i // n, i % n

            def pair(s, sc):
                rel = jnp.minimum(bs_ref[h, ci] - bs_ref[h, ci, pl.ds(s, 1), :], 0.0)
                w = qs_ref[h, ci] * ks_ref[h, ci, pl.ds(s, 1), :] * jnp.exp(rel)
                return jnp.where(col == s, jnp.sum(w, axis=-1, keepdims=True), sc)

            sc_ref[h, ci] = lax.fori_loop(0, c, pair, jnp.zeros((c, c), F32))
            return carry

        lax.fori_loop(0, hp * n, per_chunk, 0)

    for h, (qh3, kk3, b3) in enumerate(heads):
        v3 = v_ref[h].reshape(n, c, LANE)
        b_end = b3[:, 0:1] if reverse else b3[:, c - 1:c]
        scores = jnp.where(causal, sc_ref[h], 0.0).astype(BF16)
        k_in = (kk3 * jnp.exp(b_end - b3)).astype(BF16)
        kv = jnp.einsum('nsv,nsk->nvk', v3, k_in, preferred_element_type=F32)
        decay_end = jnp.exp(b_end)
        st = st_ref[h]
        incoming = [None] * n
        for i in (range(n - 1, -1, -1) if reverse else range(n)):
            incoming[i] = st.astype(BF16)
            st = st * decay_end[i] + kv[i]
        st_ref[h] = st
        o = (jnp.einsum('nts,nsv->ntv', scores, v3, preferred_element_type=F32)
             + jnp.einsum('ntk,nvk->ntv', (qh3 * jnp.exp(b3)).astype(BF16), jnp.stack(incoming),
                          preferred_element_type=F32))
        o = o.reshape(tb, LANE)
        if finalize:
            o = _rms(o + of_ref[h], gain_ref[h]) * _silu(gate_ref[h].astype(F32))
        o_ref[h] = o.astype(o_ref.dtype)


def _hgrn_scan(pa, pf, pc, lower_bounds, *, seq, layer, reverse, tb, hp, o_fwd=None, gain=None):
    bsz = pa.shape[1] // seq
    tb = min(tb, seq)
    n_t = seq // tb
    n_chunks = tb // C_CHUNK
    finalize = o_fwd is not None
    assert C_HEADS % hp == 0

    def blk(cb):
        assert cb % hp == 0
        return pl.BlockSpec((hp, None, tb, LANE),
                            lambda b, hd, t: (cb // hp + hd, b, (n_t - 1 - t) if reverse else t, 0))

    def view(p):
        return p.reshape(p.shape[0], bsz, seq, LANE)

    f_cb = (CB_CFB if reverse else CB_CFF) - SEG_F[0]
    in_specs = [blk(CB_CQ - SEG_A[0]), blk(f_cb), blk(CB_CI - SEG_C[0]),
                pl.BlockSpec((DEPTH, hp, 1, LANE), lambda b, hd, t: (0, hd, 0, 0))]
    args = [view(pa), view(pf), view(pc), lower_bounds.reshape(DEPTH, C_HEADS, 1, LANE)]
    if finalize:
        in_specs += [blk(0), blk(CB_CGATE - SEG_C[0]),
                     pl.BlockSpec((hp, 1, LANE), lambda b, hd, t: (hd, 0, 0))]
        args += [o_fwd, view(pc), gain.reshape(C_HEADS, 1, LANE)]
    out_dtype = BF16 if finalize else F32
    return pl.pallas_call(
        functools.partial(_hgrn_body, layer=layer, reverse=reverse, tb=tb, finalize=finalize, hp=hp),
        grid=(bsz, C_HEADS // hp, n_t),
        in_specs=in_specs,
        out_specs=blk(0),
        out_shape=jax.ShapeDtypeStruct((C_HEADS, bsz, seq, LANE), out_dtype),
        scratch_shapes=[pltpu.VMEM((hp, LANE, LANE), F32),
                        pltpu.VMEM((hp, n_chunks, C_CHUNK, LANE), F32),
                        pltpu.VMEM((hp, n_chunks, C_CHUNK, LANE), F32),
                        pltpu.VMEM((hp, n_chunks, C_CHUNK, LANE), F32),
                        pltpu.VMEM((hp, n_chunks, C_CHUNK, C_CHUNK), F32)],
        compiler_params=_cparams(("parallel", "parallel", "arbitrary")),
        name="hgrn_scan_bwd" if reverse else "hgrn_scan_fwd",
    )(*args)


def _silu(g):
    return g / (1.0 + jnp.exp(-g))


def _sigmoid(g):
    return 1.0 / (1.0 + jnp.exp(-g))


def _lane_concat(ref, n):
    return jnp.concatenate([ref[i] for i in range(n)], axis=-1)


def _merge_out_body(*refs, emit_next):
    (oa0_ref, oa1_ref, oa2_ref, l0_ref, l1_ref, l2_ref, ag_ref, ob_ref, bg0_ref, bg1_ref,
     yc_ref) = refs[:11]
    m_refs = refs[11:23]
    x_ref, wa_ref, wb_ref, wc_ref, wo_ref, pg_ref = refs[23:29]
    if emit_next:
        ng_ref, out_ref, xn_ref = refs[29:]
    else:
        (out_ref,) = refs[29:]

    ya = []
    for s in range(A_SLOTS):
        l0, l1, l2 = l0_ref[s], l1_ref[s], l2_ref[s]
        mx = jnp.maximum(jnp.maximum(l0, l1), l2)
        e0, e1, e2 = jnp.exp(l0 - mx), jnp.exp(l1 - mx), jnp.exp(l2 - mx)
        mixed = (e0 * oa0_ref[s] + e1 * oa1_ref[s] + e2 * oa2_ref[s]) / (e0 + e1 + e2)
        ya.append((mixed * _silu(ag_ref[s].astype(F32))).astype(BF16))
    ya = jnp.concatenate(ya, axis=-1)
    yb = []
    for hd in range(B_Q_HEADS):
        g_ref = bg0_ref if hd < 4 else bg1_ref
        yb.append((ob_ref[hd] * _silu(g_ref[hd % 4].astype(F32))).astype(BF16))
    yb = jnp.concatenate(yb, axis=-1)
    yc = jnp.concatenate([yc_ref[hd] for hd in range(C_HEADS)], axis=-1)

    proj = (jnp.dot(ya, wa_ref[...], preferred_element_type=F32),
            jnp.dot(yb, wb_ref[...], preferred_element_type=F32),
            jnp.dot(yc, wc_ref[...], preferred_element_type=F32))
    per_branch = D_MODEL // (TILE_CB * LANE)
    y = []
    for j in range(per_branch):
        cols = slice(j * TILE_CB * LANE, (j + 1) * TILE_CB * LANE)
        acc = None
        for br in range(N_BRANCHES):
            gate = _sigmoid(_lane_concat(m_refs[br * per_branch + j], TILE_CB).astype(F32))
            term = gate * proj[br][:, cols]
            acc = term if acc is None else acc + term
        y.append(acc.astype(BF16))
    y = jnp.concatenate(y, axis=-1)
    out = jnp.dot(y, wo_ref[...], preferred_element_type=F32)
    x_new = x_ref[...] + _rms(out, pg_ref[...])
    out_ref[...] = x_new
    if emit_next:
        xn_ref[...] = _rms(x_new, ng_ref[...]).astype(xn_ref.dtype)


def _merge_out(x2, pa, pc, oa, lse, ob, yc, wa, wb, wc, wo, post_gain, next_gain, *, tm):
    rows, d = x2.shape
    emit_next = next_gain is not None

    def heads(n, first=0):
        return pl.BlockSpec((n, tm, LANE), lambda i: (first // n, i, 0))

    def const(shape):
        return pl.BlockSpec(shape, lambda i: (0,) * len(shape), pipeline_mode=pl.Buffered(1))

    n_merge = N_BRANCHES * D_MODEL // (TILE_CB * LANE)
    in_specs = ([heads(A_SLOTS)] * 6
                + [heads(A_SLOTS, CB_AGATE), heads(B_Q_HEADS),
                   heads(4, CB_BGATE), heads(4, CB_BGATE + 4), heads(C_HEADS)]
                + [heads(TILE_CB, CB_MERGE - SEG_C[0] + TILE_CB * j) for j in range(n_merge)]
                + [pl.BlockSpec((tm, d), lambda i: (i, 0)),
                   const(wa.shape), const(wb.shape), const(wc.shape), const(wo.shape),
                   const((1, d))])
    args = (list(oa) + list(lse) + [pa, ob, pa, pa, yc] + [pc] * n_merge
            + [x2, wa, wb, wc, wo, post_gain.reshape(1, d)])
    row_spec = pl.BlockSpec((tm, d), lambda i: (i, 0))
    out_specs, out_shape = row_spec, jax.ShapeDtypeStruct((rows, d), F32)
    if emit_next:
        in_specs.append(const((1, d)))
        args.append(next_gain.reshape(1, d))
        out_specs = [row_spec, row_spec]
        out_shape = [out_shape, jax.ShapeDtypeStruct((rows, d), BF16)]
    return pl.pallas_call(
        functools.partial(_merge_out_body, emit_next=emit_next),
        grid=(rows // tm,),
        in_specs=in_specs,
        out_specs=out_specs,
        out_shape=out_shape,
        compiler_params=_cparams(("parallel",)),
        name="merge_out",
    )(*args)


def kernel(x, pre_norm, post_norm, w_in, sink_logits, hgrn_lower_bounds, hgrn_norm,
           w_branch_a, w_branch_b, w_branch_c, w_out):
    bsz, seq, d = x.shape
    rows = bsz * seq
    assert d == D_MODEL and w_in.shape[-1] == IN_COLS
    tile = min(PROJ_TILE_ROWS, seq)
    assert seq % tile == 0
    rope = _rope_tables(seq)
    plans = _projection_plans()
    x2 = x.reshape(rows, d)
    xn = _pre_norm(x2, pre_norm[0].astype(F32), tm=min(512, rows))
    for layer in range(DEPTH):
        w = w_in[layer].astype(BF16)
        pa = _in_projection(xn, w, rope, seq=seq, seg=SEG_A, out_dtype=BF16, plans=plans, tm=tile)
        pf = _in_projection(xn, w, None, seq=seq, seg=SEG_F, out_dtype=F32, plans=None, tm=tile)
        pc = _in_projection(xn, w, None, seq=seq, seg=SEG_C, out_dtype=BF16, plans=None, tm=tile)

        oa, lse = [], []
        for g, (window, dil) in enumerate(DIL_GROUPS):
            o_g, lse_g = _banded_attention(
                pa, None, seq=seq, tile=tile, dil=dil, hw=window // (2 * dil), qb=128,
                q_cb=CB_AQ + g * A_SLOTS, k_cb=CB_AK + g * A_SLOTS, v_cb=CB_AV + g * A_SLOTS,
                n_kv=A_SLOTS, n_q=1, want_lse=True)
            oa.append(o_g.reshape(A_SLOTS, rows, LANE))
            lse.append(lse_g.reshape(A_SLOTS, rows, LANE))

        sink = jnp.broadcast_to(sink_logits[layer].astype(F32)[:, None, None], (B_Q_HEADS, 1, LANE))
        ob = _banded_attention(
            pa, sink, seq=seq, tile=tile, dil=1, hw=B_HALF_WINDOW, qb=128,
            q_cb=CB_BQ, k_cb=CB_BK, v_cb=CB_BV,
            n_kv=B_KV_HEADS, n_q=B_Q_HEADS // B_KV_HEADS, want_lse=False, unroll=2)
        ob = ob.reshape(B_Q_HEADS, rows, LANE)

        lower = hgrn_lower_bounds.astype(F32)
        o_fwd = _hgrn_scan(pa, pf, pc, lower, seq=seq, layer=layer, reverse=False, tb=512, hp=4)
        yc = _hgrn_scan(pa, pf, pc, lower, seq=seq, layer=layer, reverse=True, tb=512, hp=4,
                        o_fwd=o_fwd, gain=hgrn_norm[layer].astype(F32))
        yc = yc.reshape(C_HEADS, rows, LANE)

        next_gain = pre_norm[layer + 1].astype(F32) if layer + 1 < DEPTH else None
        res = _merge_out(x2, pa, pc, oa, lse, ob, yc,
                         w_branch_a[layer].astype(BF16), w_branch_b[layer].astype(BF16),
                         w_branch_c[layer].astype(BF16), w_out[layer].astype(BF16),
                         post_norm[layer].astype(F32), next_gain, tm=min(256, rows))
        x2, xn = res if next_gain is not None else (res, None)
    return x2.reshape(bsz, seq, d)
```

```python
import functools

import jax
import jax.numpy as jnp
from jax import lax
from jax.experimental import pallas as pl
from jax.experimental.pallas import tpu as pltpu

F32 = jnp.float32
BF16 = jnp.bfloat16

LANE = 128
D_MODEL = 2048
DEPTH = 2
HEAD_DIM = 128
ROT_DIM = HEAD_DIM // 4
ROT_HALF = ROT_DIM // 2
ROPE_THETA = 500000.0
NORM_EPS = 1e-6

DIL_GROUPS = ((128, 1), (512, 4), (2048, 16))
A_SLOTS = 4
A_HEADS = A_SLOTS * len(DIL_GROUPS)
B_Q_HEADS = 8
B_KV_HEADS = 2
B_HALF_WINDOW = 128
C_HEADS = 8
C_CHUNK = 64
N_BRANCHES = 3

CB_AQ = 0
CB_AK = CB_AQ + A_HEADS
CB_AV = CB_AK + A_HEADS
CB_AGATE = CB_AV + A_HEADS
CB_BQ = CB_AGATE + A_SLOTS
CB_BK = CB_BQ + B_Q_HEADS
CB_BV = CB_BK + B_KV_HEADS
CB_BGATE = CB_BV + B_KV_HEADS
CB_CQ = CB_BGATE + B_Q_HEADS
CB_CFF = CB_CQ + C_HEADS
CB_CFB = CB_CFF + C_HEADS
CB_CI = CB_CFB + C_HEADS
CB_CGATE = CB_CI + C_HEADS
CB_MERGE = CB_CGATE + C_HEADS
N_CB = CB_MERGE + N_BRANCHES * D_MODEL // LANE
IN_COLS = N_CB * LANE
TILE_CB = 4

SEG_A, SEG_F, SEG_C = (0, CB_CFF), (CB_CFF, CB_CI), (CB_CI, N_CB)

PROJ_TILE_ROWS = 2048
NO_ROPE, ROPE_K, ROPE_Q = 0, 1, 2

MASK_VALUE = -1e30
HGRN_FACTOR_RANGE = 80.0

VMEM_LIMIT = 56 * 1024 * 1024


def _cparams(sem):
    return pltpu.CompilerParams(dimension_semantics=sem, vmem_limit_bytes=VMEM_LIMIT)


def _projection_plans():
    codes = [NO_ROPE] * SEG_A[1]
    dils = [1] * SEG_A[1]
    for g, (_, dil) in enumerate(DIL_GROUPS):
        for s in range(A_SLOTS):
            codes[CB_AQ + g * A_SLOTS + s] = ROPE_Q
            codes[CB_AK + g * A_SLOTS + s] = ROPE_K
            for base in (CB_AQ, CB_AK, CB_AV):
                dils[base + g * A_SLOTS + s] = dil
    for h in range(B_Q_HEADS):
        codes[CB_BQ + h] = ROPE_Q
    for h in range(B_KV_HEADS):
        codes[CB_BK + h] = ROPE_K
    plans = []
    for t in range(SEG_A[1] // TILE_CB):
        sl = slice(t * TILE_CB, (t + 1) * TILE_CB)
        assert len(set(dils[sl])) == 1
        plans.append((tuple(codes[sl]), dils[sl][0]))
    return tuple(plans)


def _rms(x, gain):
    return x * lax.rsqrt(jnp.mean(x * x, axis=-1, keepdims=True) + NORM_EPS) * gain


def _pre_norm_body(x_ref, g_ref, o_ref):
    o_ref[...] = _rms(x_ref[...], g_ref[...]).astype(o_ref.dtype)


def _pre_norm(x2, gain, *, tm):
    rows, d = x2.shape
    return pl.pallas_call(
        _pre_norm_body,
        grid=(rows // tm,),
        in_specs=[pl.BlockSpec((tm, d), lambda i: (i, 0)), pl.BlockSpec((1, d), lambda i: (0, 0))],
        out_specs=pl.BlockSpec((tm, d), lambda i: (i, 0)),
        out_shape=jax.ShapeDtypeStruct((rows, d), BF16),
        compiler_params=_cparams(("parallel",)),
        name="pre_norm",
    )(x2, gain.reshape(1, d))


def _rope_tables(seq):
    pos = jnp.arange(seq, dtype=F32)
    inv_freq = ROPE_THETA ** (-jnp.arange(0, ROT_DIM, 2, dtype=F32) / ROT_DIM)
    ang = pos[:, None] * inv_freq[None, :]
    cos, sin = jnp.cos(ang), jnp.sin(ang)
    rest = HEAD_DIM - ROT_DIM
    c_full = jnp.concatenate([cos, cos, jnp.ones((seq, rest), F32)], axis=-1)
    s_full = jnp.concatenate([-sin, sin, jnp.zeros((seq, rest), F32)], axis=-1)
    return jnp.stack([c_full, s_full])


def _rotate_half_matrix():
    src = lax.broadcasted_iota(jnp.int32, (LANE, LANE), 0)
    dst = lax.broadcasted_iota(jnp.int32, (LANE, LANE), 1)
    hit = ((dst < ROT_HALF) & (src == dst + ROT_HALF)) | (
        (dst >= ROT_HALF) & (dst < ROT_DIM) & (src == dst - ROT_HALF))
    return jnp.where(hit, 1.0, 0.0).astype(BF16)


def _apply_rope(t, cos_full, sin_signed, swap):
    swapped = jnp.dot(t.astype(BF16), swap, preferred_element_type=F32)
    return t * cos_full + swapped * sin_signed


def _in_projection_body(*refs, plans, tm):
    if plans is None:
        x_ref, w_ref, o_ref = refs
    else:
        x_ref, w_ref, rope_ref, o_ref, stage_ref = refs
    def matmul():
        return jnp.dot(x_ref[...], w_ref[...], preferred_element_type=F32)

    if plans is None:
        acc = matmul()
        for c in range(TILE_CB):
            o_ref[c] = acc[:, c * LANE:(c + 1) * LANE].astype(o_ref.dtype)
        return

    j = pl.program_id(1)
    groups = {}
    for tile, plan in enumerate(plans):
        groups.setdefault(plan, []).append(tile)
    scale = HEAD_DIM ** -0.5
    swap = _rotate_half_matrix()
    for (codes, dil), tiles in groups.items():
        cond = functools.reduce(jnp.logical_or, [j == t for t in tiles])

        @pl.when(cond)
        def _(codes=codes, dil=dil):
            acc = matmul()
            for c in range(TILE_CB):
                t = acc[:, c * LANE:(c + 1) * LANE]
                if codes[c] != NO_ROPE:
                    t = _apply_rope(t, rope_ref[0], rope_ref[1], swap)
                if codes[c] == ROPE_Q:
                    t = t * scale
                if dil == 1:
                    o_ref[c] = t.astype(o_ref.dtype)
                else:
                    stage_ref[c] = t
            if dil > 1:
                cs = tm // dil
                for c in range(TILE_CB):
                    def phase(p, carry, c=c):
                        dst = pl.ds(pl.multiple_of(p * cs, cs), cs)
                        src = stage_ref[c, pl.ds(p, cs, stride=dil), :]
                        o_ref[c, dst, :] = src.astype(o_ref.dtype)
                        return carry

                    lax.fori_loop(0, dil, phase, 0)


def _in_projection(xn, w_bf16, rope, *, seq, seg, out_dtype, plans, tm):
    rows, d = xn.shape
    cb0, cb1 = seg
    n_tiles = (cb1 - cb0) // TILE_CB
    tn = TILE_CB * LANE
    tile0 = cb0 // TILE_CB
    in_specs = [pl.BlockSpec((tm, d), lambda i, j: (i, 0)),
                pl.BlockSpec((d, tn), lambda i, j: (0, tile0 + j))]
    args = [xn, w_bf16]
    scratch = []
    if plans is not None:
        per_seq = seq // tm
        in_specs.append(pl.BlockSpec((2, tm, LANE), lambda i, j: (0, i % per_seq, 0)))
        args.append(rope)
        scratch.append(pltpu.VMEM((TILE_CB, tm, LANE), F32))
    return pl.pallas_call(
        functools.partial(_in_projection_body, plans=plans, tm=tm),
        grid=(rows // tm, n_tiles),
        in_specs=in_specs,
        out_specs=pl.BlockSpec((TILE_CB, tm, LANE), lambda i, j: (j, i, 0)),
        out_shape=jax.ShapeDtypeStruct((cb1 - cb0, rows, LANE), out_dtype),
        scratch_shapes=scratch,
        compiler_params=_cparams(("parallel", "arbitrary")),
        name=f"in_projection_{cb0}",
    )(*args)


def _band_bias(qb, hw):
    qi = lax.broadcasted_iota(jnp.int32, (qb, qb + 2 * hw), 0)
    kj = lax.broadcasted_iota(jnp.int32, (qb, qb + 2 * hw), 1)
    return jnp.where(jnp.abs(kj - hw - qi) <= hw, 0.0, MASK_VALUE)


def _attend_tile(t, q_ref, k_refs, v_refs, kbuf, vbuf, band_ref, kbias_ref, emit, *,
                 dil, hw, cs, qb, n_sub, n_q, sink, unroll):
    nk = qb + 2 * hw
    nks = cs + 2 * hw
    n_u = cs // qb
    kp_ref, kc_ref, kn_ref = k_refs
    vp_ref, vc_ref, vn_ref = v_refs
    kbuf[:, 0:hw] = kp_ref[...]
    kbuf[:, hw:hw + cs] = kc_ref[...]
    kbuf[:, hw + cs:] = kn_ref[...]
    vbuf[:, 0:hw, 0:LANE] = vp_ref[...]
    vbuf[:, hw:hw + cs, 0:LANE] = vc_ref[...]
    vbuf[:, hw + cs:, 0:LANE] = vn_ref[...]
    vbuf[:, :, LANE:] = jnp.ones((dil, nks, LANE), BF16)
    kpos = t * cs - hw + lax.broadcasted_iota(jnp.int32, (1, nks), 1)
    kbias_ref[...] = jnp.where((kpos >= 0) & (kpos < n_sub), 0.0, MASK_VALUE)

    n_blocks = dil * n_u
    group_size = min(unroll, n_blocks)
    assert n_blocks % group_size == 0

    def group(i, carry):
        units = []
        for j in range(group_size):
            idx = i * group_size + j
            u = idx % n_u
            ph = idx // n_u
            units.append((u, ph, pl.multiple_of(u * qb, qb)))
        scores = []
        for u, ph, u0 in units:
            q = q_ref[:, ph, pl.ds(u0, qb), :].reshape(n_q * qb, LANE)
            s = lax.dot_general(q, kbuf[ph, pl.ds(u0, nk), :],
                                (((1,), (1,)), ((), ())), preferred_element_type=F32)
            scores.append(s.reshape(n_q, qb, nk) + band_ref[...] + kbias_ref[:, pl.ds(u0, nk)])
        probs, maxes = [], []
        for s in scores:
            m = jnp.max(s, axis=-1, keepdims=True)
            if sink is not None:
                m = jnp.maximum(m, sink)
            maxes.append(m)
            probs.append(jnp.exp(s - m).astype(BF16).reshape(n_q * qb, nk))
        for (u, ph, u0), p, m in zip(units, probs, maxes):
            o2 = jnp.dot(p, vbuf[ph, pl.ds(u0, nk), :], preferred_element_type=F32)
            rows = pl.ds(u0, qb) if dil == 1 else pl.ds(ph + u * (qb * dil), qb, stride=dil)
            emit(rows, o2.reshape(n_q, qb, 2 * LANE), m)
        return carry

    lax.fori_loop(0, n_blocks // group_size, group, 0)


def _tile_specs(view_dims, cb, n, dil, cs, hw, n_t):
    per = cs // hw
    lead = cb if n is None else cb // n

    def spec(rows, tile_of, blk):
        return pl.BlockSpec((n, None, None, dil, rows, LANE),
                            lambda b, hd, t: (lead + hd, b, tile_of(t), 0, blk, 0))

    return (spec(hw, lambda t: jnp.maximum(t - 1, 0), per - 1),
            spec(cs, lambda t: t, 0),
            spec(hw, lambda t: jnp.minimum(t + 1, n_t - 1), 0))


def _windowed_attention_body(q_ref, kp_ref, kc_ref, kn_ref, vp_ref, vc_ref, vn_ref, sink_ref, o_ref,
                             kbuf, vbuf, band_ref, kbias_ref, *, hw, cs, qb, n_sub, n_q, unroll):
    band_ref[...] = _band_bias(qb, hw)
    sink = sink_ref[...][:, :, :1]

    def emit(rows, o2, m):
        denom = o2[:, :, LANE:] + jnp.exp(sink - m)
        o_ref[:, rows, :] = (o2[:, :, :LANE] / denom).astype(o_ref.dtype)

    _attend_tile(pl.program_id(2), q_ref, (kp_ref, kc_ref, kn_ref), (vp_ref, vc_ref, vn_ref),
                 kbuf, vbuf, band_ref, kbias_ref, emit, dil=1, hw=hw, cs=cs, qb=qb, n_sub=n_sub,
                 n_q=n_q, sink=sink, unroll=unroll)


def _windowed_attention(pa, sink, *, seq, tile, hw, qb, unroll):
    n_cb, rows, _ = pa.shape
    bsz = rows // seq
    n_t = seq // tile
    n_q = B_Q_HEADS // B_KV_HEADS
    view = pa.reshape(n_cb, bsz, n_t, 1, tile, LANE)
    in_specs = ([_tile_specs(view.shape, CB_BQ, n_q, 1, tile, hw, n_t)[1]]
                + list(_tile_specs(view.shape, CB_BK, None, 1, tile, hw, n_t))
                + list(_tile_specs(view.shape, CB_BV, None, 1, tile, hw, n_t))
                + [pl.BlockSpec((n_q, 1, LANE), lambda b, hd, t: (hd, 0, 0))])
    out = pl.pallas_call(
        functools.partial(_windowed_attention_body, hw=hw, cs=tile, qb=qb, n_sub=seq, n_q=n_q,
                          unroll=unroll),
        grid=(bsz, B_KV_HEADS, n_t),
        in_specs=in_specs,
        out_specs=pl.BlockSpec((n_q, None, tile, LANE), lambda b, hd, t: (hd, b, t, 0)),
        out_shape=jax.ShapeDtypeStruct((B_Q_HEADS, bsz, seq, LANE), BF16),
        scratch_shapes=[pltpu.VMEM((1, tile + 2 * hw, LANE), BF16),
                        pltpu.VMEM((1, tile + 2 * hw, 2 * LANE), BF16),
                        pltpu.VMEM((qb, qb + 2 * hw), F32),
                        pltpu.VMEM((1, tile + 2 * hw), F32)],
        compiler_params=_cparams(("parallel", "parallel", "arbitrary")),
        name="windowed_attention",
    )(*([view] * 7 + [sink]))
    return out.reshape(B_Q_HEADS, rows, LANE)


def _dilated_attention_body(*refs, tile, hw, qb, seq, unroll):
    n_g = len(DIL_GROUPS)
    group_refs = [refs[7 * g:7 * g + 7] for g in range(n_g)]
    gate_ref, o_ref = refs[7 * n_g:7 * n_g + 2]
    scratch = refs[7 * n_g + 2:]
    bufs = [scratch[3 * g:3 * g + 3] for g in range(n_g)]
    band_ref, num_ref, max_ref, den_ref = scratch[3 * n_g:]
    band_ref[...] = _band_bias(qb, hw)

    for g, (_, dil) in enumerate(DIL_GROUPS):
        q_ref, kp_ref, kc_ref, kn_ref, vp_ref, vc_ref, vn_ref = group_refs[g]
        kbuf, vbuf, kbias_ref = bufs[g]

        def emit(rows, o2, m, g=g):
            num_ref[g, rows, :] = o2[0, :, :LANE]
            den_ref[g, rows, :] = o2[0, :, LANE:]
            max_ref[g, rows, :] = jnp.broadcast_to(m[0], (qb, LANE))

        _attend_tile(pl.program_id(2), q_ref, (kp_ref, kc_ref, kn_ref), (vp_ref, vc_ref, vn_ref),
                     kbuf, vbuf, band_ref, kbias_ref, emit, dil=dil, hw=hw, cs=tile // dil, qb=qb,
                     n_sub=seq // dil, n_q=1, sink=None, unroll=unroll)

    top = functools.reduce(jnp.maximum, [max_ref[g] for g in range(n_g)])
    weights = [jnp.exp(max_ref[g] - top) for g in range(n_g)]
    num = sum(w * num_ref[g] for g, w in enumerate(weights))
    den = sum(w * den_ref[g] for g, w in enumerate(weights))
    o_ref[...] = (num / den * _silu(gate_ref[...].astype(F32))).astype(o_ref.dtype)


def _dilated_attention(pa, *, seq, tile, qb, unroll):
    n_cb, rows, _ = pa.shape
    bsz = rows // seq
    n_t = seq // tile
    hws = {window // (2 * dil) for window, dil in DIL_GROUPS}
    assert len(hws) == 1
    hw = hws.pop()
    in_specs, args, scratch = [], [], []
    for g, (_, dil) in enumerate(DIL_GROUPS):
        cs = tile // dil
        assert cs % qb == 0 and cs % hw == 0
        view = pa.reshape(n_cb, bsz, n_t, dil, cs, LANE)
        in_specs += ([_tile_specs(view.shape, CB_AQ + g * A_SLOTS, 1, dil, cs, hw, n_t)[1]]
                     + list(_tile_specs(view.shape, CB_AK + g * A_SLOTS, None, dil, cs, hw, n_t))
                     + list(_tile_specs(view.shape, CB_AV + g * A_SLOTS, None, dil, cs, hw, n_t)))
        args += [view] * 7
        scratch += [pltpu.VMEM((dil, cs + 2 * hw, LANE), BF16),
                    pltpu.VMEM((dil, cs + 2 * hw, 2 * LANE), BF16),
                    pltpu.VMEM((1, cs + 2 * hw), F32)]
    row_spec = pl.BlockSpec((None, None, tile, LANE), lambda b, hd, t: (hd, b, t, 0))
    in_specs.append(pl.BlockSpec((None, None, tile, LANE), lambda b, hd, t: (CB_AGATE + hd, b, t, 0)))
    args.append(pa.reshape(n_cb, bsz, seq, LANE))
    n_g = len(DIL_GROUPS)
    scratch += [pltpu.VMEM((qb, qb + 2 * hw), F32),
                pltpu.VMEM((n_g, tile, LANE), F32),
                pltpu.VMEM((n_g, tile, LANE), F32),
                pltpu.VMEM((n_g, tile, LANE), F32)]
    out = pl.pallas_call(
        functools.partial(_dilated_attention_body, tile=tile, hw=hw, qb=qb, seq=seq, unroll=unroll),
        grid=(bsz, A_SLOTS, n_t),
        in_specs=in_specs,
        out_specs=row_spec,
        out_shape=jax.ShapeDtypeStruct((A_SLOTS, bsz, seq, LANE), BF16),
        scratch_shapes=scratch,
        compiler_params=_cparams(("parallel", "parallel", "arbitrary")),
        name="dilated_attention",
    )(*args)
    return out.reshape(A_SLOTS, rows, LANE)


def _hgrn_body(*refs, layer, reverse, tb, finalize, hp):
    q_ref, f_ref, v_ref, lbp_ref = refs[:4]
    rest = list(refs[4:])
    if finalize:
        of_ref, gate_ref, gain_ref = rest[:3]
        rest = rest[3:]
    o_ref, st_ref, qs_ref, ks_ref, bs_ref, sc_ref = rest
    c = C_CHUNK
    n = tb // c

    @pl.when(pl.program_id(2) == 0)
    def _():
        st_ref[...] = jnp.zeros_like(st_ref)

    row = lax.broadcasted_iota(jnp.int32, (c, c), 0)
    col = lax.broadcasted_iota(jnp.int32, (c, c), 1)
    causal = (row <= col) if reverse else (row >= col)
    tri = jnp.where(causal, 1.0, 0.0).astype(BF16)

    def widen(a):
        return jnp.concatenate([a[i * c:(i + 1) * c] for i in range(n)], axis=1)

    def gates(h):
        fz = f_ref[h]
        e = jnp.exp(-jnp.abs(fz))
        log_sig = jnp.minimum(fz, 0.0) - jnp.log(1.0 + e)
        sig_neg = jnp.where(fz >= 0.0, e, 1.0) / (1.0 + e)
        if layer == 0:
            log_f, kk = log_sig, sig_neg
        else:
            lbp = lbp_ref[:, h]
            e_l = jnp.exp(lbp - jnp.max(lbp, axis=0, keepdims=True))
            sm = e_l / jnp.sum(e_l, axis=0, keepdims=True)
            lb = sm[1]
            for i in range(2, layer + 1):
                lb = lb + sm[i]
            one_m_lb = 1.0 - lb
            tiny = jnp.float32(1e-37)
            log_lb = jnp.where(lb > 0.0, jnp.log(jnp.maximum(lb, tiny)), -jnp.inf)
            log_1m_lb = jnp.where(one_m_lb > 0.0, jnp.log(jnp.maximum(one_m_lb, tiny)), -jnp.inf)
            cand = log_1m_lb + log_sig
            log_f = jnp.maximum(log_lb, cand) + jnp.log(1.0 + jnp.exp(-jnp.abs(log_lb - cand)))
            kk = one_m_lb * sig_neg
        qh = _silu(q_ref[h].astype(F32))

        hi = log_f.astype(BF16)
        lo = (log_f - hi.astype(F32)).astype(BF16)
        b_wide = (jnp.dot(tri, widen(hi), preferred_element_type=F32)
                  + jnp.dot(tri, widen(lo), preferred_element_type=F32))
        b3 = jnp.stack([b_wide[:, i * LANE:(i + 1) * LANE] for i in range(n)])
        return qh.reshape(n, c, LANE), kk.reshape(n, c, LANE), b3

    heads = [gates(h) for h in range(hp)]
    mids = [b3 - b3[:, c // 2 - 1:c // 2] for _, _, b3 in heads]
    span = functools.reduce(jnp.maximum, [jnp.max(jnp.abs(d3)) for d3 in mids])

    @pl.when(span <= HGRN_FACTOR_RANGE)
    def _():
        for h, ((qh3, kk3, _), d3) in enumerate(zip(heads, mids)):
            qt = (qh3 * jnp.exp(d3)).astype(BF16)
            kt = (kk3 * jnp.exp(-d3)).astype(BF16)
            sc_ref[h] = jnp.einsum('ntk,nsk->nts', qt, kt, preferred_element_type=F32)

    @pl.when(span > HGRN_FACTOR_RANGE)
    def _():
        for h, (qh3, kk3, b3) in enumerate(heads):
            qs_ref[h] = qh3
            ks_ref[h] = kk3
            bs_ref[h] = b3

        def per_chunk(i, carry):
            h, ci = i // n, i % n

            def pair(s, sc):
                rel = jnp.minimum(bs_ref[h, ci] - bs_ref[h, ci, pl.ds(s, 1), :], 0.0)
                w = qs_ref[h, ci] * ks_ref[h, ci, pl.ds(s, 1), :] * jnp.exp(rel)
                return jnp.where(col == s, jnp.sum(w, axis=-1, keepdims=True), sc)

            sc_ref[h, ci] = lax.fori_loop(0, c, pair, jnp.zeros((c, c), F32))
            return carry

        lax.fori_loop(0, hp * n, per_chunk, 0)

    for h, (qh3, kk3, b3) in enumerate(heads):
        v3 = v_ref[h].reshape(n, c, LANE)
        b_end = b3[:, 0:1] if reverse else b3[:, c - 1:c]
        scores = jnp.where(causal, sc_ref[h], 0.0).astype(BF16)
        k_in = (kk3 * jnp.exp(b_end - b3)).astype(BF16)
        kv = jnp.einsum('nsv,nsk->nvk', v3, k_in, preferred_element_type=F32)
        decay_end = jnp.exp(b_end)
        st = st_ref[h]
        incoming = [None] * n
        for i in (range(n - 1, -1, -1) if reverse else range(n)):
            incoming[i] = st.astype(BF16)
            st = st * decay_end[i] + kv[i]
        st_ref[h] = st
        o = (jnp.einsum('nts,nsv->ntv', scores, v3, preferred_element_type=F32)
             + jnp.einsum('ntk,nvk->ntv', (qh3 * jnp.exp(b3)).astype(BF16), jnp.stack(incoming),
                          preferred_element_type=F32))
        o = o.reshape(tb, LANE)
        if finalize:
            o = _rms(o + of_ref[h], gain_ref[h]) * _silu(gate_ref[h].astype(F32))
        o_ref[h] = o.astype(o_ref.dtype)


def _hgrn_scan(pa, pf, pc, lower_bounds, *, seq, layer, reverse, tb, hp, o_fwd=None, gain=None):
    bsz = pa.shape[1] // seq
    tb = min(tb, seq)
    n_t = seq // tb
    n_chunks = tb // C_CHUNK
    finalize = o_fwd is not None
    assert C_HEADS % hp == 0

    def blk(cb):
        assert cb % hp == 0
        return pl.BlockSpec((hp, None, tb, LANE),
                            lambda b, hd, t: (cb // hp + hd, b, (n_t - 1 - t) if reverse else t, 0))

    def view(p):
        return p.reshape(p.shape[0], bsz, seq, LANE)

    f_cb = (CB_CFB if reverse else CB_CFF) - SEG_F[0]
    in_specs = [blk(CB_CQ - SEG_A[0]), blk(f_cb), blk(CB_CI - SEG_C[0]),
                pl.BlockSpec((DEPTH, hp, 1, LANE), lambda b, hd, t: (0, hd, 0, 0))]
    args = [view(pa), view(pf), view(pc), lower_bounds.reshape(DEPTH, C_HEADS, 1, LANE)]
    if finalize:
        in_specs += [blk(0), blk(CB_CGATE - SEG_C[0]),
                     pl.BlockSpec((hp, 1, LANE), lambda b, hd, t: (hd, 0, 0))]
        args += [o_fwd, view(pc), gain.reshape(C_HEADS, 1, LANE)]
    out_dtype = BF16 if finalize else F32
    return pl.pallas_call(
        functools.partial(_hgrn_body, layer=layer, reverse=reverse, tb=tb, finalize=finalize, hp=hp),
        grid=(bsz, C_HEADS // hp, n_t),
        in_specs=in_specs,
        out_specs=blk(0),
        out_shape=jax.ShapeDtypeStruct((C_HEADS, bsz, seq, LANE), out_dtype),
        scratch_shapes=[pltpu.VMEM((hp, LANE, LANE), F32),
                        pltpu.VMEM((hp, n_chunks, C_CHUNK, LANE), F32),
                        pltpu.VMEM((hp, n_chunks, C_CHUNK, LANE), F32),
                        pltpu.VMEM((hp, n_chunks, C_CHUNK, LANE), F32),
                        pltpu.VMEM((hp, n_chunks, C_CHUNK, C_CHUNK), F32)],
        compiler_params=_cparams(("parallel", "parallel", "arbitrary")),
        name="hgrn_scan_bwd" if reverse else "hgrn_scan_fwd",
    )(*args)


def _silu(g):
    return g / (1.0 + jnp.exp(-g))


def _sigmoid(g):
    return 1.0 / (1.0 + jnp.exp(-g))


def _lane_concat(ref, n):
    return jnp.concatenate([ref[i] for i in range(n)], axis=-1)


def _merge_out_body(*refs, emit_next):
    ya_ref, ob_ref, bg0_ref, bg1_ref, yc_ref = refs[:5]
    m_refs = refs[5:17]
    x_ref, wa_ref, wb_ref, wc_ref, wo_ref, pg_ref = refs[17:23]
    if emit_next:
        ng_ref, out_ref, xn_ref = refs[23:]
    else:
        (out_ref,) = refs[23:]

    ya = _lane_concat(ya_ref, A_SLOTS)
    yb = []
    for hd in range(B_Q_HEADS):
        g_ref = bg0_ref if hd < 4 else bg1_ref
        yb.append((ob_ref[hd].astype(F32) * _silu(g_ref[hd % 4].astype(F32))).astype(BF16))
    yb = jnp.concatenate(yb, axis=-1)
    yc = _lane_concat(yc_ref, C_HEADS)

    proj = (jnp.dot(ya, wa_ref[...], preferred_element_type=F32),
            jnp.dot(yb, wb_ref[...], preferred_element_type=F32),
            jnp.dot(yc, wc_ref[...], preferred_element_type=F32))
    per_branch = D_MODEL // (TILE_CB * LANE)
    y = []
    for j in range(per_branch):
        cols = slice(j * TILE_CB * LANE, (j + 1) * TILE_CB * LANE)
        acc = None
        for br in range(N_BRANCHES):
            gate = _sigmoid(_lane_concat(m_refs[br * per_branch + j], TILE_CB).astype(F32))
            term = gate * proj[br][:, cols]
            acc = term if acc is None else acc + term
        y.append(acc.astype(BF16))
    y = jnp.concatenate(y, axis=-1)
    out = jnp.dot(y, wo_ref[...], preferred_element_type=F32)
    x_new = x_ref[...] + _rms(out, pg_ref[...])
    out_ref[...] = x_new
    if emit_next:
        xn_ref[...] = _rms(x_new, ng_ref[...]).astype(xn_ref.dtype)


def _merge_out(x2, pa, pc, ya, ob, yc, wa, wb, wc, wo, post_gain, next_gain, *, tm):
    rows, d = x2.shape
    emit_next = next_gain is not None

    def heads(n, first=0):
        return pl.BlockSpec((n, tm, LANE), lambda i: (first // n, i, 0))

    def const(shape):
        return pl.BlockSpec(shape, lambda i: (0,) * len(shape), pipeline_mode=pl.Buffered(1))

    n_merge = N_BRANCHES * D_MODEL // (TILE_CB * LANE)
    in_specs = ([heads(A_SLOTS), heads(B_Q_HEADS),
                 heads(4, CB_BGATE), heads(4, CB_BGATE + 4), heads(C_HEADS)]
                + [heads(TILE_CB, CB_MERGE - SEG_C[0] + TILE_CB * j) for j in range(n_merge)]
                + [pl.BlockSpec((tm, d), lambda i: (i, 0)),
                   const(wa.shape), const(wb.shape), const(wc.shape), const(wo.shape),
                   const((1, d))])
    args = ([ya, ob, pa, pa, yc] + [pc] * n_merge
            + [x2, wa, wb, wc, wo, post_gain.reshape(1, d)])
    row_spec = pl.BlockSpec((tm, d), lambda i: (i, 0))
    out_specs, out_shape = row_spec, jax.ShapeDtypeStruct((rows, d), F32)
    if emit_next:
        in_specs.append(const((1, d)))
        args.append(next_gain.reshape(1, d))
        out_specs = [row_spec, row_spec]
        out_shape = [out_shape, jax.ShapeDtypeStruct((rows, d), BF16)]
    return pl.pallas_call(
        functools.partial(_merge_out_body, emit_next=emit_next),
        grid=(rows // tm,),
        in_specs=in_specs,
        out_specs=out_specs,
        out_shape=out_shape,
        compiler_params=_cparams(("parallel",)),
        name="merge_out",
    )(*args)


def kernel(x, pre_norm, post_norm, w_in, sink_logits, hgrn_lower_bounds, hgrn_norm,
           w_branch_a, w_branch_b, w_branch_c, w_out):
    bsz, seq, d = x.shape
    rows = bsz * seq
    assert d == D_MODEL and w_in.shape[-1] == IN_COLS
    tile = min(PROJ_TILE_ROWS, seq)
    assert seq % tile == 0
    rope = _rope_tables(seq)
    plans = _projection_plans()
    x2 = x.reshape(rows, d)
    xn = _pre_norm(x2, pre_norm[0].astype(F32), tm=min(512, rows))
    for layer in range(DEPTH):
        w = w_in[layer].astype(BF16)
        pa = _in_projection(xn, w, rope, seq=seq, seg=SEG_A, out_dtype=BF16, plans=plans, tm=tile)
        pf = _in_projection(xn, w, None, seq=seq, seg=SEG_F, out_dtype=F32, plans=None, tm=tile)
        pc = _in_projection(xn, w, None, seq=seq, seg=SEG_C, out_dtype=BF16, plans=None, tm=tile)

        ya = _dilated_attention(pa, seq=seq, tile=tile, qb=128, unroll=8)
        sink = jnp.broadcast_to(sink_logits[layer].astype(F32)[:, None, None], (B_Q_HEADS, 1, LANE))
        ob = _windowed_attention(pa, sink, seq=seq, tile=tile, hw=B_HALF_WINDOW, qb=128, unroll=2)

        lower = hgrn_lower_bounds.astype(F32)
        o_fwd = _hgrn_scan(pa, pf, pc, lower, seq=seq, layer=layer, reverse=False, tb=512, hp=4)
        yc = _hgrn_scan(pa, pf, pc, lower, seq=seq, layer=layer, reverse=True, tb=512, hp=4,
                        o_fwd=o_fwd, gain=hgrn_norm[layer].astype(F32))
        yc = yc.reshape(C_HEADS, rows, LANE)

        next_gain = pre_norm[layer + 1].astype(F32) if layer + 1 < DEPTH else None
        res = _merge_out(x2, pa, pc, ya, ob, yc,
                         w_branch_a[layer].astype(BF16), w_branch_b[layer].astype(BF16),
                         w_branch_c[layer].astype(BF16), w_out[layer].astype(BF16),
                         post_norm[layer].astype(F32), next_gain, tm=min(256, rows))
        x2, xn = res if next_gain is not None else (res, None)
    return x2.reshape(bsz, seq, d)
```

```python
import functools

import jax
import jax.numpy as jnp
from jax import lax
from jax.experimental import pallas as pl
from jax.experimental.pallas import tpu as pltpu

F32 = jnp.float32
BF16 = jnp.bfloat16

LANE = 128
D_MODEL = 2048
DEPTH = 2
HEAD_DIM = 128
ROT_DIM = HEAD_DIM // 4
ROT_HALF = ROT_DIM // 2
ROPE_THETA = 500000.0
NORM_EPS = 1e-6

DIL_GROUPS = ((128, 1), (512, 4), (2048, 16))
A_SLOTS = 4
A_HEADS = A_SLOTS * len(DIL_GROUPS)
B_Q_HEADS = 8
B_KV_HEADS = 2
B_HALF_WINDOW = 128
C_HEADS = 8
C_CHUNK = 64
N_BRANCHES = 3

CB_AQ = 0
CB_AK = CB_AQ + A_HEADS
CB_AV = CB_AK + A_HEADS
CB_AGATE = CB_AV + A_HEADS
CB_BQ = CB_AGATE + A_SLOTS
CB_BK = CB_BQ + B_Q_HEADS
CB_BV = CB_BK + B_KV_HEADS
CB_BGATE = CB_BV + B_KV_HEADS
CB_CQ = CB_BGATE + B_Q_HEADS
CB_CFF = CB_CQ + C_HEADS
CB_CFB = CB_CFF + C_HEADS
CB_CI = CB_CFB + C_HEADS
CB_CGATE = CB_CI + C_HEADS
CB_MERGE = CB_CGATE + C_HEADS
N_CB = CB_MERGE + N_BRANCHES * D_MODEL // LANE
IN_COLS = N_CB * LANE
TILE_CB = 4

SEG_A, SEG_F, SEG_C = (0, CB_CFF), (CB_CFF, CB_CI), (CB_CI, N_CB)

PROJ_TILE_ROWS = 2048
NO_ROPE, ROPE_K, ROPE_Q = 0, 1, 2

MASK_VALUE = -1e30
LOG2_E = 1.4426950408889634
HGRN_FACTOR_RANGE = 115.0

VMEM_LIMIT = 56 * 1024 * 1024


def _cparams(sem):
    return pltpu.CompilerParams(dimension_semantics=sem, vmem_limit_bytes=VMEM_LIMIT)


def _projection_plans():
    codes = [NO_ROPE] * SEG_A[1]
    dils = [1] * SEG_A[1]
    for g, (_, dil) in enumerate(DIL_GROUPS):
        for s in range(A_SLOTS):
            codes[CB_AQ + g * A_SLOTS + s] = ROPE_Q
            codes[CB_AK + g * A_SLOTS + s] = ROPE_K
            for base in (CB_AQ, CB_AK, CB_AV):
                dils[base + g * A_SLOTS + s] = dil
    for h in range(B_Q_HEADS):
        codes[CB_BQ + h] = ROPE_Q
    for h in range(B_KV_HEADS):
        codes[CB_BK + h] = ROPE_K
    plans = []
    for t in range(SEG_A[1] // TILE_CB):
        sl = slice(t * TILE_CB, (t + 1) * TILE_CB)
        assert len(set(dils[sl])) == 1
        plans.append((tuple(codes[sl]), dils[sl][0]))
    return tuple(plans)


def _rms(x, gain):
    return x * lax.rsqrt(jnp.mean(x * x, axis=-1, keepdims=True) + NORM_EPS) * gain


def _pre_norm_body(x_ref, g_ref, o_ref):
    o_ref[...] = _rms(x_ref[...], g_ref[...]).astype(o_ref.dtype)


def _pre_norm(x2, gain, *, tm):
    rows, d = x2.shape
    return pl.pallas_call(
        _pre_norm_body,
        grid=(rows // tm,),
        in_specs=[pl.BlockSpec((tm, d), lambda i: (i, 0)), pl.BlockSpec((1, d), lambda i: (0, 0))],
        out_specs=pl.BlockSpec((tm, d), lambda i: (i, 0)),
        out_shape=jax.ShapeDtypeStruct((rows, d), BF16),
        compiler_params=_cparams(("parallel",)),
        name="pre_norm",
    )(x2, gain.reshape(1, d))


def _rope_tables(seq):
    pos = jnp.arange(seq, dtype=F32)
    inv_freq = ROPE_THETA ** (-jnp.arange(0, ROT_DIM, 2, dtype=F32) / ROT_DIM)
    ang = pos[:, None] * inv_freq[None, :]
    cos, sin = jnp.cos(ang), jnp.sin(ang)
    rest = HEAD_DIM - ROT_DIM
    c_full = jnp.concatenate([cos, cos, jnp.ones((seq, rest), F32)], axis=-1)
    s_full = jnp.concatenate([-sin, sin, jnp.zeros((seq, rest), F32)], axis=-1)
    return jnp.stack([c_full, s_full])


def _rotate_half_matrix():
    src = lax.broadcasted_iota(jnp.int32, (LANE, LANE), 0)
    dst = lax.broadcasted_iota(jnp.int32, (LANE, LANE), 1)
    hit = ((dst < ROT_HALF) & (src == dst + ROT_HALF)) | (
        (dst >= ROT_HALF) & (dst < ROT_DIM) & (src == dst - ROT_HALF))
    return jnp.where(hit, 1.0, 0.0).astype(BF16)


def _apply_rope(t, cos_full, sin_signed, swap):
    swapped = jnp.dot(t.astype(BF16), swap, preferred_element_type=F32)
    return t * cos_full + swapped * sin_signed


def _in_projection_body(*refs, plans, tm):
    if plans is None:
        x_ref, w_ref, o_ref = refs
    else:
        x_ref, w_ref, rope_ref, o_ref, stage_ref = refs
    acc = jnp.dot(x_ref[...], w_ref[...], preferred_element_type=F32)
    for c in range(TILE_CB):
        o_ref[c] = acc[:, c * LANE:(c + 1) * LANE].astype(o_ref.dtype)
    if plans is None:
        return

    j = pl.program_id(1)
    groups = {}
    for tile, plan in enumerate(plans):
        if plan != ((NO_ROPE,) * TILE_CB, 1):
            groups.setdefault(plan, []).append(tile)
    scale = HEAD_DIM ** -0.5 * LOG2_E
    swap = _rotate_half_matrix()
    for (codes, dil), tiles in groups.items():
        cond = functools.reduce(jnp.logical_or, [j == t for t in tiles])

        @pl.when(cond)
        def _(codes=codes, dil=dil):
            for c in range(TILE_CB):
                if codes[c] == NO_ROPE and dil == 1:
                    continue
                t = acc[:, c * LANE:(c + 1) * LANE]
                if codes[c] != NO_ROPE:
                    t = _apply_rope(t, rope_ref[0], rope_ref[1], swap)
                if codes[c] == ROPE_Q:
                    t = t * scale
                if dil == 1:
                    o_ref[c] = t.astype(o_ref.dtype)
                else:
                    stage_ref[c] = t
            if dil > 1:
                cs = tm // dil
                for c in range(TILE_CB):
                    def phase(p, carry, c=c):
                        dst = pl.ds(pl.multiple_of(p * cs, cs), cs)
                        src = stage_ref[c, pl.ds(p, cs, stride=dil), :]
                        o_ref[c, dst, :] = src.astype(o_ref.dtype)
                        return carry

                    lax.fori_loop(0, dil, phase, 0)


def _in_projection(xn, w_bf16, rope, *, seq, seg, out_dtype, plans, tm):
    rows, d = xn.shape
    cb0, cb1 = seg
    n_tiles = (cb1 - cb0) // TILE_CB
    tn = TILE_CB * LANE
    tile0 = cb0 // TILE_CB
    in_specs = [pl.BlockSpec((tm, d), lambda i, j: (i, 0)),
                pl.BlockSpec((d, tn), lambda i, j: (0, tile0 + j))]
    args = [xn, w_bf16]
    scratch = []
    if plans is not None:
        per_seq = seq // tm
        in_specs.append(pl.BlockSpec((2, tm, LANE), lambda i, j: (0, i % per_seq, 0)))
        args.append(rope)
        scratch.append(pltpu.VMEM((TILE_CB, tm, LANE), F32))
    return pl.pallas_call(
        functools.partial(_in_projection_body, plans=plans, tm=tm),
        grid=(rows // tm, n_tiles),
        in_specs=in_specs,
        out_specs=pl.BlockSpec((TILE_CB, tm, LANE), lambda i, j: (j, i, 0)),
        out_shape=jax.ShapeDtypeStruct((cb1 - cb0, rows, LANE), out_dtype),
        scratch_shapes=scratch,
        compiler_params=_cparams(("parallel", "arbitrary")),
        name=f"in_projection_{cb0}",
    )(*args)


def _band_bias(qb, hw):
    qi = lax.broadcasted_iota(jnp.int32, (qb, qb + 2 * hw), 0)
    kj = lax.broadcasted_iota(jnp.int32, (qb, qb + 2 * hw), 1)
    return jnp.where(jnp.abs(kj - hw - qi) <= hw, 0.0, MASK_VALUE)


def _first_grid_step():
    first = pl.program_id(0) == 0
    for axis in (1, 2):
        first = first & (pl.program_id(axis) == 0)
    return first


def _attend_tile(t, q_ref, k_refs, v_refs, kbuf, vbuf, band_ref, kbias_ref, emit, *,
                 dil, hw, cs, qb, n_sub, n_q, sink, unroll):
    nk = qb + 2 * hw
    nks = cs + 2 * hw
    n_u = cs // qb
    kp_ref, kc_ref, kn_ref = k_refs
    vp_ref, vc_ref, vn_ref = v_refs
    kbuf[:, 0:hw] = kp_ref[...]
    kbuf[:, hw:hw + cs] = kc_ref[...]
    kbuf[:, hw + cs:] = kn_ref[...]
    vbuf[:, 0:hw, 0:LANE] = vp_ref[...]
    vbuf[:, hw:hw + cs, 0:LANE] = vc_ref[...]
    vbuf[:, hw + cs:, 0:LANE] = vn_ref[...]

    @pl.when(_first_grid_step())
    def _():
        vbuf[:, :, LANE:] = jnp.ones((dil, nks, LANE), BF16)

    kpos = t * cs - hw + lax.broadcasted_iota(jnp.int32, (1, nks), 1)
    kbias_ref[...] = jnp.where((kpos >= 0) & (kpos < n_sub), 0.0, MASK_VALUE)

    n_blocks = dil * n_u
    group_size = min(unroll, n_blocks)
    assert n_blocks % group_size == 0

    def group(i, carry):
        units = []
        for j in range(group_size):
            idx = i * group_size + j
            u = idx % n_u
            ph = idx // n_u
            units.append((u, ph, pl.multiple_of(u * qb, qb)))
        scores = []
        for u, ph, u0 in units:
            q = q_ref[:, ph, pl.ds(u0, qb), :].reshape(n_q * qb, LANE)
            s = lax.dot_general(q, kbuf[ph, pl.ds(u0, nk), :],
                                (((1,), (1,)), ((), ())), preferred_element_type=F32)
            scores.append(s.reshape(n_q, qb, nk) + band_ref[...] + kbias_ref[:, pl.ds(u0, nk)])
        probs, maxes = [], []
        for s in scores:
            m = jnp.max(s, axis=-1, keepdims=True)
            if sink is not None:
                m = jnp.maximum(m, sink)
            maxes.append(m)
            probs.append(jnp.exp2(s - m).astype(BF16).reshape(n_q * qb, nk))
        for (u, ph, u0), p, m in zip(units, probs, maxes):
            o2 = jnp.dot(p, vbuf[ph, pl.ds(u0, nk), :], preferred_element_type=F32)
            rows = pl.ds(u0, qb) if dil == 1 else pl.ds(ph + u * (qb * dil), qb, stride=dil)
            emit(rows, o2.reshape(n_q, qb, 2 * LANE), m)
        return carry

    lax.fori_loop(0, n_blocks // group_size, group, 0)


def _tile_specs(view_dims, cb, n, dil, cs, hw, n_t):
    per = cs // hw
    lead = cb if n is None else cb // n

    def spec(rows, tile_of, blk):
        return pl.BlockSpec((n, None, None, dil, rows, LANE),
                            lambda b, hd, t: (lead + hd, b, tile_of(t), 0, blk, 0))

    return (spec(hw, lambda t: jnp.maximum(t - 1, 0), per - 1),
            spec(cs, lambda t: t, 0),
            spec(hw, lambda t: jnp.minimum(t + 1, n_t - 1), 0))


def _windowed_attention_body(q_ref, kp_ref, kc_ref, kn_ref, vp_ref, vc_ref, vn_ref, sink_ref, o_ref,
                             kbuf, vbuf, band_ref, kbias_ref, *, hw, cs, qb, n_sub, n_q, unroll):
    band_ref[...] = _band_bias(qb, hw)
    sink = sink_ref[...][:, :, :1] * LOG2_E

    def emit(rows, o2, m):
        denom = o2[:, :, LANE:] + jnp.exp2(sink - m)
        o_ref[:, rows, :] = (o2[:, :, :LANE] / denom).astype(o_ref.dtype)

    _attend_tile(pl.program_id(2), q_ref, (kp_ref, kc_ref, kn_ref), (vp_ref, vc_ref, vn_ref),
                 kbuf, vbuf, band_ref, kbias_ref, emit, dil=1, hw=hw, cs=cs, qb=qb, n_sub=n_sub,
                 n_q=n_q, sink=sink, unroll=unroll)


def _windowed_attention(pa, sink, *, seq, tile, hw, qb, unroll):
    n_cb, rows, _ = pa.shape
    bsz = rows // seq
    n_t = seq // tile
    n_q = B_Q_HEADS // B_KV_HEADS
    view = pa.reshape(n_cb, bsz, n_t, 1, tile, LANE)
    in_specs = ([_tile_specs(view.shape, CB_BQ, n_q, 1, tile, hw, n_t)[1]]
                + list(_tile_specs(view.shape, CB_BK, None, 1, tile, hw, n_t))
                + list(_tile_specs(view.shape, CB_BV, None, 1, tile, hw, n_t))
                + [pl.BlockSpec((n_q, 1, LANE), lambda b, hd, t: (hd, 0, 0))])
    out = pl.pallas_call(
        functools.partial(_windowed_attention_body, hw=hw, cs=tile, qb=qb, n_sub=seq, n_q=n_q,
                          unroll=unroll),
        grid=(bsz, B_KV_HEADS, n_t),
        in_specs=in_specs,
        out_specs=pl.BlockSpec((n_q, None, tile, LANE), lambda b, hd, t: (hd, b, t, 0)),
        out_shape=jax.ShapeDtypeStruct((B_Q_HEADS, bsz, seq, LANE), BF16),
        scratch_shapes=[pltpu.VMEM((1, tile + 2 * hw, LANE), BF16),
                        pltpu.VMEM((1, tile + 2 * hw, 2 * LANE), BF16),
                        pltpu.VMEM((qb, qb + 2 * hw), F32),
                        pltpu.VMEM((1, tile + 2 * hw), F32)],
        compiler_params=_cparams(("arbitrary", "arbitrary", "arbitrary")),
        name="windowed_attention",
    )(*([view] * 7 + [sink]))
    return out.reshape(B_Q_HEADS, rows, LANE)


def _dilated_attention_body(*refs, tile, hw, qb, seq, unroll):
    n_g = len(DIL_GROUPS)
    group_refs = [refs[7 * g:7 * g + 7] for g in range(n_g)]
    gate_ref, o_ref = refs[7 * n_g:7 * n_g + 2]
    scratch = refs[7 * n_g + 2:]
    bufs = [scratch[3 * g:3 * g + 3] for g in range(n_g)]
    band_ref, num_ref, max_ref, den_ref = scratch[3 * n_g:]
    band_ref[...] = _band_bias(qb, hw)

    for g, (_, dil) in enumerate(DIL_GROUPS):
        q_ref, kp_ref, kc_ref, kn_ref, vp_ref, vc_ref, vn_ref = group_refs[g]
        kbuf, vbuf, kbias_ref = bufs[g]

        def emit(rows, o2, m, g=g):
            num_ref[g, rows, :] = o2[0, :, :LANE]
            den_ref[g, rows, :] = o2[0, :, LANE:]
            max_ref[g, rows, :] = jnp.broadcast_to(m[0], (qb, LANE))

        _attend_tile(pl.program_id(2), q_ref, (kp_ref, kc_ref, kn_ref), (vp_ref, vc_ref, vn_ref),
                     kbuf, vbuf, band_ref, kbias_ref, emit, dil=dil, hw=hw, cs=tile // dil, qb=qb,
                     n_sub=seq // dil, n_q=1, sink=None, unroll=unroll)

    top = functools.reduce(jnp.maximum, [max_ref[g] for g in range(n_g)])
    weights = [jnp.exp2(max_ref[g] - top) for g in range(n_g)]
    num = sum(w * num_ref[g] for g, w in enumerate(weights))
    den = sum(w * den_ref[g] for g, w in enumerate(weights))
    o_ref[...] = (num / den * _silu(gate_ref[...].astype(F32))).astype(o_ref.dtype)


def _dilated_attention(pa, *, seq, tile, qb, unroll):
    n_cb, rows, _ = pa.shape
    bsz = rows // seq
    n_t = seq // tile
    hws = {window // (2 * dil) for window, dil in DIL_GROUPS}
    assert len(hws) == 1
    hw = hws.pop()
    in_specs, args, scratch = [], [], []
    for g, (_, dil) in enumerate(DIL_GROUPS):
        cs = tile // dil
        assert cs % qb == 0 and cs % hw == 0
        view = pa.reshape(n_cb, bsz, n_t, dil, cs, LANE)
        in_specs += ([_tile_specs(view.shape, CB_AQ + g * A_SLOTS, 1, dil, cs, hw, n_t)[1]]
                     + list(_tile_specs(view.shape, CB_AK + g * A_SLOTS, None, dil, cs, hw, n_t))
                     + list(_tile_specs(view.shape, CB_AV + g * A_SLOTS, None, dil, cs, hw, n_t)))
        args += [view] * 7
        scratch += [pltpu.VMEM((dil, cs + 2 * hw, LANE), BF16),
                    pltpu.VMEM((dil, cs + 2 * hw, 2 * LANE), BF16),
                    pltpu.VMEM((1, cs + 2 * hw), F32)]
    row_spec = pl.BlockSpec((None, None, tile, LANE), lambda b, hd, t: (hd, b, t, 0))
    in_specs.append(pl.BlockSpec((None, None, tile, LANE), lambda b, hd, t: (CB_AGATE + hd, b, t, 0)))
    args.append(pa.reshape(n_cb, bsz, seq, LANE))
    n_g = len(DIL_GROUPS)
    scratch += [pltpu.VMEM((qb, qb + 2 * hw), F32),
                pltpu.VMEM((n_g, tile, LANE), F32),
                pltpu.VMEM((n_g, tile, LANE), F32),
                pltpu.VMEM((n_g, tile, LANE), F32)]
    out = pl.pallas_call(
        functools.partial(_dilated_attention_body, tile=tile, hw=hw, qb=qb, seq=seq, unroll=unroll),
        grid=(bsz, A_SLOTS, n_t),
        in_specs=in_specs,
        out_specs=row_spec,
        out_shape=jax.ShapeDtypeStruct((A_SLOTS, bsz, seq, LANE), BF16),
        scratch_shapes=scratch,
        compiler_params=_cparams(("arbitrary", "arbitrary", "arbitrary")),
        name="dilated_attention",
    )(*args)
    return out.reshape(A_SLOTS, rows, LANE)


def _hgrn_body(*refs, layer, reverse, tb, finalize, hp):
    q_ref, f_ref, v_ref, lbp_ref = refs[:4]
    rest = list(refs[4:])
    if finalize:
        of_ref, gate_ref, gain_ref = rest[:3]
        rest = rest[3:]
    o_ref, st_ref, qs_ref, ks_ref, bs_ref, sc_ref = rest
    c = C_CHUNK
    n = tb // c

    @pl.when(pl.program_id(2) == 0)
    def _():
        st_ref[...] = jnp.zeros_like(st_ref)

    row = lax.broadcasted_iota(jnp.int32, (c, c), 0)
    col = lax.broadcasted_iota(jnp.int32, (c, c), 1)
    causal = (row <= col) if reverse else (row >= col)
    tri = jnp.where(causal, 1.0, 0.0).astype(BF16)

    def widen(a):
        return jnp.concatenate([a[i * c:(i + 1) * c] for i in range(n)], axis=1)

    def gates(h):
        fz = f_ref[h]
        e = jnp.exp2(jnp.abs(fz) * -LOG2_E)
        log_sig = jnp.minimum(fz, 0.0) * LOG2_E - jnp.log2(1.0 + e)
        sig_neg = jnp.where(fz >= 0.0, e, 1.0) / (1.0 + e)
        if layer == 0:
            log_f, kk = log_sig, sig_neg
        else:
            lbp = lbp_ref[:, h]
            e_l = jnp.exp(lbp - jnp.max(lbp, axis=0, keepdims=True))
            sm = e_l / jnp.sum(e_l, axis=0, keepdims=True)
            lb = sm[1]
            for i in range(2, layer + 1):
                lb = lb + sm[i]
            one_m_lb = 1.0 - lb
            tiny = jnp.float32(1e-37)
            log_lb = jnp.where(lb > 0.0, jnp.log2(jnp.maximum(lb, tiny)), -jnp.inf)
            log_1m_lb = jnp.where(one_m_lb > 0.0, jnp.log2(jnp.maximum(one_m_lb, tiny)), -jnp.inf)
            cand = log_1m_lb + log_sig
            log_f = (jnp.maximum(log_lb, cand)
                     + jnp.log2(1.0 + jnp.exp2(-jnp.abs(log_lb - cand))))
            kk = one_m_lb * sig_neg
        qh = _silu(q_ref[h].astype(F32))

        hi = log_f.astype(BF16)
        lo = (log_f - hi.astype(F32)).astype(BF16)
        b_wide = (jnp.dot(tri, widen(hi), preferred_element_type=F32)
                  + jnp.dot(tri, widen(lo), preferred_element_type=F32))
        b3 = jnp.stack([b_wide[:, i * LANE:(i + 1) * LANE] for i in range(n)])
        return qh.reshape(n, c, LANE), kk.reshape(n, c, LANE), b3

    heads = [gates(h) for h in range(hp)]
    mids = [b3 - b3[:, c // 2 - 1:c // 2] for _, _, b3 in heads]
    span = functools.reduce(jnp.maximum, [jnp.max(jnp.abs(d3)) for d3 in mids])

    @pl.when(span <= HGRN_FACTOR_RANGE)
    def _():
        for h, ((qh3, kk3, _), d3) in enumerate(zip(heads, mids)):
            grow = jnp.exp2(d3)
            qt = (qh3 * grow).astype(BF16)
            kt = (kk3 / grow).astype(BF16)
            sc_ref[h] = jnp.einsum('ntk,nsk->nts', qt, kt, preferred_element_type=F32)

    @pl.when(span > HGRN_FACTOR_RANGE)
    def _():
        for h, (qh3, kk3, b3) in enumerate(heads):
            qs_ref[h] = qh3
            ks_ref[h] = kk3
            bs_ref[h] = b3

        def per_chunk(i, carry):
            h, ci = i // n, i % n

            def pair(s, sc):
                rel = jnp.minimum(bs_ref[h, ci] - bs_ref[h, ci, pl.ds(s, 1), :], 0.0)
                w = qs_ref[h, ci] * ks_ref[h, ci, pl.ds(s, 1), :] * jnp.exp2(rel)
                return jnp.where(col == s, jnp.sum(w, axis=-1, keepdims=True), sc)

            sc_ref[h, ci] = lax.fori_loop(0, c, pair, jnp.zeros((c, c), F32))
            return carry

        lax.fori_loop(0, hp * n, per_chunk, 0)

    for h, (qh3, kk3, b3) in enumerate(heads):
        v3 = v_ref[h].reshape(n, c, LANE)
        b_end = b3[:, 0:1] if reverse else b3[:, c - 1:c]
        scores = jnp.where(causal, sc_ref[h], 0.0).astype(BF16)
        k_in = (kk3 * jnp.exp2(b_end - b3)).astype(BF16)
        kv = jnp.einsum('nsv,nsk->nvk', v3, k_in, preferred_element_type=F32)
        decay_end = jnp.exp2(b_end)
        st = st_ref[h]
        incoming = [None] * n
        for i in (range(n - 1, -1, -1) if reverse else range(n)):
            incoming[i] = st.astype(BF16)
            st = st * decay_end[i] + kv[i]
        st_ref[h] = st
        o = (jnp.einsum('nts,nsv->ntv', scores, v3, preferred_element_type=F32)
             + jnp.einsum('ntk,nvk->ntv', (qh3 * jnp.exp2(b3)).astype(BF16), jnp.stack(incoming),
                          preferred_element_type=F32))
        o = o.reshape(tb, LANE)
        if finalize:
            o = _rms(o + of_ref[h], gain_ref[h]) * _silu(gate_ref[h].astype(F32))
        o_ref[h] = o.astype(o_ref.dtype)


def _hgrn_scan(pa, pf, pc, lower_bounds, *, seq, layer, reverse, tb, hp, o_fwd=None, gain=None):
    bsz = pa.shape[1] // seq
    tb = min(tb, seq)
    n_t = seq // tb
    n_chunks = tb // C_CHUNK
    finalize = o_fwd is not None
    assert C_HEADS % hp == 0

    def blk(cb):
        assert cb % hp == 0
        return pl.BlockSpec((hp, None, tb, LANE),
                            lambda b, hd, t: (cb // hp + hd, b, (n_t - 1 - t) if reverse else t, 0))

    def view(p):
        return p.reshape(p.shape[0], bsz, seq, LANE)

    f_cb = (CB_CFB if reverse else CB_CFF) - SEG_F[0]
    in_specs = [blk(CB_CQ - SEG_A[0]), blk(f_cb), blk(CB_CI - SEG_C[0]),
                pl.BlockSpec((DEPTH, hp, 1, LANE), lambda b, hd, t: (0, hd, 0, 0))]
    args = [view(pa), view(pf), view(pc), lower_bounds.reshape(DEPTH, C_HEADS, 1, LANE)]
    if finalize:
        in_specs += [blk(0), blk(CB_CGATE - SEG_C[0]),
                     pl.BlockSpec((hp, 1, LANE), lambda b, hd, t: (hd, 0, 0))]
        args += [o_fwd, view(pc), gain.reshape(C_HEADS, 1, LANE)]
    out_dtype = BF16 if finalize else F32
    return pl.pallas_call(
        functools.partial(_hgrn_body, layer=layer, reverse=reverse, tb=tb, finalize=finalize, hp=hp),
        grid=(bsz, C_HEADS // hp, n_t),
        in_specs=in_specs,
        out_specs=blk(0),
        out_shape=jax.ShapeDtypeStruct((C_HEADS, bsz, seq, LANE), out_dtype),
        scratch_shapes=[pltpu.VMEM((hp, LANE, LANE), F32),
                        pltpu.VMEM((hp, n_chunks, C_CHUNK, LANE), F32),
                        pltpu.VMEM((hp, n_chunks, C_CHUNK, LANE), F32),
                        pltpu.VMEM((hp, n_chunks, C_CHUNK, LANE), F32),
                        pltpu.VMEM((hp, n_chunks, C_CHUNK, C_CHUNK), F32)],
        compiler_params=_cparams(("parallel", "parallel", "arbitrary")),
        name="hgrn_scan_bwd" if reverse else "hgrn_scan_fwd",
    )(*args)


def _silu(g):
    return g / (1.0 + jnp.exp(-g))


def _sigmoid(g):
    return 1.0 / (1.0 + jnp.exp(-g))


def _lane_concat(ref, n):
    return jnp.concatenate([ref[i] for i in range(n)], axis=-1)


def _merge_out_body(*refs, emit_next):
    ya_ref, ob_ref, bg0_ref, bg1_ref, yc_ref = refs[:5]
    m_refs = refs[5:17]
    x_ref, wa_ref, wb_ref, wc_ref, wo_ref, pg_ref = refs[17:23]
    if emit_next:
        ng_ref, out_ref, xn_ref = refs[23:]
    else:
        (out_ref,) = refs[23:]

    ya = _lane_concat(ya_ref, A_SLOTS)
    yb = []
    for hd in range(B_Q_HEADS):
        g_ref = bg0_ref if hd < 4 else bg1_ref
        yb.append((ob_ref[hd].astype(F32) * _silu(g_ref[hd % 4].astype(F32))).astype(BF16))
    yb = jnp.concatenate(yb, axis=-1)
    yc = _lane_concat(yc_ref, C_HEADS)

    proj = (jnp.dot(ya, wa_ref[...], preferred_element_type=F32),
            jnp.dot(yb, wb_ref[...], preferred_element_type=F32),
            jnp.dot(yc, wc_ref[...], preferred_element_type=F32))
    per_branch = D_MODEL // (TILE_CB * LANE)
    y = []
    for j in range(per_branch):
        cols = slice(j * TILE_CB * LANE, (j + 1) * TILE_CB * LANE)
        acc = None
        for br in range(N_BRANCHES):
            gate = _sigmoid(_lane_concat(m_refs[br * per_branch + j], TILE_CB).astype(F32))
            term = gate * proj[br][:, cols]
            acc = term if acc is None else acc + term
        y.append(acc.astype(BF16))
    y = jnp.concatenate(y, axis=-1)
    out = jnp.dot(y, wo_ref[...], preferred_element_type=F32)
    x_new = x_ref[...] + _rms(out, pg_ref[...])
    out_ref[...] = x_new
    if emit_next:
        xn_ref[...] = _rms(x_new, ng_ref[...]).astype(xn_ref.dtype)


def _merge_out(x2, pa, pc, ya, ob, yc, wa, wb, wc, wo, post_gain, next_gain, *, tm):
    rows, d = x2.shape
    emit_next = next_gain is not None

    def heads(n, first=0):
        return pl.BlockSpec((n, tm, LANE), lambda i: (first // n, i, 0))

    def const(shape):
        return pl.BlockSpec(shape, lambda i: (0,) * len(shape), pipeline_mode=pl.Buffered(1))

    n_merge = N_BRANCHES * D_MODEL // (TILE_CB * LANE)
    in_specs = ([heads(A_SLOTS), heads(B_Q_HEADS),
                 heads(4, CB_BGATE), heads(4, CB_BGATE + 4), heads(C_HEADS)]
                + [heads(TILE_CB, CB_MERGE - SEG_C[0] + TILE_CB * j) for j in range(n_merge)]
                + [pl.BlockSpec((tm, d), lambda i: (i, 0)),
                   const(wa.shape), const(wb.shape), const(wc.shape), const(wo.shape),
                   const((1, d))])
    args = ([ya, ob, pa, pa, yc] + [pc] * n_merge
            + [x2, wa, wb, wc, wo, post_gain.reshape(1, d)])
    row_spec = pl.BlockSpec((tm, d), lambda i: (i, 0))
    out_specs, out_shape = row_spec, jax.ShapeDtypeStruct((rows, d), F32)
    if emit_next:
        in_specs.append(const((1, d)))
        args.append(next_gain.reshape(1, d))
        out_specs = [row_spec, row_spec]
        out_shape = [out_shape, jax.ShapeDtypeStruct((rows, d), BF16)]
    return pl.pallas_call(
        functools.partial(_merge_out_body, emit_next=emit_next),
        grid=(rows // tm,),
        in_specs=in_specs,
        out_specs=out_specs,
        out_shape=out_shape,
        compiler_params=_cparams(("parallel",)),
        name="merge_out",
    )(*args)


def kernel(x, pre_norm, post_norm, w_in, sink_logits, hgrn_lower_bounds, hgrn_norm,
           w_branch_a, w_branch_b, w_branch_c, w_out):
    bsz, seq, d = x.shape
    rows = bsz * seq
    assert d == D_MODEL and w_in.shape[-1] == IN_COLS
    tile = min(PROJ_TILE_ROWS, seq)
    assert seq % tile == 0
    rope = _rope_tables(seq)
    plans = _projection_plans()
    x2 = x.reshape(rows, d)
    xn = _pre_norm(x2, pre_norm[0].astype(F32), tm=min(512, rows))
    for layer in range(DEPTH):
        w = w_in[layer].astype(BF16)
        pa = _in_projection(xn, w, rope, seq=seq, seg=SEG_A, out_dtype=BF16, plans=plans, tm=tile)
        pf = _in_projection(xn, w, None, seq=seq, seg=SEG_F, out_dtype=F32, plans=None, tm=tile)
        pc = _in_projection(xn, w, None, seq=seq, seg=SEG_C, out_dtype=BF16, plans=None, tm=tile)

        ya = _dilated_attention(pa, seq=seq, tile=tile, qb=128, unroll=8)
        sink = jnp.broadcast_to(sink_logits[layer].astype(F32)[:, None, None], (B_Q_HEADS, 1, LANE))
        ob = _windowed_attention(pa, sink, seq=seq, tile=tile, hw=B_HALF_WINDOW, qb=128, unroll=2)

        lower = hgrn_lower_bounds.astype(F32)
        o_fwd = _hgrn_scan(pa, pf, pc, lower, seq=seq, layer=layer, reverse=False, tb=512, hp=4)
        yc = _hgrn_scan(pa, pf, pc, lower, seq=seq, layer=layer, reverse=True, tb=512, hp=4,
                        o_fwd=o_fwd, gain=hgrn_norm[layer].astype(F32))
        yc = yc.reshape(C_HEADS, rows, LANE)

        next_gain = pre_norm[layer + 1].astype(F32) if layer + 1 < DEPTH else None
        res = _merge_out(x2, pa, pc, ya, ob, yc,
                         w_branch_a[layer].astype(BF16), w_branch_b[layer].astype(BF16),
                         w_branch_c[layer].astype(BF16), w_out[layer].astype(BF16),
                         post_norm[layer].astype(F32), next_gain, tm=min(256, rows))
        x2, xn = res if next_gain is not None else (res, None)
    return x2.reshape(bsz, seq, d)
```

```python
import functools

import jax
import jax.numpy as jnp
from jax import lax
from jax.experimental import pallas as pl
from jax.experimental.pallas import tpu as pltpu

F32 = jnp.float32
BF16 = jnp.bfloat16

LANE = 128
D_MODEL = 2048
DEPTH = 2
HEAD_DIM = 128
ROT_DIM = HEAD_DIM // 4
ROT_HALF = ROT_DIM // 2
ROPE_THETA = 500000.0
NORM_EPS = 1e-6

DIL_GROUPS = ((128, 1), (512, 4), (2048, 16))
A_SLOTS = 4
A_HEADS = A_SLOTS * len(DIL_GROUPS)
B_Q_HEADS = 8
B_KV_HEADS = 2
B_HALF_WINDOW = 128
C_HEADS = 8
C_CHUNK = 64
N_BRANCHES = 3

CB_AQ = 0
CB_AK = CB_AQ + A_HEADS
CB_AV = CB_AK + A_HEADS
CB_AGATE = CB_AV + A_HEADS
CB_BQ = CB_AGATE + A_SLOTS
CB_BK = CB_BQ + B_Q_HEADS
CB_BV = CB_BK + B_KV_HEADS
CB_BGATE = CB_BV + B_KV_HEADS
CB_CQ = CB_BGATE + B_Q_HEADS
CB_CFF = CB_CQ + C_HEADS
CB_CFB = CB_CFF + C_HEADS
CB_CI = CB_CFB + C_HEADS
CB_CGATE = CB_CI + C_HEADS
CB_MERGE = CB_CGATE + C_HEADS
N_CB = CB_MERGE + N_BRANCHES * D_MODEL // LANE
IN_COLS = N_CB * LANE
TILE_CB = 4

SEG_A, SEG_F, SEG_C = (0, CB_CFF), (CB_CFF, CB_CI), (CB_CI, N_CB)

PROJ_TILE_ROWS = 2048
NO_ROPE, ROPE_K, ROPE_Q = 0, 1, 2

MASK_VALUE = -1e30
LOG2_E = 1.4426950408889634
HGRN_FACTOR_RANGE = 115.0

VMEM_LIMIT = 56 * 1024 * 1024


def _cparams(sem):
    return pltpu.CompilerParams(dimension_semantics=sem, vmem_limit_bytes=VMEM_LIMIT)


def _projection_plans():
    codes = [NO_ROPE] * SEG_A[1]
    dils = [1] * SEG_A[1]
    for g, (_, dil) in enumerate(DIL_GROUPS):
        for s in range(A_SLOTS):
            codes[CB_AQ + g * A_SLOTS + s] = ROPE_Q
            codes[CB_AK + g * A_SLOTS + s] = ROPE_K
            for base in (CB_AQ, CB_AK, CB_AV):
                dils[base + g * A_SLOTS + s] = dil
    for h in range(B_Q_HEADS):
        codes[CB_BQ + h] = ROPE_Q
    for h in range(B_KV_HEADS):
        codes[CB_BK + h] = ROPE_K
    plans = []
    for t in range(SEG_A[1] // TILE_CB):
        sl = slice(t * TILE_CB, (t + 1) * TILE_CB)
        assert len(set(dils[sl])) == 1
        plans.append((tuple(codes[sl]), dils[sl][0]))
    return tuple(plans)


def _rms(x, gain):
    return x * lax.rsqrt(jnp.mean(x * x, axis=-1, keepdims=True) + NORM_EPS) * gain


def _pre_norm_body(x_ref, g_ref, o_ref):
    o_ref[...] = _rms(x_ref[...], g_ref[...]).astype(o_ref.dtype)


def _pre_norm(x2, gain, *, tm):
    rows, d = x2.shape
    return pl.pallas_call(
        _pre_norm_body,
        grid=(rows // tm,),
        in_specs=[pl.BlockSpec((tm, d), lambda i: (i, 0)), pl.BlockSpec((1, d), lambda i: (0, 0))],
        out_specs=pl.BlockSpec((tm, d), lambda i: (i, 0)),
        out_shape=jax.ShapeDtypeStruct((rows, d), BF16),
        compiler_params=_cparams(("parallel",)),
        name="pre_norm",
    )(x2, gain.reshape(1, d))


def _rope_tables(seq):
    pos = jnp.arange(seq, dtype=F32)
    inv_freq = ROPE_THETA ** (-jnp.arange(0, ROT_DIM, 2, dtype=F32) / ROT_DIM)
    ang = pos[:, None] * inv_freq[None, :]
    cos, sin = jnp.cos(ang), jnp.sin(ang)
    rest = HEAD_DIM - ROT_DIM
    c_full = jnp.concatenate([cos, cos, jnp.ones((seq, rest), F32)], axis=-1)
    s_full = jnp.concatenate([-sin, sin, jnp.zeros((seq, rest), F32)], axis=-1)
    return jnp.stack([c_full, s_full])


def _rotate_half_matrix():
    src = lax.broadcasted_iota(jnp.int32, (LANE, LANE), 0)
    dst = lax.broadcasted_iota(jnp.int32, (LANE, LANE), 1)
    hit = ((dst < ROT_HALF) & (src == dst + ROT_HALF)) | (
        (dst >= ROT_HALF) & (dst < ROT_DIM) & (src == dst - ROT_HALF))
    return jnp.where(hit, 1.0, 0.0).astype(BF16)


def _apply_rope(t, cos_full, sin_signed, swap):
    swapped = jnp.dot(t.astype(BF16), swap, preferred_element_type=F32)
    return t * cos_full + swapped * sin_signed


def _in_projection_body(*refs, plans, tm):
    if plans is None:
        x_ref, w_ref, o_ref = refs
    else:
        x_ref, w_ref, rope_ref, o_ref, stage_ref = refs
    acc = jnp.dot(x_ref[...], w_ref[...], preferred_element_type=F32)
    for c in range(TILE_CB):
        o_ref[c] = acc[:, c * LANE:(c + 1) * LANE].astype(o_ref.dtype)
    if plans is None:
        return

    j = pl.program_id(1)
    groups = {}
    for tile, plan in enumerate(plans):
        if plan != ((NO_ROPE,) * TILE_CB, 1):
            groups.setdefault(plan, []).append(tile)
    scale = HEAD_DIM ** -0.5 * LOG2_E
    swap = _rotate_half_matrix()
    for (codes, dil), tiles in groups.items():
        cond = functools.reduce(jnp.logical_or, [j == t for t in tiles])

        @pl.when(cond)
        def _(codes=codes, dil=dil):
            for c in range(TILE_CB):
                if codes[c] == NO_ROPE and dil == 1:
                    continue
                t = acc[:, c * LANE:(c + 1) * LANE]
                if codes[c] != NO_ROPE:
                    t = _apply_rope(t, rope_ref[0], rope_ref[1], swap)
                if codes[c] == ROPE_Q:
                    t = t * scale
                if dil == 1:
                    o_ref[c] = t.astype(o_ref.dtype)
                else:
                    stage_ref[c] = t
            if dil > 1:
                cs = tm // dil
                for c in range(TILE_CB):
                    def phase(p, carry, c=c):
                        dst = pl.ds(pl.multiple_of(p * cs, cs), cs)
                        src = stage_ref[c, pl.ds(p, cs, stride=dil), :]
                        o_ref[c, dst, :] = src.astype(o_ref.dtype)
                        return carry

                    lax.fori_loop(0, dil, phase, 0)


def _in_projection(xn, w_bf16, rope, *, seq, seg, out_dtype, plans, tm):
    rows, d = xn.shape
    cb0, cb1 = seg
    n_tiles = (cb1 - cb0) // TILE_CB
    tn = TILE_CB * LANE
    tile0 = cb0 // TILE_CB
    in_specs = [pl.BlockSpec((tm, d), lambda i, j: (i, 0)),
                pl.BlockSpec((d, tn), lambda i, j: (0, tile0 + j))]
    args = [xn, w_bf16]
    scratch = []
    if plans is not None:
        per_seq = seq // tm
        in_specs.append(pl.BlockSpec((2, tm, LANE), lambda i, j: (0, i % per_seq, 0)))
        args.append(rope)
        scratch.append(pltpu.VMEM((TILE_CB, tm, LANE), F32))
    return pl.pallas_call(
        functools.partial(_in_projection_body, plans=plans, tm=tm),
        grid=(rows // tm, n_tiles),
        in_specs=in_specs,
        out_specs=pl.BlockSpec((TILE_CB, tm, LANE), lambda i, j: (j, i, 0)),
        out_shape=jax.ShapeDtypeStruct((cb1 - cb0, rows, LANE), out_dtype),
        scratch_shapes=scratch,
        compiler_params=_cparams(("parallel", "arbitrary")),
        name=f"in_projection_{cb0}",
    )(*args)


def _band_bias(qb, hw):
    qi = lax.broadcasted_iota(jnp.int32, (qb, qb + 2 * hw), 0)
    kj = lax.broadcasted_iota(jnp.int32, (qb, qb + 2 * hw), 1)
    return jnp.where(jnp.abs(kj - hw - qi) <= hw, 0.0, MASK_VALUE)


def _first_grid_step():
    first = pl.program_id(0) == 0
    for axis in (1, 2):
        first = first & (pl.program_id(axis) == 0)
    return first


def _attend_tile(t, q_ref, k_refs, v_refs, kbuf, vbuf, band_ref, kbias_ref, emit, *,
                 dil, hw, cs, qb, n_sub, n_q, sink, unroll):
    nk = qb + 2 * hw
    nks = cs + 2 * hw
    n_u = cs // qb
    kp_ref, kc_ref, kn_ref = k_refs
    vp_ref, vc_ref, vn_ref = v_refs
    kbuf[:, 0:hw] = kp_ref[...]
    kbuf[:, hw:hw + cs] = kc_ref[...]
    kbuf[:, hw + cs:] = kn_ref[...]
    vbuf[:, 0:hw, 0:LANE] = vp_ref[...]
    vbuf[:, hw:hw + cs, 0:LANE] = vc_ref[...]
    vbuf[:, hw + cs:, 0:LANE] = vn_ref[...]

    @pl.when(_first_grid_step())
    def _():
        vbuf[:, :, LANE:] = jnp.ones((dil, nks, LANE), BF16)

    kpos = t * cs - hw + lax.broadcasted_iota(jnp.int32, (1, nks), 1)
    kbias_ref[...] = jnp.where((kpos >= 0) & (kpos < n_sub), 0.0, MASK_VALUE)

    n_blocks = dil * n_u
    group_size = min(unroll, n_blocks)
    assert n_blocks % group_size == 0

    def group(i, carry):
        units = []
        for j in range(group_size):
            idx = i * group_size + j
            u = idx % n_u
            ph = idx // n_u
            units.append((u, ph, pl.multiple_of(u * qb, qb)))
        scores = []
        for u, ph, u0 in units:
            q = q_ref[:, ph, pl.ds(u0, qb), :].reshape(n_q * qb, LANE)
            s = lax.dot_general(q, kbuf[ph, pl.ds(u0, nk), :],
                                (((1,), (1,)), ((), ())), preferred_element_type=F32)
            scores.append(s.reshape(n_q, qb, nk) + band_ref[...] + kbias_ref[:, pl.ds(u0, nk)])
        probs, maxes = [], []
        for s in scores:
            m = jnp.max(s, axis=-1, keepdims=True)
            if sink is not None:
                m = jnp.maximum(m, sink)
            maxes.append(m)
            probs.append(jnp.exp2(s - m).astype(BF16).reshape(n_q * qb, nk))
        for (u, ph, u0), p, m in zip(units, probs, maxes):
            o2 = jnp.dot(p, vbuf[ph, pl.ds(u0, nk), :], preferred_element_type=F32)
            rows = pl.ds(u0, qb) if dil == 1 else pl.ds(ph + u * (qb * dil), qb, stride=dil)
            emit(rows, o2.reshape(n_q, qb, 2 * LANE), m)
        return carry

    lax.fori_loop(0, n_blocks // group_size, group, 0)


def _tile_specs(view_dims, cb, n, dil, cs, hw, n_t):
    per = cs // hw
    lead = cb if n is None else cb // n

    def spec(rows, tile_of, blk):
        return pl.BlockSpec((n, None, None, dil, rows, LANE),
                            lambda b, hd, t: (lead + hd, b, tile_of(t), 0, blk, 0))

    return (spec(hw, lambda t: jnp.maximum(t - 1, 0), per - 1),
            spec(cs, lambda t: t, 0),
            spec(hw, lambda t: jnp.minimum(t + 1, n_t - 1), 0))


def _windowed_attention_body(q_ref, kp_ref, kc_ref, kn_ref, vp_ref, vc_ref, vn_ref, sink_ref, o_ref,
                             kbuf, vbuf, band_ref, kbias_ref, *, hw, cs, qb, n_sub, n_q, unroll):
    band_ref[...] = _band_bias(qb, hw)
    sink = sink_ref[...][:, :, :1] * LOG2_E

    def emit(rows, o2, m):
        denom = o2[:, :, LANE:] + jnp.exp2(sink - m)
        o_ref[:, rows, :] = (o2[:, :, :LANE] / denom).astype(o_ref.dtype)

    _attend_tile(pl.program_id(2), q_ref, (kp_ref, kc_ref, kn_ref), (vp_ref, vc_ref, vn_ref),
                 kbuf, vbuf, band_ref, kbias_ref, emit, dil=1, hw=hw, cs=cs, qb=qb, n_sub=n_sub,
                 n_q=n_q, sink=sink, unroll=unroll)


def _windowed_attention(pa, sink, *, seq, tile, hw, qb, unroll):
    n_cb, rows, _ = pa.shape
    bsz = rows // seq
    n_t = seq // tile
    n_q = B_Q_HEADS // B_KV_HEADS
    view = pa.reshape(n_cb, bsz, n_t, 1, tile, LANE)
    in_specs = ([_tile_specs(view.shape, CB_BQ, n_q, 1, tile, hw, n_t)[1]]
                + list(_tile_specs(view.shape, CB_BK, None, 1, tile, hw, n_t))
                + list(_tile_specs(view.shape, CB_BV, None, 1, tile, hw, n_t))
                + [pl.BlockSpec((n_q, 1, LANE), lambda b, hd, t: (hd, 0, 0))])
    out = pl.pallas_call(
        functools.partial(_windowed_attention_body, hw=hw, cs=tile, qb=qb, n_sub=seq, n_q=n_q,
                          unroll=unroll),
        grid=(bsz, B_KV_HEADS, n_t),
        in_specs=in_specs,
        out_specs=pl.BlockSpec((n_q, None, tile, LANE), lambda b, hd, t: (hd, b, t, 0)),
        out_shape=jax.ShapeDtypeStruct((B_Q_HEADS, bsz, seq, LANE), BF16),
        scratch_shapes=[pltpu.VMEM((1, tile + 2 * hw, LANE), BF16),
                        pltpu.VMEM((1, tile + 2 * hw, 2 * LANE), BF16),
                        pltpu.VMEM((qb, qb + 2 * hw), F32),
                        pltpu.VMEM((1, tile + 2 * hw), F32)],
        compiler_params=_cparams(("arbitrary", "arbitrary", "arbitrary")),
        name="windowed_attention",
    )(*([view] * 7 + [sink]))
    return out.reshape(B_Q_HEADS, rows, LANE)


def _dilated_attention_body(*refs, tile, hw, qb, seq, unroll):
    n_g = len(DIL_GROUPS)
    group_refs = [refs[7 * g:7 * g + 7] for g in range(n_g)]
    gate_ref, o_ref = refs[7 * n_g:7 * n_g + 2]
    scratch = refs[7 * n_g + 2:]
    bufs = [scratch[3 * g:3 * g + 3] for g in range(n_g)]
    band_ref, num_ref, max_ref, den_ref = scratch[3 * n_g:]
    band_ref[...] = _band_bias(qb, hw)

    for g, (_, dil) in enumerate(DIL_GROUPS):
        q_ref, kp_ref, kc_ref, kn_ref, vp_ref, vc_ref, vn_ref = group_refs[g]
        kbuf, vbuf, kbias_ref = bufs[g]

        def emit(rows, o2, m, g=g):
            num_ref[g, rows, :] = o2[0, :, :LANE]
            den_ref[g, rows, :] = o2[0, :, LANE:]
            max_ref[g, rows, :] = jnp.broadcast_to(m[0], (qb, LANE))

        _attend_tile(pl.program_id(2), q_ref, (kp_ref, kc_ref, kn_ref), (vp_ref, vc_ref, vn_ref),
                     kbuf, vbuf, band_ref, kbias_ref, emit, dil=dil, hw=hw, cs=tile // dil, qb=qb,
                     n_sub=seq // dil, n_q=1, sink=None, unroll=unroll)

    top = functools.reduce(jnp.maximum, [max_ref[g] for g in range(n_g)])
    weights = [jnp.exp2(max_ref[g] - top) for g in range(n_g)]
    num = sum(w * num_ref[g] for g, w in enumerate(weights))
    den = sum(w * den_ref[g] for g, w in enumerate(weights))
    o_ref[...] = (num / den * _silu(gate_ref[...].astype(F32))).astype(o_ref.dtype)


def _dilated_attention(pa, *, seq, tile, qb, unroll):
    n_cb, rows, _ = pa.shape
    bsz = rows // seq
    n_t = seq // tile
    hws = {window // (2 * dil) for window, dil in DIL_GROUPS}
    assert len(hws) == 1
    hw = hws.pop()
    in_specs, args, scratch = [], [], []
    for g, (_, dil) in enumerate(DIL_GROUPS):
        cs = tile // dil
        assert cs % qb == 0 and cs % hw == 0
        view = pa.reshape(n_cb, bsz, n_t, dil, cs, LANE)
        in_specs += ([_tile_specs(view.shape, CB_AQ + g * A_SLOTS, 1, dil, cs, hw, n_t)[1]]
                     + list(_tile_specs(view.shape, CB_AK + g * A_SLOTS, None, dil, cs, hw, n_t))
                     + list(_tile_specs(view.shape, CB_AV + g * A_SLOTS, None, dil, cs, hw, n_t)))
        args += [view] * 7
        scratch += [pltpu.VMEM((dil, cs + 2 * hw, LANE), BF16),
                    pltpu.VMEM((dil, cs + 2 * hw, 2 * LANE), BF16),
                    pltpu.VMEM((1, cs + 2 * hw), F32)]
    row_spec = pl.BlockSpec((None, None, tile, LANE), lambda b, hd, t: (hd, b, t, 0))
    in_specs.append(pl.BlockSpec((None, None, tile, LANE), lambda b, hd, t: (CB_AGATE + hd, b, t, 0)))
    args.append(pa.reshape(n_cb, bsz, seq, LANE))
    n_g = len(DIL_GROUPS)
    scratch += [pltpu.VMEM((qb, qb + 2 * hw), F32),
                pltpu.VMEM((n_g, tile, LANE), F32),
                pltpu.VMEM((n_g, tile, LANE), F32),
                pltpu.VMEM((n_g, tile, LANE), F32)]
    out = pl.pallas_call(
        functools.partial(_dilated_attention_body, tile=tile, hw=hw, qb=qb, seq=seq, unroll=unroll),
        grid=(bsz, A_SLOTS, n_t),
        in_specs=in_specs,
        out_specs=row_spec,
        out_shape=jax.ShapeDtypeStruct((A_SLOTS, bsz, seq, LANE), BF16),
        scratch_shapes=scratch,
        compiler_params=_cparams(("arbitrary", "arbitrary", "arbitrary")),
        name="dilated_attention",
    )(*args)
    return out.reshape(A_SLOTS, rows, LANE)


def _hgrn_body(*refs, layer, reverse, tb, finalize, hp):
    q_ref, f_ref, v_ref, lbp_ref = refs[:4]
    rest = list(refs[4:])
    if finalize:
        of_ref, gate_ref, gain_ref = rest[:3]
        rest = rest[3:]
    o_ref, st_ref, qs_ref, ks_ref, bs_ref, sc_ref, rest_ref = rest
    c = C_CHUNK
    n = tb // c

    @pl.when(pl.program_id(2) == 0)
    def _():
        st_ref[...] = jnp.zeros_like(st_ref)

    row = lax.broadcasted_iota(jnp.int32, (c, c), 0)
    col = lax.broadcasted_iota(jnp.int32, (c, c), 1)
    causal = (row <= col) if reverse else (row >= col)
    tri = jnp.where(causal, 1.0, 0.0).astype(BF16)

    def widen(a):
        return jnp.concatenate([a[i * c:(i + 1) * c] for i in range(n)], axis=1)

    def gates(h):
        fz = f_ref[h]
        e = jnp.exp2(jnp.abs(fz) * -LOG2_E)
        log_sig = jnp.minimum(fz, 0.0) * LOG2_E - jnp.log2(1.0 + e)
        sig_neg = jnp.where(fz >= 0.0, e, 1.0) / (1.0 + e)
        if layer == 0:
            log_f, kk = log_sig, sig_neg
        else:
            lbp = lbp_ref[:, h]
            e_l = jnp.exp(lbp - jnp.max(lbp, axis=0, keepdims=True))
            sm = e_l / jnp.sum(e_l, axis=0, keepdims=True)
            lb = sm[1]
            for i in range(2, layer + 1):
                lb = lb + sm[i]
            one_m_lb = 1.0 - lb
            tiny = jnp.float32(1e-37)
            log_lb = jnp.where(lb > 0.0, jnp.log2(jnp.maximum(lb, tiny)), -jnp.inf)
            log_1m_lb = jnp.where(one_m_lb > 0.0, jnp.log2(jnp.maximum(one_m_lb, tiny)), -jnp.inf)
            cand = log_1m_lb + log_sig
            log_f = (jnp.maximum(log_lb, cand)
                     + jnp.log2(1.0 + jnp.exp2(-jnp.abs(log_lb - cand))))
            kk = one_m_lb * sig_neg
        qh = _silu(q_ref[h].astype(F32))

        hi = log_f.astype(BF16)
        lo = (log_f - hi.astype(F32)).astype(BF16)
        b_wide = (jnp.dot(tri, widen(hi), preferred_element_type=F32)
                  + jnp.dot(tri, widen(lo), preferred_element_type=F32))
        b3 = jnp.stack([b_wide[:, i * LANE:(i + 1) * LANE] for i in range(n)])
        return qh.reshape(n, c, LANE), kk.reshape(n, c, LANE), b3

    heads = [gates(h) for h in range(hp)]
    mids = [b3 - b3[:, c // 2 - 1:c // 2] for _, _, b3 in heads]
    span = functools.reduce(jnp.maximum, [jnp.max(jnp.abs(d3)) for d3 in mids])

    def finish(h, o):
        o = o.reshape(tb, LANE)
        if finalize:
            o = _rms(o + of_ref[h], gain_ref[h]) * _silu(gate_ref[h].astype(F32))
        o_ref[h] = o.astype(o_ref.dtype)

    def intra(h, sc):
        scores = jnp.where(causal, sc, 0.0).astype(BF16)
        return jnp.einsum('nts,nsv->ntv', scores, v_ref[h].reshape(n, c, LANE),
                          preferred_element_type=F32)

    partial_out, updates = [], []
    for h, ((qh3, kk3, b3), d3) in enumerate(zip(heads, mids)):
        grow = jnp.exp2(jnp.clip(d3, -HGRN_FACTOR_RANGE, HGRN_FACTOR_RANGE))
        qt = (qh3 * grow).astype(BF16)
        kt = (kk3 / grow).astype(BF16)
        sc = jnp.einsum('ntk,nsk->nts', qt, kt, preferred_element_type=F32)
        v3 = v_ref[h].reshape(n, c, LANE)
        b_end = b3[:, 0:1] if reverse else b3[:, c - 1:c]
        k_in = (kk3 * jnp.exp2(b_end - b3)).astype(BF16)
        kv = jnp.einsum('nsv,nsk->nvk', v3, k_in, preferred_element_type=F32)
        partial_out.append(intra(h, sc))
        updates.append((kv, jnp.exp2(b_end)))
    incoming = []
    for h, (kv, decay_end) in enumerate(updates):
        st = st_ref[h]
        before = [None] * n
        for i in (range(n - 1, -1, -1) if reverse else range(n)):
            before[i] = st.astype(BF16)
            st = st * decay_end[i] + kv[i]
        st_ref[h] = st
        incoming.append(jnp.stack(before))
    for h, (qh3, kk3, b3) in enumerate(heads):
        inter = jnp.einsum('ntk,nvk->ntv', (qh3 * jnp.exp2(b3)).astype(BF16), incoming[h],
                           preferred_element_type=F32)
        rest_ref[h] = inter
        finish(h, partial_out[h] + inter)

    @pl.when(span > HGRN_FACTOR_RANGE)
    def _():
        for h, (qh3, kk3, b3) in enumerate(heads):
            qs_ref[h] = qh3
            ks_ref[h] = kk3
            bs_ref[h] = b3

        def per_chunk(i, carry):
            h, ci = i // n, i % n

            def pair(s, sc):
                rel = jnp.minimum(bs_ref[h, ci] - bs_ref[h, ci, pl.ds(s, 1), :], 0.0)
                w = qs_ref[h, ci] * ks_ref[h, ci, pl.ds(s, 1), :] * jnp.exp2(rel)
                return jnp.where(col == s, jnp.sum(w, axis=-1, keepdims=True), sc)

            sc_ref[h, ci] = lax.fori_loop(0, c, pair, jnp.zeros((c, c), F32))
            return carry

        lax.fori_loop(0, hp * n, per_chunk, 0)
        for h in range(hp):
            finish(h, intra(h, sc_ref[h]) + rest_ref[h])


def _hgrn_scan(pa, pf, pc, lower_bounds, *, seq, layer, reverse, tb, hp, o_fwd=None, gain=None):
    bsz = pa.shape[1] // seq
    tb = min(tb, seq)
    n_t = seq // tb
    n_chunks = tb // C_CHUNK
    finalize = o_fwd is not None
    assert C_HEADS % hp == 0

    def blk(cb):
        assert cb % hp == 0
        return pl.BlockSpec((hp, None, tb, LANE),
                            lambda b, hd, t: (cb // hp + hd, b, (n_t - 1 - t) if reverse else t, 0))

    def view(p):
        return p.reshape(p.shape[0], bsz, seq, LANE)

    f_cb = (CB_CFB if reverse else CB_CFF) - SEG_F[0]
    in_specs = [blk(CB_CQ - SEG_A[0]), blk(f_cb), blk(CB_CI - SEG_C[0]),
                pl.BlockSpec((DEPTH, hp, 1, LANE), lambda b, hd, t: (0, hd, 0, 0))]
    args = [view(pa), view(pf), view(pc), lower_bounds.reshape(DEPTH, C_HEADS, 1, LANE)]
    if finalize:
        in_specs += [blk(0), blk(CB_CGATE - SEG_C[0]),
                     pl.BlockSpec((hp, 1, LANE), lambda b, hd, t: (hd, 0, 0))]
        args += [o_fwd, view(pc), gain.reshape(C_HEADS, 1, LANE)]
    out_dtype = BF16 if finalize else F32
    return pl.pallas_call(
        functools.partial(_hgrn_body, layer=layer, reverse=reverse, tb=tb, finalize=finalize, hp=hp),
        grid=(bsz, C_HEADS // hp, n_t),
        in_specs=in_specs,
        out_specs=blk(0),
        out_shape=jax.ShapeDtypeStruct((C_HEADS, bsz, seq, LANE), out_dtype),
        scratch_shapes=[pltpu.VMEM((hp, LANE, LANE), F32),
                        pltpu.VMEM((hp, n_chunks, C_CHUNK, LANE), F32),
                        pltpu.VMEM((hp, n_chunks, C_CHUNK, LANE), F32),
                        pltpu.VMEM((hp, n_chunks, C_CHUNK, LANE), F32),
                        pltpu.VMEM((hp, n_chunks, C_CHUNK, C_CHUNK), F32),
                        pltpu.VMEM((hp, n_chunks, C_CHUNK, LANE), F32)],
        compiler_params=_cparams(("parallel", "parallel", "arbitrary")),
        name="hgrn_scan_bwd" if reverse else "hgrn_scan_fwd",
    )(*args)


def _silu(g):
    return g / (1.0 + jnp.exp(-g))


def _sigmoid(g):
    return 1.0 / (1.0 + jnp.exp(-g))


def _lane_concat(ref, n):
    return jnp.concatenate([ref[i] for i in range(n)], axis=-1)


def _merge_out_body(*refs, emit_next):
    ya_ref, ob_ref, bg0_ref, bg1_ref, yc_ref = refs[:5]
    m_refs = refs[5:17]
    x_ref, wa_ref, wb_ref, wc_ref, wo_ref, pg_ref = refs[17:23]
    if emit_next:
        ng_ref, out_ref, xn_ref = refs[23:]
    else:
        (out_ref,) = refs[23:]

    ya = _lane_concat(ya_ref, A_SLOTS)
    yb = []
    for hd in range(B_Q_HEADS):
        g_ref = bg0_ref if hd < 4 else bg1_ref
        yb.append((ob_ref[hd].astype(F32) * _silu(g_ref[hd % 4].astype(F32))).astype(BF16))
    yb = jnp.concatenate(yb, axis=-1)
    yc = _lane_concat(yc_ref, C_HEADS)

    proj = (jnp.dot(ya, wa_ref[...], preferred_element_type=F32),
            jnp.dot(yb, wb_ref[...], preferred_element_type=F32),
            jnp.dot(yc, wc_ref[...], preferred_element_type=F32))
    per_branch = D_MODEL // (TILE_CB * LANE)
    y = []
    for j in range(per_branch):
        cols = slice(j * TILE_CB * LANE, (j + 1) * TILE_CB * LANE)
        acc = None
        for br in range(N_BRANCHES):
            gate = _sigmoid(_lane_concat(m_refs[br * per_branch + j], TILE_CB).astype(F32))
            term = gate * proj[br][:, cols]
            acc = term if acc is None else acc + term
        y.append(acc.astype(BF16))
    y = jnp.concatenate(y, axis=-1)
    out = jnp.dot(y, wo_ref[...], preferred_element_type=F32)
    x_new = x_ref[...] + _rms(out, pg_ref[...])
    out_ref[...] = x_new
    if emit_next:
        xn_ref[...] = _rms(x_new, ng_ref[...]).astype(xn_ref.dtype)


def _merge_out(x2, pa, pc, ya, ob, yc, wa, wb, wc, wo, post_gain, next_gain, *, tm):
    rows, d = x2.shape
    emit_next = next_gain is not None

    def heads(n, first=0):
        return pl.BlockSpec((n, tm, LANE), lambda i: (first // n, i, 0))

    def const(shape):
        return pl.BlockSpec(shape, lambda i: (0,) * len(shape), pipeline_mode=pl.Buffered(1))

    n_merge = N_BRANCHES * D_MODEL // (TILE_CB * LANE)
    in_specs = ([heads(A_SLOTS), heads(B_Q_HEADS),
                 heads(4, CB_BGATE), heads(4, CB_BGATE + 4), heads(C_HEADS)]
                + [heads(TILE_CB, CB_MERGE - SEG_C[0] + TILE_CB * j) for j in range(n_merge)]
                + [pl.BlockSpec((tm, d), lambda i: (i, 0)),
                   const(wa.shape), const(wb.shape), const(wc.shape), const(wo.shape),
                   const((1, d))])
    args = ([ya, ob, pa, pa, yc] + [pc] * n_merge
            + [x2, wa, wb, wc, wo, post_gain.reshape(1, d)])
    row_spec = pl.BlockSpec((tm, d), lambda i: (i, 0))
    out_specs, out_shape = row_spec, jax.ShapeDtypeStruct((rows, d), F32)
    if emit_next:
        in_specs.append(const((1, d)))
        args.append(next_gain.reshape(1, d))
        out_specs = [row_spec, row_spec]
        out_shape = [out_shape, jax.ShapeDtypeStruct((rows, d), BF16)]
    return pl.pallas_call(
        functools.partial(_merge_out_body, emit_next=emit_next),
        grid=(rows // tm,),
        in_specs=in_specs,
        out_specs=out_specs,
        out_shape=out_shape,
        compiler_params=_cparams(("parallel",)),
        name="merge_out",
    )(*args)


def kernel(x, pre_norm, post_norm, w_in, sink_logits, hgrn_lower_bounds, hgrn_norm,
           w_branch_a, w_branch_b, w_branch_c, w_out):
    bsz, seq, d = x.shape
    rows = bsz * seq
    assert d == D_MODEL and w_in.shape[-1] == IN_COLS
    tile = min(PROJ_TILE_ROWS, seq)
    assert seq % tile == 0
    rope = _rope_tables(seq)
    plans = _projection_plans()
    x2 = x.reshape(rows, d)
    xn = _pre_norm(x2, pre_norm[0].astype(F32), tm=min(512, rows))
    for layer in range(DEPTH):
        w = w_in[layer].astype(BF16)
        pa = _in_projection(xn, w, rope, seq=seq, seg=SEG_A, out_dtype=BF16, plans=plans, tm=tile)
        pf = _in_projection(xn, w, None, seq=seq, seg=SEG_F, out_dtype=F32, plans=None, tm=tile)
        pc = _in_projection(xn, w, None, seq=seq, seg=SEG_C, out_dtype=BF16, plans=None, tm=tile)

        ya = _dilated_attention(pa, seq=seq, tile=tile, qb=128, unroll=8)
        sink = jnp.broadcast_to(sink_logits[layer].astype(F32)[:, None, None], (B_Q_HEADS, 1, LANE))
        ob = _windowed_attention(pa, sink, seq=seq, tile=tile, hw=B_HALF_WINDOW, qb=128, unroll=4)

        lower = hgrn_lower_bounds.astype(F32)
        o_fwd = _hgrn_scan(pa, pf, pc, lower, seq=seq, layer=layer, reverse=False, tb=512, hp=4)
        yc = _hgrn_scan(pa, pf, pc, lower, seq=seq, layer=layer, reverse=True, tb=512, hp=4,
                        o_fwd=o_fwd, gain=hgrn_norm[layer].astype(F32))
        yc = yc.reshape(C_HEADS, rows, LANE)

        next_gain = pre_norm[layer + 1].astype(F32) if layer + 1 < DEPTH else None
        res = _merge_out(x2, pa, pc, ya, ob, yc,
                         w_branch_a[layer].astype(BF16), w_branch_b[layer].astype(BF16),
                         w_branch_c[layer].astype(BF16), w_out[layer].astype(BF16),
                         post_norm[layer].astype(F32), next_gain, tm=min(256, rows))
        x2, xn = res if next_gain is not None else (res, None)
    return x2.reshape(bsz, seq, d)
```

```python
import functools

import jax
import jax.numpy as jnp
from jax import lax
from jax.experimental import pallas as pl
from jax.experimental.pallas import tpu as pltpu

F32 = jnp.float32
BF16 = jnp.bfloat16

LANE = 128
D_MODEL = 2048
DEPTH = 2
HEAD_DIM = 128
ROT_DIM = HEAD_DIM // 4
ROT_HALF = ROT_DIM // 2
ROPE_THETA = 500000.0
NORM_EPS = 1e-6

DIL_GROUPS = ((128, 1), (512, 4), (2048, 16))
A_SLOTS = 4
A_HEADS = A_SLOTS * len(DIL_GROUPS)
B_Q_HEADS = 8
B_KV_HEADS = 2
B_HALF_WINDOW = 128
C_HEADS = 8
C_CHUNK = 64
N_BRANCHES = 3

CB_AQ = 0
CB_AK = CB_AQ + A_HEADS
CB_AV = CB_AK + A_HEADS
CB_AGATE = CB_AV + A_HEADS
CB_BQ = CB_AGATE + A_SLOTS
CB_BK = CB_BQ + B_Q_HEADS
CB_BV = CB_BK + B_KV_HEADS
CB_BGATE = CB_BV + B_KV_HEADS
CB_CQ = CB_BGATE + B_Q_HEADS
CB_CFF = CB_CQ + C_HEADS
CB_CFB = CB_CFF + C_HEADS
CB_CI = CB_CFB + C_HEADS
CB_CGATE = CB_CI + C_HEADS
CB_MERGE = CB_CGATE + C_HEADS
N_CB = CB_MERGE + N_BRANCHES * D_MODEL // LANE
IN_COLS = N_CB * LANE
TILE_CB = 4

SEG_A, SEG_F, SEG_C = (0, CB_CFF), (CB_CFF, CB_CI), (CB_CI, N_CB)

PROJ_TILE_ROWS = 2048
NO_ROPE, ROPE_K, ROPE_Q = 0, 1, 2

MASK_VALUE = -1e30
LOG2_E = 1.4426950408889634
HGRN_FACTOR_RANGE = 115.0

VMEM_LIMIT = 56 * 1024 * 1024


def _cparams(sem):
    return pltpu.CompilerParams(dimension_semantics=sem, vmem_limit_bytes=VMEM_LIMIT)


def _projection_plans():
    codes = [NO_ROPE] * SEG_A[1]
    dils = [1] * SEG_A[1]
    for g, (_, dil) in enumerate(DIL_GROUPS):
        for s in range(A_SLOTS):
            codes[CB_AQ + g * A_SLOTS + s] = ROPE_Q
            codes[CB_AK + g * A_SLOTS + s] = ROPE_K
            for base in (CB_AQ, CB_AK, CB_AV):
                dils[base + g * A_SLOTS + s] = dil
    for h in range(B_Q_HEADS):
        codes[CB_BQ + h] = ROPE_Q
    for h in range(B_KV_HEADS):
        codes[CB_BK + h] = ROPE_K
    plans = []
    for t in range(SEG_A[1] // TILE_CB):
        sl = slice(t * TILE_CB, (t + 1) * TILE_CB)
        assert len(set(dils[sl])) == 1
        plans.append((tuple(codes[sl]), dils[sl][0]))
    return tuple(plans)


def _rms(x, gain):
    return x * lax.rsqrt(jnp.mean(x * x, axis=-1, keepdims=True) + NORM_EPS) * gain


def _pre_norm_body(x_ref, g_ref, o_ref):
    o_ref[...] = _rms(x_ref[...], g_ref[...]).astype(o_ref.dtype)


def _pre_norm(x2, gain, *, tm):
    rows, d = x2.shape
    return pl.pallas_call(
        _pre_norm_body,
        grid=(rows // tm,),
        in_specs=[pl.BlockSpec((tm, d), lambda i: (i, 0)), pl.BlockSpec((1, d), lambda i: (0, 0))],
        out_specs=pl.BlockSpec((tm, d), lambda i: (i, 0)),
        out_shape=jax.ShapeDtypeStruct((rows, d), BF16),
        compiler_params=_cparams(("parallel",)),
        name="pre_norm",
    )(x2, gain.reshape(1, d))


def _rope_tables(seq):
    pos = jnp.arange(seq, dtype=F32)
    inv_freq = ROPE_THETA ** (-jnp.arange(0, ROT_DIM, 2, dtype=F32) / ROT_DIM)
    ang = pos[:, None] * inv_freq[None, :]
    cos, sin = jnp.cos(ang), jnp.sin(ang)
    rest = HEAD_DIM - ROT_DIM
    c_full = jnp.concatenate([cos, cos, jnp.ones((seq, rest), F32)], axis=-1)
    s_full = jnp.concatenate([-sin, sin, jnp.zeros((seq, rest), F32)], axis=-1)
    return jnp.stack([c_full, s_full])


def _rotate_half_matrix(n_heads):
    width = n_heads * LANE
    src = lax.broadcasted_iota(jnp.int32, (width, width), 0)
    dst = lax.broadcasted_iota(jnp.int32, (width, width), 1)
    lane = dst % LANE
    hit = ((lane < ROT_HALF) & (src == dst + ROT_HALF)) | (
        (lane >= ROT_HALF) & (lane < ROT_DIM) & (src == dst - ROT_HALF))
    return jnp.where(hit, 1.0, 0.0).astype(BF16)


def _apply_rope(t, cos_full, sin_signed, swap):
    n_heads = t.shape[1] // LANE
    swapped = jnp.dot(t.astype(BF16), swap, preferred_element_type=F32)
    tile = lambda a: jnp.concatenate([a] * n_heads, axis=1) if n_heads > 1 else a
    return t * tile(cos_full) + swapped * tile(sin_signed)


def _in_projection_body(*refs, plans, tm):
    if plans is None:
        x_ref, w_ref, o_ref = refs
    else:
        x_ref, w_ref, rope_ref, o_ref, stage_ref = refs
    acc = jnp.dot(x_ref[...], w_ref[...], preferred_element_type=F32)
    for c in range(TILE_CB):
        o_ref[c] = acc[:, c * LANE:(c + 1) * LANE].astype(o_ref.dtype)
    if plans is None:
        return

    j = pl.program_id(1)
    groups = {}
    for tile, plan in enumerate(plans):
        if plan != ((NO_ROPE,) * TILE_CB, 1):
            groups.setdefault(plan, []).append(tile)
    scale = HEAD_DIM ** -0.5 * LOG2_E
    pair = 2
    swap = _rotate_half_matrix(pair)
    for (codes, dil), tiles in groups.items():
        cond = functools.reduce(jnp.logical_or, [j == t for t in tiles])

        @pl.when(cond)
        def _(codes=codes, dil=dil):
            for c0 in range(0, TILE_CB, pair):
                code = codes[c0]
                assert all(codes[c] == code for c in range(c0, c0 + pair))
                if code == NO_ROPE and dil == 1:
                    continue
                t = acc[:, c0 * LANE:(c0 + pair) * LANE]
                if code != NO_ROPE:
                    t = _apply_rope(t, rope_ref[0], rope_ref[1], swap)
                if code == ROPE_Q:
                    t = t * scale
                for c in range(c0, c0 + pair):
                    part = t[:, (c - c0) * LANE:(c - c0 + 1) * LANE]
                    if dil == 1:
                        o_ref[c] = part.astype(o_ref.dtype)
                    else:
                        stage_ref[c] = part
            if dil > 1:
                cs = tm // dil
                for c in range(TILE_CB):
                    def phase(p, carry, c=c):
                        dst = pl.ds(pl.multiple_of(p * cs, cs), cs)
                        src = stage_ref[c, pl.ds(p, cs, stride=dil), :]
                        o_ref[c, dst, :] = src.astype(o_ref.dtype)
                        return carry

                    lax.fori_loop(0, dil, phase, 0)


def _in_projection(xn, w_all, layer, rope, *, seq, seg, out_dtype, plans, tm):
    rows, d = xn.shape
    cb0, cb1 = seg
    n_tiles = (cb1 - cb0) // TILE_CB
    tn = TILE_CB * LANE
    tile0 = cb0 // TILE_CB
    in_specs = [pl.BlockSpec((tm, d), lambda i, j: (i, 0)),
                pl.BlockSpec((None, d, tn), lambda i, j: (layer, 0, tile0 + j))]
    args = [xn, w_all]
    scratch = []
    if plans is not None:
        per_seq = seq // tm
        in_specs.append(pl.BlockSpec((2, tm, LANE), lambda i, j: (0, i % per_seq, 0)))
        args.append(rope)
        scratch.append(pltpu.VMEM((TILE_CB, tm, LANE), F32))
    return pl.pallas_call(
        functools.partial(_in_projection_body, plans=plans, tm=tm),
        grid=(rows // tm, n_tiles),
        in_specs=in_specs,
        out_specs=pl.BlockSpec((TILE_CB, tm, LANE), lambda i, j: (j, i, 0)),
        out_shape=jax.ShapeDtypeStruct((cb1 - cb0, rows, LANE), out_dtype),
        scratch_shapes=scratch,
        compiler_params=_cparams(("parallel", "arbitrary")),
        name=f"in_projection_{cb0}",
    )(*args)


def _band_bias(qb, hw):
    qi = lax.broadcasted_iota(jnp.int32, (qb, qb + 2 * hw), 0)
    kj = lax.broadcasted_iota(jnp.int32, (qb, qb + 2 * hw), 1)
    return jnp.where(jnp.abs(kj - hw - qi) <= hw, 0.0, MASK_VALUE)


def _first_grid_step():
    first = pl.program_id(0) == 0
    for axis in (1, 2):
        first = first & (pl.program_id(axis) == 0)
    return first


def _attend_tile(t, q_ref, k_refs, v_refs, kbuf, vbuf, band_ref, kbias_ref, emit, *,
                 dil, hw, cs, qb, n_sub, n_q, sink, unroll):
    nk = qb + 2 * hw
    nks = cs + 2 * hw
    n_u = cs // qb
    kp_ref, kc_ref, kn_ref = k_refs
    vp_ref, vc_ref, vn_ref = v_refs
    kbuf[:, 0:hw] = kp_ref[...]
    kbuf[:, hw:hw + cs] = kc_ref[...]
    kbuf[:, hw + cs:] = kn_ref[...]
    vbuf[:, 0:hw, 0:LANE] = vp_ref[...]
    vbuf[:, hw:hw + cs, 0:LANE] = vc_ref[...]
    vbuf[:, hw + cs:, 0:LANE] = vn_ref[...]

    @pl.when(_first_grid_step())
    def _():
        vbuf[:, :, LANE:] = jnp.ones((dil, nks, LANE), BF16)

    kpos = t * cs - hw + lax.broadcasted_iota(jnp.int32, (1, nks), 1)
    kbias_ref[...] = jnp.where((kpos >= 0) & (kpos < n_sub), 0.0, MASK_VALUE)

    n_blocks = dil * n_u
    group_size = min(unroll, n_blocks)
    assert n_blocks % group_size == 0

    def group(i, carry):
        units = []
        for j in range(group_size):
            idx = i * group_size + j
            u = idx % n_u
            ph = idx // n_u
            units.append((u, ph, pl.multiple_of(u * qb, qb)))
        scores = []
        for u, ph, u0 in units:
            q = q_ref[:, ph, pl.ds(u0, qb), :].reshape(n_q * qb, LANE)
            s = lax.dot_general(q, kbuf[ph, pl.ds(u0, nk), :],
                                (((1,), (1,)), ((), ())), preferred_element_type=F32)
            scores.append(s.reshape(n_q, qb, nk) + band_ref[...] + kbias_ref[:, pl.ds(u0, nk)])
        probs, maxes = [], []
        for s in scores:
            m = jnp.max(s, axis=-1, keepdims=True)
            if sink is not None:
                m = jnp.maximum(m, sink)
            maxes.append(m)
            probs.append(jnp.exp2(s - m).astype(BF16).reshape(n_q * qb, nk))
        for (u, ph, u0), p, m in zip(units, probs, maxes):
            o2 = jnp.dot(p, vbuf[ph, pl.ds(u0, nk), :], preferred_element_type=F32)
            rows = pl.ds(u0, qb) if dil == 1 else pl.ds(ph + u * (qb * dil), qb, stride=dil)
            emit(rows, o2.reshape(n_q, qb, 2 * LANE), m)
        return carry

    lax.fori_loop(0, n_blocks // group_size, group, 0)


def _tile_specs(view_dims, cb, n, dil, cs, hw, n_t):
    per = cs // hw
    lead = cb if n is None else cb // n

    def spec(rows, tile_of, blk):
        return pl.BlockSpec((n, None, None, dil, rows, LANE),
                            lambda b, hd, t: (lead + hd, b, tile_of(t), 0, blk, 0))

    return (spec(hw, lambda t: jnp.maximum(t - 1, 0), per - 1),
            spec(cs, lambda t: t, 0),
            spec(hw, lambda t: jnp.minimum(t + 1, n_t - 1), 0))


def _windowed_attention_body(q_ref, kp_ref, kc_ref, kn_ref, vp_ref, vc_ref, vn_ref, sink_ref, o_ref,
                             kbuf, vbuf, band_ref, kbias_ref, *, hw, cs, qb, n_sub, n_q, unroll):
    band_ref[...] = _band_bias(qb, hw)
    sink = sink_ref[...][:, :, :1] * LOG2_E

    def emit(rows, o2, m):
        denom = o2[:, :, LANE:] + jnp.exp2(sink - m)
        o_ref[:, rows, :] = (o2[:, :, :LANE] / denom).astype(o_ref.dtype)

    _attend_tile(pl.program_id(2), q_ref, (kp_ref, kc_ref, kn_ref), (vp_ref, vc_ref, vn_ref),
                 kbuf, vbuf, band_ref, kbias_ref, emit, dil=1, hw=hw, cs=cs, qb=qb, n_sub=n_sub,
                 n_q=n_q, sink=sink, unroll=unroll)


def _windowed_attention(pa, sink, *, seq, tile, hw, qb, unroll):
    n_cb, rows, _ = pa.shape
    bsz = rows // seq
    n_t = seq // tile
    n_q = B_Q_HEADS // B_KV_HEADS
    view = pa.reshape(n_cb, bsz, n_t, 1, tile, LANE)
    in_specs = ([_tile_specs(view.shape, CB_BQ, n_q, 1, tile, hw, n_t)[1]]
                + list(_tile_specs(view.shape, CB_BK, None, 1, tile, hw, n_t))
                + list(_tile_specs(view.shape, CB_BV, None, 1, tile, hw, n_t))
                + [pl.BlockSpec((n_q, 1, LANE), lambda b, hd, t: (hd, 0, 0))])
    out = pl.pallas_call(
        functools.partial(_windowed_attention_body, hw=hw, cs=tile, qb=qb, n_sub=seq, n_q=n_q,
                          unroll=unroll),
        grid=(bsz, B_KV_HEADS, n_t),
        in_specs=in_specs,
        out_specs=pl.BlockSpec((n_q, None, tile, LANE), lambda b, hd, t: (hd, b, t, 0)),
        out_shape=jax.ShapeDtypeStruct((B_Q_HEADS, bsz, seq, LANE), BF16),
        scratch_shapes=[pltpu.VMEM((1, tile + 2 * hw, LANE), BF16),
                        pltpu.VMEM((1, tile + 2 * hw, 2 * LANE), BF16),
                        pltpu.VMEM((qb, qb + 2 * hw), F32),
                        pltpu.VMEM((1, tile + 2 * hw), F32)],
        compiler_params=_cparams(("arbitrary", "arbitrary", "arbitrary")),
        name="windowed_attention",
    )(*([view] * 7 + [sink]))
    return out.reshape(B_Q_HEADS, rows, LANE)


def _dilated_attention_body(*refs, tile, hw, qb, seq, unroll):
    n_g = len(DIL_GROUPS)
    group_refs = [refs[7 * g:7 * g + 7] for g in range(n_g)]
    gate_ref, o_ref = refs[7 * n_g:7 * n_g + 2]
    scratch = refs[7 * n_g + 2:]
    bufs = [scratch[3 * g:3 * g + 3] for g in range(n_g)]
    band_ref, num_ref, max_ref, den_ref = scratch[3 * n_g:]
    band_ref[...] = _band_bias(qb, hw)

    for g, (_, dil) in enumerate(DIL_GROUPS):
        q_ref, kp_ref, kc_ref, kn_ref, vp_ref, vc_ref, vn_ref = group_refs[g]
        kbuf, vbuf, kbias_ref = bufs[g]

        def emit(rows, o2, m, g=g):
            num_ref[g, rows, :] = o2[0, :, :LANE]
            den_ref[g, rows, :] = o2[0, :, LANE:]
            max_ref[g, rows, :] = jnp.broadcast_to(m[0], (qb, LANE))

        _attend_tile(pl.program_id(2), q_ref, (kp_ref, kc_ref, kn_ref), (vp_ref, vc_ref, vn_ref),
                     kbuf, vbuf, band_ref, kbias_ref, emit, dil=dil, hw=hw, cs=tile // dil, qb=qb,
                     n_sub=seq // dil, n_q=1, sink=None, unroll=unroll)

    top = functools.reduce(jnp.maximum, [max_ref[g] for g in range(n_g)])
    weights = [jnp.exp2(max_ref[g] - top) for g in range(n_g)]
    num = sum(w * num_ref[g] for g, w in enumerate(weights))
    den = sum(w * den_ref[g] for g, w in enumerate(weights))
    o_ref[...] = (num / den * _silu(gate_ref[...].astype(F32))).astype(o_ref.dtype)


def _dilated_attention(pa, *, seq, tile, qb, unroll):
    n_cb, rows, _ = pa.shape
    bsz = rows // seq
    n_t = seq // tile
    hws = {window // (2 * dil) for window, dil in DIL_GROUPS}
    assert len(hws) == 1
    hw = hws.pop()
    in_specs, args, scratch = [], [], []
    for g, (_, dil) in enumerate(DIL_GROUPS):
        cs = tile // dil
        assert cs % qb == 0 and cs % hw == 0
        view = pa.reshape(n_cb, bsz, n_t, dil, cs, LANE)
        in_specs += ([_tile_specs(view.shape, CB_AQ + g * A_SLOTS, 1, dil, cs, hw, n_t)[1]]
                     + list(_tile_specs(view.shape, CB_AK + g * A_SLOTS, None, dil, cs, hw, n_t))
                     + list(_tile_specs(view.shape, CB_AV + g * A_SLOTS, None, dil, cs, hw, n_t)))
        args += [view] * 7
        scratch += [pltpu.VMEM((dil, cs + 2 * hw, LANE), BF16),
                    pltpu.VMEM((dil, cs + 2 * hw, 2 * LANE), BF16),
                    pltpu.VMEM((1, cs + 2 * hw), F32)]
    row_spec = pl.BlockSpec((None, None, tile, LANE), lambda b, hd, t: (hd, b, t, 0))
    in_specs.append(pl.BlockSpec((None, None, tile, LANE), lambda b, hd, t: (CB_AGATE + hd, b, t, 0)))
    args.append(pa.reshape(n_cb, bsz, seq, LANE))
    n_g = len(DIL_GROUPS)
    scratch += [pltpu.VMEM((qb, qb + 2 * hw), F32),
                pltpu.VMEM((n_g, tile, LANE), F32),
                pltpu.VMEM((n_g, tile, LANE), F32),
                pltpu.VMEM((n_g, tile, LANE), F32)]
    out = pl.pallas_call(
        functools.partial(_dilated_attention_body, tile=tile, hw=hw, qb=qb, seq=seq, unroll=unroll),
        grid=(bsz, A_SLOTS, n_t),
        in_specs=in_specs,
        out_specs=row_spec,
        out_shape=jax.ShapeDtypeStruct((A_SLOTS, bsz, seq, LANE), BF16),
        scratch_shapes=scratch,
        compiler_params=_cparams(("arbitrary", "arbitrary", "arbitrary")),
        name="dilated_attention",
    )(*args)
    return out.reshape(A_SLOTS, rows, LANE)


def _hgrn_body(*refs, layer, reverse, tb, finalize, hp):
    q_ref, f_ref, v_ref, lbp_ref = refs[:4]
    rest = list(refs[4:])
    if finalize:
        of_ref, gate_ref, gain_ref = rest[:3]
        rest = rest[3:]
    o_ref, st_ref, qs_ref, ks_ref, bs_ref, sc_ref, rest_ref = rest
    c = C_CHUNK
    n = tb // c

    @pl.when(pl.program_id(2) == 0)
    def _():
        st_ref[...] = jnp.zeros_like(st_ref)

    row = lax.broadcasted_iota(jnp.int32, (c, c), 0)
    col = lax.broadcasted_iota(jnp.int32, (c, c), 1)
    causal = (row <= col) if reverse else (row >= col)
    tri = jnp.where(causal, 1.0, 0.0).astype(BF16)

    def widen(a):
        return jnp.concatenate([a[i * c:(i + 1) * c] for i in range(n)], axis=1)

    def gates(h):
        fz = f_ref[h]
        e = jnp.exp2(jnp.abs(fz) * -LOG2_E)
        log_sig = jnp.minimum(fz, 0.0) * LOG2_E - jnp.log2(1.0 + e)
        sig_neg = jnp.where(fz >= 0.0, e, 1.0) / (1.0 + e)
        if layer == 0:
            log_f, kk = log_sig, sig_neg
        else:
            lbp = lbp_ref[:, h]
            e_l = jnp.exp(lbp - jnp.max(lbp, axis=0, keepdims=True))
            sm = e_l / jnp.sum(e_l, axis=0, keepdims=True)
            lb = sm[1]
            for i in range(2, layer + 1):
                lb = lb + sm[i]
            one_m_lb = 1.0 - lb
            tiny = jnp.float32(1e-37)
            log_lb = jnp.where(lb > 0.0, jnp.log2(jnp.maximum(lb, tiny)), -jnp.inf)
            log_1m_lb = jnp.where(one_m_lb > 0.0, jnp.log2(jnp.maximum(one_m_lb, tiny)), -jnp.inf)
            cand = log_1m_lb + log_sig
            log_f = (jnp.maximum(log_lb, cand)
                     + jnp.log2(1.0 + jnp.exp2(-jnp.abs(log_lb - cand))))
            kk = one_m_lb * sig_neg
        qh = _silu(q_ref[h].astype(F32))

        hi = log_f.astype(BF16)
        lo = (log_f - hi.astype(F32)).astype(BF16)
        b_wide = (jnp.dot(tri, widen(hi), preferred_element_type=F32)
                  + jnp.dot(tri, widen(lo), preferred_element_type=F32))
        b3 = jnp.stack([b_wide[:, i * LANE:(i + 1) * LANE] for i in range(n)])
        return qh.reshape(n, c, LANE), kk.reshape(n, c, LANE), b3

    heads = [gates(h) for h in range(hp)]
    mids = [b3 - b3[:, c // 2 - 1:c // 2] for _, _, b3 in heads]
    span = functools.reduce(jnp.maximum, [jnp.max(jnp.abs(d3)) for d3 in mids])

    def finish(h, o):
        o = o.reshape(tb, LANE)
        if finalize:
            o = _rms(o + of_ref[h], gain_ref[h]) * _silu(gate_ref[h].astype(F32))
        o_ref[h] = o.astype(o_ref.dtype)

    def intra(h, sc):
        scores = jnp.where(causal, sc, 0.0).astype(BF16)
        return jnp.einsum('nts,nsv->ntv', scores, v_ref[h].reshape(n, c, LANE),
                          preferred_element_type=F32)

    partial_out, updates = [], []
    for h, ((qh3, kk3, b3), d3) in enumerate(zip(heads, mids)):
        grow = jnp.exp2(jnp.clip(d3, -HGRN_FACTOR_RANGE, HGRN_FACTOR_RANGE))
        qt = (qh3 * grow).astype(BF16)
        kt = (kk3 / grow).astype(BF16)
        sc = jnp.einsum('ntk,nsk->nts', qt, kt, preferred_element_type=F32)
        v3 = v_ref[h].reshape(n, c, LANE)
        b_end = b3[:, 0:1] if reverse else b3[:, c - 1:c]
        k_in = (kk3 * jnp.exp2(b_end - b3)).astype(BF16)
        kv = jnp.einsum('nsv,nsk->nvk', v3, k_in, preferred_element_type=F32)
        partial_out.append(intra(h, sc))
        updates.append((kv, jnp.exp2(b_end)))
    incoming = []
    for h, (kv, decay_end) in enumerate(updates):
        st = st_ref[h]
        before = [None] * n
        for i in (range(n - 1, -1, -1) if reverse else range(n)):
            before[i] = st.astype(BF16)
            st = st * decay_end[i] + kv[i]
        st_ref[h] = st
        incoming.append(jnp.stack(before))
    for h, (qh3, kk3, b3) in enumerate(heads):
        inter = jnp.einsum('ntk,nvk->ntv', (qh3 * jnp.exp2(b3)).astype(BF16), incoming[h],
                           preferred_element_type=F32)
        rest_ref[h] = inter
        finish(h, partial_out[h] + inter)

    @pl.when(span > HGRN_FACTOR_RANGE)
    def _():
        for h, (qh3, kk3, b3) in enumerate(heads):
            qs_ref[h] = qh3
            ks_ref[h] = kk3
            bs_ref[h] = b3

        def per_chunk(i, carry):
            h, ci = i // n, i % n

            def pair(s, sc):
                rel = jnp.minimum(bs_ref[h, ci] - bs_ref[h, ci, pl.ds(s, 1), :], 0.0)
                w = qs_ref[h, ci] * ks_ref[h, ci, pl.ds(s, 1), :] * jnp.exp2(rel)
                return jnp.where(col == s, jnp.sum(w, axis=-1, keepdims=True), sc)

            sc_ref[h, ci] = lax.fori_loop(0, c, pair, jnp.zeros((c, c), F32))
            return carry

        lax.fori_loop(0, hp * n, per_chunk, 0)
        for h in range(hp):
            finish(h, intra(h, sc_ref[h]) + rest_ref[h])


def _hgrn_scan(pa, pf, pc, lower_bounds, *, seq, layer, reverse, tb, hp, o_fwd=None, gain=None):
    bsz = pa.shape[1] // seq
    tb = min(tb, seq)
    n_t = seq // tb
    n_chunks = tb // C_CHUNK
    finalize = o_fwd is not None
    assert C_HEADS % hp == 0

    def blk(cb):
        assert cb % hp == 0
        return pl.BlockSpec((hp, None, tb, LANE),
                            lambda b, hd, t: (cb // hp + hd, b, (n_t - 1 - t) if reverse else t, 0))

    def view(p):
        return p.reshape(p.shape[0], bsz, seq, LANE)

    f_cb = (CB_CFB if reverse else CB_CFF) - SEG_F[0]
    in_specs = [blk(CB_CQ - SEG_A[0]), blk(f_cb), blk(CB_CI - SEG_C[0]),
                pl.BlockSpec((DEPTH, hp, 1, LANE), lambda b, hd, t: (0, hd, 0, 0))]
    args = [view(pa), view(pf), view(pc), lower_bounds.reshape(DEPTH, C_HEADS, 1, LANE)]
    if finalize:
        in_specs += [blk(0), blk(CB_CGATE - SEG_C[0]),
                     pl.BlockSpec((hp, 1, LANE), lambda b, hd, t: (hd, 0, 0))]
        args += [o_fwd, view(pc), gain.reshape(C_HEADS, 1, LANE)]
    out_dtype = BF16 if finalize else F32
    return pl.pallas_call(
        functools.partial(_hgrn_body, layer=layer, reverse=reverse, tb=tb, finalize=finalize, hp=hp),
        grid=(bsz, C_HEADS // hp, n_t),
        in_specs=in_specs,
        out_specs=blk(0),
        out_shape=jax.ShapeDtypeStruct((C_HEADS, bsz, seq, LANE), out_dtype),
        scratch_shapes=[pltpu.VMEM((hp, LANE, LANE), F32),
                        pltpu.VMEM((hp, n_chunks, C_CHUNK, LANE), F32),
                        pltpu.VMEM((hp, n_chunks, C_CHUNK, LANE), F32),
                        pltpu.VMEM((hp, n_chunks, C_CHUNK, LANE), F32),
                        pltpu.VMEM((hp, n_chunks, C_CHUNK, C_CHUNK), F32),
                        pltpu.VMEM((hp, n_chunks, C_CHUNK, LANE), F32)],
        compiler_params=_cparams(("parallel", "parallel", "arbitrary")),
        name="hgrn_scan_bwd" if reverse else "hgrn_scan_fwd",
    )(*args)


def _silu(g):
    return g / (1.0 + jnp.exp(-g))


def _sigmoid(g):
    return 1.0 / (1.0 + jnp.exp(-g))


def _lane_concat(ref, n):
    return jnp.concatenate([ref[i] for i in range(n)], axis=-1)


def _merge_out_body(*refs, emit_next):
    ya_ref, ob_ref, bg0_ref, bg1_ref, yc_ref = refs[:5]
    m_refs = refs[5:17]
    x_ref, wa_ref, wb_ref, wc_ref, wo_ref, pg_ref = refs[17:23]
    if emit_next:
        ng_ref, out_ref, xn_ref = refs[23:]
    else:
        (out_ref,) = refs[23:]

    ya = _lane_concat(ya_ref, A_SLOTS)
    yb = []
    for hd in range(B_Q_HEADS):
        g_ref = bg0_ref if hd < 4 else bg1_ref
        yb.append((ob_ref[hd].astype(F32) * _silu(g_ref[hd % 4].astype(F32))).astype(BF16))
    yb = jnp.concatenate(yb, axis=-1)
    yc = _lane_concat(yc_ref, C_HEADS)

    proj = (jnp.dot(ya, wa_ref[...], preferred_element_type=F32),
            jnp.dot(yb, wb_ref[...], preferred_element_type=F32),
            jnp.dot(yc, wc_ref[...], preferred_element_type=F32))
    per_branch = D_MODEL // (TILE_CB * LANE)
    y = []
    for j in range(per_branch):
        cols = slice(j * TILE_CB * LANE, (j + 1) * TILE_CB * LANE)
        acc = None
        for br in range(N_BRANCHES):
            gate = _sigmoid(_lane_concat(m_refs[br * per_branch + j], TILE_CB).astype(F32))
            term = gate * proj[br][:, cols]
            acc = term if acc is None else acc + term
        y.append(acc.astype(BF16))
    y = jnp.concatenate(y, axis=-1)
    out = jnp.dot(y, wo_ref[...], preferred_element_type=F32)
    x_new = x_ref[...] + _rms(out, pg_ref[...])
    out_ref[...] = x_new
    if emit_next:
        xn_ref[...] = _rms(x_new, ng_ref[...]).astype(xn_ref.dtype)


def _merge_out(x2, pa, pc, ya, ob, yc, wa, wb, wc, wo, layer, post_gain, next_gain, *, tm):
    rows, d = x2.shape
    emit_next = next_gain is not None

    def heads(n, first=0):
        return pl.BlockSpec((n, tm, LANE), lambda i: (first // n, i, 0))

    def const(shape):
        return pl.BlockSpec(shape, lambda i: (0,) * len(shape), pipeline_mode=pl.Buffered(1))

    def weight(w):
        return pl.BlockSpec((None,) + w.shape[1:], lambda i: (layer, 0, 0),
                            pipeline_mode=pl.Buffered(1))

    n_merge = N_BRANCHES * D_MODEL // (TILE_CB * LANE)
    in_specs = ([heads(A_SLOTS), heads(B_Q_HEADS),
                 heads(4, CB_BGATE), heads(4, CB_BGATE + 4), heads(C_HEADS)]
                + [heads(TILE_CB, CB_MERGE - SEG_C[0] + TILE_CB * j) for j in range(n_merge)]
                + [pl.BlockSpec((tm, d), lambda i: (i, 0)),
                   weight(wa), weight(wb), weight(wc), weight(wo),
                   const((1, d))])
    args = ([ya, ob, pa, pa, yc] + [pc] * n_merge
            + [x2, wa, wb, wc, wo, post_gain.reshape(1, d)])
    row_spec = pl.BlockSpec((tm, d), lambda i: (i, 0))
    out_specs, out_shape = row_spec, jax.ShapeDtypeStruct((rows, d), F32)
    if emit_next:
        in_specs.append(const((1, d)))
        args.append(next_gain.reshape(1, d))
        out_specs = [row_spec, row_spec]
        out_shape = [out_shape, jax.ShapeDtypeStruct((rows, d), BF16)]
    return pl.pallas_call(
        functools.partial(_merge_out_body, emit_next=emit_next),
        grid=(rows // tm,),
        in_specs=in_specs,
        out_specs=out_specs,
        out_shape=out_shape,
        compiler_params=_cparams(("parallel",)),
        name="merge_out",
    )(*args)


def kernel(x, pre_norm, post_norm, w_in, sink_logits, hgrn_lower_bounds, hgrn_norm,
           w_branch_a, w_branch_b, w_branch_c, w_out):
    bsz, seq, d = x.shape
    rows = bsz * seq
    assert d == D_MODEL and w_in.shape[-1] == IN_COLS
    tile = min(PROJ_TILE_ROWS, seq)
    assert seq % tile == 0
    rope = _rope_tables(seq)
    plans = _projection_plans()
    x2 = x.reshape(rows, d)
    xn = _pre_norm(x2, pre_norm[0].astype(F32), tm=min(512, rows))
    w, wa, wb, wc, wo = (t.astype(BF16) for t in (w_in, w_branch_a, w_branch_b, w_branch_c, w_out))
    for layer in range(DEPTH):
        pa = _in_projection(xn, w, layer, rope, seq=seq, seg=SEG_A, out_dtype=BF16, plans=plans,
                            tm=tile)
        pf = _in_projection(xn, w, layer, None, seq=seq, seg=SEG_F, out_dtype=F32, plans=None,
                            tm=tile)
        pc = _in_projection(xn, w, layer, None, seq=seq, seg=SEG_C, out_dtype=BF16, plans=None,
                            tm=tile)

        ya = _dilated_attention(pa, seq=seq, tile=tile, qb=128, unroll=16)
        sink = jnp.broadcast_to(sink_logits[layer].astype(F32)[:, None, None], (B_Q_HEADS, 1, LANE))
        ob = _windowed_attention(pa, sink, seq=seq, tile=tile, hw=B_HALF_WINDOW, qb=128, unroll=4)

        lower = hgrn_lower_bounds.astype(F32)
        o_fwd = _hgrn_scan(pa, pf, pc, lower, seq=seq, layer=layer, reverse=False, tb=1024, hp=4)
        yc = _hgrn_scan(pa, pf, pc, lower, seq=seq, layer=layer, reverse=True, tb=1024, hp=4,
                        o_fwd=o_fwd, gain=hgrn_norm[layer].astype(F32))
        yc = yc.reshape(C_HEADS, rows, LANE)

        next_gain = pre_norm[layer + 1].astype(F32) if layer + 1 < DEPTH else None
        res = _merge_out(x2, pa, pc, ya, ob, yc, wa, wb, wc, wo, layer,
                         post_norm[layer].astype(F32), next_gain, tm=min(256, rows))
        x2, xn = res if next_gain is not None else (res, None)
    return x2.reshape(bsz, seq, d)
```

```python
import functools

import jax
import jax.numpy as jnp
from jax import lax
from jax.experimental import pallas as pl
from jax.experimental.pallas import tpu as pltpu

F32 = jnp.float32
BF16 = jnp.bfloat16

LANE = 128
D_MODEL = 2048
DEPTH = 2
HEAD_DIM = 128
ROT_DIM = HEAD_DIM // 4
ROT_HALF = ROT_DIM // 2
ROPE_THETA = 500000.0
NORM_EPS = 1e-6

DIL_GROUPS = ((128, 1), (512, 4), (2048, 16))
A_SLOTS = 4
A_HEADS = A_SLOTS * len(DIL_GROUPS)
B_Q_HEADS = 8
B_KV_HEADS = 2
B_HALF_WINDOW = 128
C_HEADS = 8
C_CHUNK = 64
N_BRANCHES = 3

CB_AQ = 0
CB_AK = CB_AQ + A_HEADS
CB_AV = CB_AK + A_HEADS
CB_AGATE = CB_AV + A_HEADS
CB_BQ = CB_AGATE + A_SLOTS
CB_BK = CB_BQ + B_Q_HEADS
CB_BV = CB_BK + B_KV_HEADS
CB_BGATE = CB_BV + B_KV_HEADS
CB_CQ = CB_BGATE + B_Q_HEADS
CB_CFF = CB_CQ + C_HEADS
CB_CFB = CB_CFF + C_HEADS
CB_CI = CB_CFB + C_HEADS
CB_CGATE = CB_CI + C_HEADS
CB_MERGE = CB_CGATE + C_HEADS
N_CB = CB_MERGE + N_BRANCHES * D_MODEL // LANE
IN_COLS = N_CB * LANE
TILE_CB = 4

SEG_A, SEG_F, SEG_C = (0, CB_CFF), (CB_CFF, CB_CI), (CB_CI, N_CB)

PROJ_TILE_ROWS = 2048
NO_ROPE, ROPE_K, ROPE_Q = 0, 1, 2

MASK_VALUE = -1e30
LOG2_E = 1.4426950408889634
HGRN_FACTOR_RANGE = 115.0

VMEM_LIMIT = 56 * 1024 * 1024


def _cparams(sem):
    return pltpu.CompilerParams(dimension_semantics=sem, vmem_limit_bytes=VMEM_LIMIT)


def _projection_plans():
    codes = [NO_ROPE] * SEG_A[1]
    dils = [1] * SEG_A[1]
    for g, (_, dil) in enumerate(DIL_GROUPS):
        for s in range(A_SLOTS):
            codes[CB_AQ + g * A_SLOTS + s] = ROPE_Q
            codes[CB_AK + g * A_SLOTS + s] = ROPE_K
            for base in (CB_AQ, CB_AK, CB_AV):
                dils[base + g * A_SLOTS + s] = dil
    for h in range(B_Q_HEADS):
        codes[CB_BQ + h] = ROPE_Q
    for h in range(B_KV_HEADS):
        codes[CB_BK + h] = ROPE_K
    plans = []
    for t in range(SEG_A[1] // TILE_CB):
        sl = slice(t * TILE_CB, (t + 1) * TILE_CB)
        assert len(set(dils[sl])) == 1
        plans.append((tuple(codes[sl]), dils[sl][0]))
    return tuple(plans)


def _rms(x, gain):
    return x * lax.rsqrt(jnp.mean(x * x, axis=-1, keepdims=True) + NORM_EPS) * gain


def _pre_norm_body(x_ref, g_ref, o_ref):
    o_ref[...] = _rms(x_ref[...], g_ref[...]).astype(o_ref.dtype)


def _pre_norm(x2, gain, *, tm):
    rows, d = x2.shape
    return pl.pallas_call(
        _pre_norm_body,
        grid=(rows // tm,),
        in_specs=[pl.BlockSpec((tm, d), lambda i: (i, 0)), pl.BlockSpec((1, d), lambda i: (0, 0))],
        out_specs=pl.BlockSpec((tm, d), lambda i: (i, 0)),
        out_shape=jax.ShapeDtypeStruct((rows, d), BF16),
        compiler_params=_cparams(("parallel",)),
        name="pre_norm",
    )(x2, gain.reshape(1, d))


def _rope_tables(seq):
    pos = jnp.arange(seq, dtype=F32)
    inv_freq = ROPE_THETA ** (-jnp.arange(0, ROT_DIM, 2, dtype=F32) / ROT_DIM)
    ang = pos[:, None] * inv_freq[None, :]
    cos, sin = jnp.cos(ang), jnp.sin(ang)
    rest = HEAD_DIM - ROT_DIM
    c_full = jnp.concatenate([cos, cos, jnp.ones((seq, rest), F32)], axis=-1)
    s_full = jnp.concatenate([-sin, sin, jnp.zeros((seq, rest), F32)], axis=-1)
    return jnp.stack([c_full, s_full])


def _rotate_half_matrix(n_heads):
    width = n_heads * LANE
    src = lax.broadcasted_iota(jnp.int32, (width, width), 0)
    dst = lax.broadcasted_iota(jnp.int32, (width, width), 1)
    lane = dst % LANE
    hit = ((lane < ROT_HALF) & (src == dst + ROT_HALF)) | (
        (lane >= ROT_HALF) & (lane < ROT_DIM) & (src == dst - ROT_HALF))
    return jnp.where(hit, 1.0, 0.0).astype(BF16)


def _apply_rope(t, cos_full, sin_signed, swap):
    n_heads = t.shape[1] // LANE
    swapped = jnp.dot(t.astype(BF16), swap, preferred_element_type=F32)
    tile = lambda a: jnp.concatenate([a] * n_heads, axis=1) if n_heads > 1 else a
    return t * tile(cos_full) + swapped * tile(sin_signed)


def _in_projection_body(*refs, plans, tm):
    if plans is None:
        x_ref, w_ref, o_ref = refs
    else:
        x_ref, w_ref, rope_ref, o_ref, stage_ref = refs
    acc = jnp.dot(x_ref[...], w_ref[...], preferred_element_type=F32)
    for c in range(TILE_CB):
        o_ref[c] = acc[:, c * LANE:(c + 1) * LANE].astype(o_ref.dtype)
    if plans is None:
        return

    j = pl.program_id(1)
    groups = {}
    for tile, plan in enumerate(plans):
        if plan != ((NO_ROPE,) * TILE_CB, 1):
            groups.setdefault(plan, []).append(tile)
    scale = HEAD_DIM ** -0.5 * LOG2_E
    pair = 2
    swap = _rotate_half_matrix(pair)
    for (codes, dil), tiles in groups.items():
        cond = functools.reduce(jnp.logical_or, [j == t for t in tiles])

        @pl.when(cond)
        def _(codes=codes, dil=dil):
            for c0 in range(0, TILE_CB, pair):
                code = codes[c0]
                assert all(codes[c] == code for c in range(c0, c0 + pair))
                if code == NO_ROPE and dil == 1:
                    continue
                t = acc[:, c0 * LANE:(c0 + pair) * LANE]
                if code != NO_ROPE:
                    t = _apply_rope(t, rope_ref[0], rope_ref[1], swap)
                if code == ROPE_Q:
                    t = t * scale
                for c in range(c0, c0 + pair):
                    part = t[:, (c - c0) * LANE:(c - c0 + 1) * LANE]
                    if dil == 1:
                        o_ref[c] = part.astype(o_ref.dtype)
                    else:
                        stage_ref[c] = part
            if dil > 1:
                cs = tm // dil
                for c in range(TILE_CB):
                    def phase(p, carry, c=c):
                        dst = pl.ds(pl.multiple_of(p * cs, cs), cs)
                        src = stage_ref[c, pl.ds(p, cs, stride=dil), :]
                        o_ref[c, dst, :] = src.astype(o_ref.dtype)
                        return carry

                    lax.fori_loop(0, dil, phase, 0)


def _in_projection(xn, w_all, layer, rope, *, seq, seg, out_dtype, plans, tm):
    rows, d = xn.shape
    cb0, cb1 = seg
    n_tiles = (cb1 - cb0) // TILE_CB
    tn = TILE_CB * LANE
    tile0 = cb0 // TILE_CB
    in_specs = [pl.BlockSpec((tm, d), lambda i, j: (i, 0)),
                pl.BlockSpec((None, d, tn), lambda i, j: (layer, 0, tile0 + j))]
    args = [xn, w_all]
    scratch = []
    if plans is not None:
        per_seq = seq // tm
        in_specs.append(pl.BlockSpec((2, tm, LANE), lambda i, j: (0, i % per_seq, 0)))
        args.append(rope)
        scratch.append(pltpu.VMEM((TILE_CB, tm, LANE), F32))
    return pl.pallas_call(
        functools.partial(_in_projection_body, plans=plans, tm=tm),
        grid=(rows // tm, n_tiles),
        in_specs=in_specs,
        out_specs=pl.BlockSpec((TILE_CB, tm, LANE), lambda i, j: (j, i, 0)),
        out_shape=jax.ShapeDtypeStruct((cb1 - cb0, rows, LANE), out_dtype),
        scratch_shapes=scratch,
        compiler_params=_cparams(("parallel", "arbitrary")),
        name=f"in_projection_{cb0}",
    )(*args)


def _band_bias(qb, hw):
    qi = lax.broadcasted_iota(jnp.int32, (qb, qb + 2 * hw), 0)
    kj = lax.broadcasted_iota(jnp.int32, (qb, qb + 2 * hw), 1)
    return jnp.where(jnp.abs(kj - hw - qi) <= hw, 0.0, MASK_VALUE)


def _first_grid_step():
    first = pl.program_id(0) == 0
    for axis in (1, 2):
        first = first & (pl.program_id(axis) == 0)
    return first


def _attend_tile(t, q_ref, k_refs, v_refs, kbuf, vbuf, band_ref, kbias_ref, emit, *,
                 dil, hw, cs, qb, n_sub, n_q, sink, unroll):
    nk = qb + 2 * hw
    nks = cs + 2 * hw
    n_u = cs // qb
    kp_ref, kc_ref, kn_ref = k_refs
    vp_ref, vc_ref, vn_ref = v_refs
    kbuf[:, 0:hw] = kp_ref[...]
    kbuf[:, hw:hw + cs] = kc_ref[...]
    kbuf[:, hw + cs:] = kn_ref[...]
    vbuf[:, 0:hw, 0:LANE] = vp_ref[...]
    vbuf[:, hw:hw + cs, 0:LANE] = vc_ref[...]
    vbuf[:, hw + cs:, 0:LANE] = vn_ref[...]

    @pl.when(_first_grid_step())
    def _():
        vbuf[:, :, LANE:] = jnp.ones((dil, nks, LANE), BF16)

    kpos = t * cs - hw + lax.broadcasted_iota(jnp.int32, (1, nks), 1)
    kbias_ref[...] = jnp.where((kpos >= 0) & (kpos < n_sub), 0.0, MASK_VALUE)

    n_blocks = dil * n_u
    group_size = min(unroll, n_blocks)
    assert n_blocks % group_size == 0

    def group(i, carry):
        units = []
        for j in range(group_size):
            idx = i * group_size + j
            u = idx % n_u
            ph = idx // n_u
            units.append((u, ph, pl.multiple_of(u * qb, qb)))
        scores = []
        for u, ph, u0 in units:
            q = q_ref[:, ph, pl.ds(u0, qb), :].reshape(n_q * qb, LANE)
            s = lax.dot_general(q, kbuf[ph, pl.ds(u0, nk), :],
                                (((1,), (1,)), ((), ())), preferred_element_type=F32)
            scores.append(s.reshape(n_q, qb, nk) + band_ref[...] + kbias_ref[:, pl.ds(u0, nk)])
        probs, maxes = [], []
        for s in scores:
            m = jnp.max(s, axis=-1, keepdims=True)
            if sink is not None:
                m = jnp.maximum(m, sink)
            maxes.append(m)
            probs.append(jnp.exp2(s - m).astype(BF16).reshape(n_q * qb, nk))
        for (u, ph, u0), p, m in zip(units, probs, maxes):
            o2 = jnp.dot(p, vbuf[ph, pl.ds(u0, nk), :], preferred_element_type=F32)
            rows = pl.ds(u0, qb) if dil == 1 else pl.ds(ph + u * (qb * dil), qb, stride=dil)
            emit(rows, o2.reshape(n_q, qb, 2 * LANE), m)
        return carry

    lax.fori_loop(0, n_blocks // group_size, group, 0)


def _tile_specs(view_dims, cb, n, dil, cs, hw, n_t):
    per = cs // hw
    lead = cb if n is None else cb // n

    def spec(rows, tile_of, blk):
        return pl.BlockSpec((n, None, None, dil, rows, LANE),
                            lambda b, hd, t: (lead + hd, b, tile_of(t), 0, blk, 0))

    return (spec(hw, lambda t: jnp.maximum(t - 1, 0), per - 1),
            spec(cs, lambda t: t, 0),
            spec(hw, lambda t: jnp.minimum(t + 1, n_t - 1), 0))


def _windowed_attention_body(q_ref, kp_ref, kc_ref, kn_ref, vp_ref, vc_ref, vn_ref, sink_ref, o_ref,
                             kbuf, vbuf, band_ref, kbias_ref, *, hw, cs, qb, n_sub, n_q, unroll):
    band_ref[...] = _band_bias(qb, hw)
    sink = sink_ref[...][:, :, :1] * LOG2_E

    def emit(rows, o2, m):
        denom = o2[:, :, LANE:] + jnp.exp2(sink - m)
        o_ref[:, rows, :] = (o2[:, :, :LANE] / denom).astype(o_ref.dtype)

    _attend_tile(pl.program_id(2), q_ref, (kp_ref, kc_ref, kn_ref), (vp_ref, vc_ref, vn_ref),
                 kbuf, vbuf, band_ref, kbias_ref, emit, dil=1, hw=hw, cs=cs, qb=qb, n_sub=n_sub,
                 n_q=n_q, sink=sink, unroll=unroll)


def _windowed_attention(pa, sink, *, seq, tile, hw, qb, unroll):
    n_cb, rows, _ = pa.shape
    bsz = rows // seq
    n_t = seq // tile
    n_q = B_Q_HEADS // B_KV_HEADS
    view = pa.reshape(n_cb, bsz, n_t, 1, tile, LANE)
    in_specs = ([_tile_specs(view.shape, CB_BQ, n_q, 1, tile, hw, n_t)[1]]
                + list(_tile_specs(view.shape, CB_BK, None, 1, tile, hw, n_t))
                + list(_tile_specs(view.shape, CB_BV, None, 1, tile, hw, n_t))
                + [pl.BlockSpec((n_q, 1, LANE), lambda b, hd, t: (hd, 0, 0))])
    out = pl.pallas_call(
        functools.partial(_windowed_attention_body, hw=hw, cs=tile, qb=qb, n_sub=seq, n_q=n_q,
                          unroll=unroll),
        grid=(bsz, B_KV_HEADS, n_t),
        in_specs=in_specs,
        out_specs=pl.BlockSpec((n_q, None, tile, LANE), lambda b, hd, t: (hd, b, t, 0)),
        out_shape=jax.ShapeDtypeStruct((B_Q_HEADS, bsz, seq, LANE), BF16),
        scratch_shapes=[pltpu.VMEM((1, tile + 2 * hw, LANE), BF16),
                        pltpu.VMEM((1, tile + 2 * hw, 2 * LANE), BF16),
                        pltpu.VMEM((qb, qb + 2 * hw), F32),
                        pltpu.VMEM((1, tile + 2 * hw), F32)],
        compiler_params=_cparams(("arbitrary", "arbitrary", "arbitrary")),
        name="windowed_attention",
    )(*([view] * 7 + [sink]))
    return out.reshape(B_Q_HEADS, rows, LANE)


def _dilated_attention_body(*refs, tile, hw, qb, seq, unroll):
    n_g = len(DIL_GROUPS)
    group_refs = [refs[7 * g:7 * g + 7] for g in range(n_g)]
    gate_ref, o_ref = refs[7 * n_g:7 * n_g + 2]
    scratch = refs[7 * n_g + 2:]
    bufs = [scratch[3 * g:3 * g + 3] for g in range(n_g)]
    band_ref, num_ref, max_ref, den_ref = scratch[3 * n_g:]
    band_ref[...] = _band_bias(qb, hw)

    for g, (_, dil) in enumerate(DIL_GROUPS):
        q_ref, kp_ref, kc_ref, kn_ref, vp_ref, vc_ref, vn_ref = group_refs[g]
        kbuf, vbuf, kbias_ref = bufs[g]

        def emit(rows, o2, m, g=g):
            num_ref[g, rows, :] = o2[0, :, :LANE]
            den_ref[g, rows, :] = o2[0, :, LANE:]
            max_ref[g, rows, :] = jnp.broadcast_to(m[0], (qb, LANE))

        _attend_tile(pl.program_id(2), q_ref, (kp_ref, kc_ref, kn_ref), (vp_ref, vc_ref, vn_ref),
                     kbuf, vbuf, band_ref, kbias_ref, emit, dil=dil, hw=hw, cs=tile // dil, qb=qb,
                     n_sub=seq // dil, n_q=1, sink=None, unroll=unroll)

    top = functools.reduce(jnp.maximum, [max_ref[g] for g in range(n_g)])
    weights = [jnp.exp2(max_ref[g] - top) for g in range(n_g)]
    num = sum(w * num_ref[g] for g, w in enumerate(weights))
    den = sum(w * den_ref[g] for g, w in enumerate(weights))
    o_ref[...] = (num / den * _silu(gate_ref[...].astype(F32))).astype(o_ref.dtype)


def _dilated_attention(pa, *, seq, tile, qb, unroll):
    n_cb, rows, _ = pa.shape
    bsz = rows // seq
    n_t = seq // tile
    hws = {window // (2 * dil) for window, dil in DIL_GROUPS}
    assert len(hws) == 1
    hw = hws.pop()
    in_specs, args, scratch = [], [], []
    for g, (_, dil) in enumerate(DIL_GROUPS):
        cs = tile // dil
        assert cs % qb == 0 and cs % hw == 0
        view = pa.reshape(n_cb, bsz, n_t, dil, cs, LANE)
        in_specs += ([_tile_specs(view.shape, CB_AQ + g * A_SLOTS, 1, dil, cs, hw, n_t)[1]]
                     + list(_tile_specs(view.shape, CB_AK + g * A_SLOTS, None, dil, cs, hw, n_t))
                     + list(_tile_specs(view.shape, CB_AV + g * A_SLOTS, None, dil, cs, hw, n_t)))
        args += [view] * 7
        scratch += [pltpu.VMEM((dil, cs + 2 * hw, LANE), BF16),
                    pltpu.VMEM((dil, cs + 2 * hw, 2 * LANE), BF16),
                    pltpu.VMEM((1, cs + 2 * hw), F32)]
    row_spec = pl.BlockSpec((None, None, tile, LANE), lambda b, hd, t: (hd, b, t, 0))
    in_specs.append(pl.BlockSpec((None, None, tile, LANE), lambda b, hd, t: (CB_AGATE + hd, b, t, 0)))
    args.append(pa.reshape(n_cb, bsz, seq, LANE))
    n_g = len(DIL_GROUPS)
    scratch += [pltpu.VMEM((qb, qb + 2 * hw), F32),
                pltpu.VMEM((n_g, tile, LANE), F32),
                pltpu.VMEM((n_g, tile, LANE), F32),
                pltpu.VMEM((n_g, tile, LANE), F32)]
    out = pl.pallas_call(
        functools.partial(_dilated_attention_body, tile=tile, hw=hw, qb=qb, seq=seq, unroll=unroll),
        grid=(bsz, A_SLOTS, n_t),
        in_specs=in_specs,
        out_specs=row_spec,
        out_shape=jax.ShapeDtypeStruct((A_SLOTS, bsz, seq, LANE), BF16),
        scratch_shapes=scratch,
        compiler_params=_cparams(("arbitrary", "arbitrary", "arbitrary")),
        name="dilated_attention",
    )(*args)
    return out.reshape(A_SLOTS, rows, LANE)


def _hgrn_body(*refs, layer, reverse, tb, finalize, hp):
    q_ref, f_ref, v_ref, lbp_ref = refs[:4]
    rest = list(refs[4:])
    if finalize:
        of_ref, gate_ref, gain_ref = rest[:3]
        rest = rest[3:]
    o_ref, st_ref, qs_ref, ks_ref, bs_ref, sc_ref, rest_ref = rest
    c = C_CHUNK
    n = tb // c

    @pl.when(pl.program_id(2) == 0)
    def _():
        st_ref[...] = jnp.zeros_like(st_ref)

    row = lax.broadcasted_iota(jnp.int32, (c, c), 0)
    col = lax.broadcasted_iota(jnp.int32, (c, c), 1)
    causal = (row <= col) if reverse else (row >= col)
    tri = jnp.where(causal, 1.0, 0.0).astype(BF16)

    def widen(a):
        return jnp.concatenate([a[i * c:(i + 1) * c] for i in range(n)], axis=1)

    def gates(h):
        fz = f_ref[h]
        e = jnp.exp2(jnp.abs(fz) * -LOG2_E)
        log_sig = jnp.minimum(fz, 0.0) * LOG2_E - jnp.log2(1.0 + e)
        sig_neg = jnp.where(fz >= 0.0, e, 1.0) / (1.0 + e)
        if layer == 0:
            log_f, kk = log_sig, sig_neg
        else:
            lbp = lbp_ref[:, h]
            e_l = jnp.exp(lbp - jnp.max(lbp, axis=0, keepdims=True))
            sm = e_l / jnp.sum(e_l, axis=0, keepdims=True)
            lb = sm[1]
            for i in range(2, layer + 1):
                lb = lb + sm[i]
            one_m_lb = 1.0 - lb
            tiny = jnp.float32(1e-37)
            log_lb = jnp.where(lb > 0.0, jnp.log2(jnp.maximum(lb, tiny)), -jnp.inf)
            log_1m_lb = jnp.where(one_m_lb > 0.0, jnp.log2(jnp.maximum(one_m_lb, tiny)), -jnp.inf)
            cand = log_1m_lb + log_sig
            log_f = (jnp.maximum(log_lb, cand)
                     + jnp.log2(1.0 + jnp.exp2(-jnp.abs(log_lb - cand))))
            kk = one_m_lb * sig_neg
        qh = _silu(q_ref[h].astype(F32))

        hi = log_f.astype(BF16)
        lo = (log_f - hi.astype(F32)).astype(BF16)
        b_wide = (jnp.dot(tri, widen(hi), preferred_element_type=F32)
                  + jnp.dot(tri, widen(lo), preferred_element_type=F32))
        b3 = jnp.stack([b_wide[:, i * LANE:(i + 1) * LANE] for i in range(n)])
        return qh.reshape(n, c, LANE), kk.reshape(n, c, LANE), b3

    heads = [gates(h) for h in range(hp)]
    mids = [b3 - b3[:, c // 2 - 1:c // 2] for _, _, b3 in heads]
    span = functools.reduce(jnp.maximum, [jnp.max(jnp.abs(d3)) for d3 in mids])

    def finish(h, o):
        o = o.reshape(tb, LANE)
        if finalize:
            o = _rms(o + of_ref[h], gain_ref[h]) * _silu(gate_ref[h].astype(F32))
        o_ref[h] = o.astype(o_ref.dtype)

    def intra(h, sc):
        scores = jnp.where(causal, sc, 0.0).astype(BF16)
        return jnp.einsum('nts,nsv->ntv', scores, v_ref[h].reshape(n, c, LANE),
                          preferred_element_type=F32)

    partial_out, updates = [], []
    for h, ((qh3, kk3, b3), d3) in enumerate(zip(heads, mids)):
        grow = jnp.exp2(jnp.clip(d3, -HGRN_FACTOR_RANGE, HGRN_FACTOR_RANGE))
        qt = (qh3 * grow).astype(BF16)
        kt = (kk3 / grow).astype(BF16)
        sc = jnp.einsum('ntk,nsk->nts', qt, kt, preferred_element_type=F32)
        v3 = v_ref[h].reshape(n, c, LANE)
        b_end = b3[:, 0:1] if reverse else b3[:, c - 1:c]
        k_in = (kk3 * jnp.exp2(b_end - b3)).astype(BF16)
        kv = jnp.einsum('nsv,nsk->nvk', v3, k_in, preferred_element_type=F32)
        partial_out.append(intra(h, sc))
        updates.append((kv, jnp.exp2(b_end)))
    incoming = []
    for h, (kv, decay_end) in enumerate(updates):
        st = st_ref[h]
        before = [None] * n
        for i in (range(n - 1, -1, -1) if reverse else range(n)):
            before[i] = st.astype(BF16)
            st = st * decay_end[i] + kv[i]
        st_ref[h] = st
        incoming.append(jnp.stack(before))
    for h, (qh3, kk3, b3) in enumerate(heads):
        inter = jnp.einsum('ntk,nvk->ntv', (qh3 * jnp.exp2(b3)).astype(BF16), incoming[h],
                           preferred_element_type=F32)
        rest_ref[h] = inter
        finish(h, partial_out[h] + inter)

    @pl.when(span > HGRN_FACTOR_RANGE)
    def _():
        for h, (qh3, kk3, b3) in enumerate(heads):
            qs_ref[h] = qh3
            ks_ref[h] = kk3
            bs_ref[h] = b3

        def per_chunk(i, carry):
            h, ci = i // n, i % n

            def pair(s, sc):
                rel = jnp.minimum(bs_ref[h, ci] - bs_ref[h, ci, pl.ds(s, 1), :], 0.0)
                w = qs_ref[h, ci] * ks_ref[h, ci, pl.ds(s, 1), :] * jnp.exp2(rel)
                return jnp.where(col == s, jnp.sum(w, axis=-1, keepdims=True), sc)

            sc_ref[h, ci] = lax.fori_loop(0, c, pair, jnp.zeros((c, c), F32))
            return carry

        lax.fori_loop(0, hp * n, per_chunk, 0)
        for h in range(hp):
            finish(h, intra(h, sc_ref[h]) + rest_ref[h])


def _hgrn_scan(pa, pf, pc, lower_bounds, *, seq, layer, reverse, tb, hp, o_fwd=None, gain=None):
    bsz = pa.shape[1] // seq
    tb = min(tb, seq)
    n_t = seq // tb
    n_chunks = tb // C_CHUNK
    finalize = o_fwd is not None
    assert C_HEADS % hp == 0

    def blk(cb):
        assert cb % hp == 0
        return pl.BlockSpec((hp, None, tb, LANE),
                            lambda b, hd, t: (cb // hp + hd, b, (n_t - 1 - t) if reverse else t, 0))

    def view(p):
        return p.reshape(p.shape[0], bsz, seq, LANE)

    f_cb = (CB_CFB if reverse else CB_CFF) - SEG_F[0]
    in_specs = [blk(CB_CQ - SEG_A[0]), blk(f_cb), blk(CB_CI - SEG_C[0]),
                pl.BlockSpec((DEPTH, hp, 1, LANE), lambda b, hd, t: (0, hd, 0, 0))]
    args = [view(pa), view(pf), view(pc), lower_bounds.reshape(DEPTH, C_HEADS, 1, LANE)]
    if finalize:
        in_specs += [blk(0), blk(CB_CGATE - SEG_C[0]),
                     pl.BlockSpec((hp, 1, LANE), lambda b, hd, t: (hd, 0, 0))]
        args += [o_fwd, view(pc), gain.reshape(C_HEADS, 1, LANE)]
    out_dtype = BF16 if finalize else F32
    return pl.pallas_call(
        functools.partial(_hgrn_body, layer=layer, reverse=reverse, tb=tb, finalize=finalize, hp=hp),
        grid=(bsz, C_HEADS // hp, n_t),
        in_specs=in_specs,
        out_specs=blk(0),
        out_shape=jax.ShapeDtypeStruct((C_HEADS, bsz, seq, LANE), out_dtype),
        scratch_shapes=[pltpu.VMEM((hp, LANE, LANE), F32),
                        pltpu.VMEM((hp, n_chunks, C_CHUNK, LANE), F32),
                        pltpu.VMEM((hp, n_chunks, C_CHUNK, LANE), F32),
                        pltpu.VMEM((hp, n_chunks, C_CHUNK, LANE), F32),
                        pltpu.VMEM((hp, n_chunks, C_CHUNK, C_CHUNK), F32),
                        pltpu.VMEM((hp, n_chunks, C_CHUNK, LANE), F32)],
        compiler_params=_cparams(("parallel", "parallel", "arbitrary")),
        name="hgrn_scan_bwd" if reverse else "hgrn_scan_fwd",
    )(*args)


def _silu(g):
    return g / (1.0 + jnp.exp(-g))


def _sigmoid(g):
    return 1.0 / (1.0 + jnp.exp(-g))


def _lane_concat(ref, n):
    return jnp.concatenate([ref[i] for i in range(n)], axis=-1)


def _merge_out_body(*refs, emit_next):
    ya_ref, ob_ref, bg0_ref, bg1_ref, yc_ref = refs[:5]
    m_refs = refs[5:17]
    x_ref, wa_ref, wb_ref, wc_ref, wo_ref, pg_ref = refs[17:23]
    if emit_next:
        ng_ref, out_ref, xn_ref = refs[23:]
    else:
        (out_ref,) = refs[23:]

    ya = _lane_concat(ya_ref, A_SLOTS)
    yb = []
    for hd in range(B_Q_HEADS):
        g_ref = bg0_ref if hd < 4 else bg1_ref
        yb.append((ob_ref[hd].astype(F32) * _silu(g_ref[hd % 4].astype(F32))).astype(BF16))
    yb = jnp.concatenate(yb, axis=-1)
    yc = _lane_concat(yc_ref, C_HEADS)

    proj = (jnp.dot(ya, wa_ref[...], preferred_element_type=F32),
            jnp.dot(yb, wb_ref[...], preferred_element_type=F32),
            jnp.dot(yc, wc_ref[...], preferred_element_type=F32))
    per_branch = D_MODEL // (TILE_CB * LANE)
    y = []
    for j in range(per_branch):
        cols = slice(j * TILE_CB * LANE, (j + 1) * TILE_CB * LANE)
        acc = None
        for br in range(N_BRANCHES):
            gate = _sigmoid(_lane_concat(m_refs[br * per_branch + j], TILE_CB).astype(F32))
            term = gate * proj[br][:, cols]
            acc = term if acc is None else acc + term
        y.append(acc.astype(BF16))
    y = jnp.concatenate(y, axis=-1)
    out = jnp.dot(y, wo_ref[...], preferred_element_type=F32)
    x_new = x_ref[...] + _rms(out, pg_ref[...])
    out_ref[...] = x_new
    if emit_next:
        xn_ref[...] = _rms(x_new, ng_ref[...]).astype(xn_ref.dtype)


def _merge_out(x2, pa, pc, ya, ob, yc, wa, wb, wc, wo, layer, post_gain, next_gain, *, tm):
    rows, d = x2.shape
    emit_next = next_gain is not None

    def heads(n, first=0):
        return pl.BlockSpec((n, tm, LANE), lambda i: (first // n, i, 0))

    def const(shape):
        return pl.BlockSpec(shape, lambda i: (0,) * len(shape), pipeline_mode=pl.Buffered(1))

    def weight(w):
        return pl.BlockSpec((None,) + w.shape[1:], lambda i: (layer, 0, 0),
                            pipeline_mode=pl.Buffered(1))

    n_merge = N_BRANCHES * D_MODEL // (TILE_CB * LANE)
    in_specs = ([heads(A_SLOTS), heads(B_Q_HEADS),
                 heads(4, CB_BGATE), heads(4, CB_BGATE + 4), heads(C_HEADS)]
                + [heads(TILE_CB, CB_MERGE - SEG_C[0] + TILE_CB * j) for j in range(n_merge)]
                + [pl.BlockSpec((tm, d), lambda i: (i, 0)),
                   weight(wa), weight(wb), weight(wc), weight(wo),
                   const((1, d))])
    args = ([ya, ob, pa, pa, yc] + [pc] * n_merge
            + [x2, wa, wb, wc, wo, post_gain.reshape(1, d)])
    row_spec = pl.BlockSpec((tm, d), lambda i: (i, 0))
    out_specs, out_shape = row_spec, jax.ShapeDtypeStruct((rows, d), F32)
    if emit_next:
        in_specs.append(const((1, d)))
        args.append(next_gain.reshape(1, d))
        out_specs = [row_spec, row_spec]
        out_shape = [out_shape, jax.ShapeDtypeStruct((rows, d), BF16)]
    return pl.pallas_call(
        functools.partial(_merge_out_body, emit_next=emit_next),
        grid=(rows // tm,),
        in_specs=in_specs,
        out_specs=out_specs,
        out_shape=out_shape,
        compiler_params=_cparams(("parallel",)),
        name="merge_out",
    )(*args)


def kernel(x, pre_norm, post_norm, w_in, sink_logits, hgrn_lower_bounds, hgrn_norm,
           w_branch_a, w_branch_b, w_branch_c, w_out):
    bsz, seq, d = x.shape
    rows = bsz * seq
    assert d == D_MODEL and w_in.shape[-1] == IN_COLS
    tile = min(PROJ_TILE_ROWS, seq)
    assert seq % tile == 0
    rope = _rope_tables(seq)
    plans = _projection_plans()
    x2 = x.reshape(rows, d)
    xn = _pre_norm(x2, pre_norm[0].astype(F32), tm=min(512, rows))
    w, wa, wb, wc, wo = (t.astype(BF16) for t in (w_in, w_branch_a, w_branch_b, w_branch_c, w_out))
    for layer in range(DEPTH):
        pa = _in_projection(xn, w, layer, rope, seq=seq, seg=SEG_A, out_dtype=BF16, plans=plans,
                            tm=tile)
        pf = _in_projection(xn, w, layer, None, seq=seq, seg=SEG_F, out_dtype=F32, plans=None,
                            tm=tile)
        pc = _in_projection(xn, w, layer, None, seq=seq, seg=SEG_C, out_dtype=BF16, plans=None,
                            tm=tile)

        ya = _dilated_attention(pa, seq=seq, tile=tile, qb=128, unroll=16)
        sink = jnp.broadcast_to(sink_logits[layer].astype(F32)[:, None, None], (B_Q_HEADS, 1, LANE))
        ob = _windowed_attention(pa, sink, seq=seq, tile=tile, hw=B_HALF_WINDOW, qb=128, unroll=16)

        lower = hgrn_lower_bounds.astype(F32)
        o_fwd = _hgrn_scan(pa, pf, pc, lower, seq=seq, layer=layer, reverse=False, tb=1024, hp=4)
        yc = _hgrn_scan(pa, pf, pc, lower, seq=seq, layer=layer, reverse=True, tb=1024, hp=4,
                        o_fwd=o_fwd, gain=hgrn_norm[layer].astype(F32))
        yc = yc.reshape(C_HEADS, rows, LANE)

        next_gain = pre_norm[layer + 1].astype(F32) if layer + 1 < DEPTH else None
        res = _merge_out(x2, pa, pc, ya, ob, yc, wa, wb, wc, wo, layer,
                         post_norm[layer].astype(F32), next_gain, tm=min(256, rows))
        x2, xn = res if next_gain is not None else (res, None)
    return x2.reshape(bsz, seq, d)
```

```python
import functools

import jax
import jax.numpy as jnp
from jax import lax
from jax.experimental import pallas as pl
from jax.experimental.pallas import tpu as pltpu

F32 = jnp.float32
BF16 = jnp.bfloat16

LANE = 128
D_MODEL = 2048
DEPTH = 2
HEAD_DIM = 128
ROT_DIM = HEAD_DIM // 4
ROT_HALF = ROT_DIM // 2
ROPE_THETA = 500000.0
NORM_EPS = 1e-6

DIL_GROUPS = ((128, 1), (512, 4), (2048, 16))
A_SLOTS = 4
A_HEADS = A_SLOTS * len(DIL_GROUPS)
B_Q_HEADS = 8
B_KV_HEADS = 2
B_HALF_WINDOW = 128
C_HEADS = 8
C_CHUNK = 64
N_BRANCHES = 3

CB_AQ = 0
CB_AK = CB_AQ + A_HEADS
CB_AV = CB_AK + A_HEADS
CB_AGATE = CB_AV + A_HEADS
CB_BQ = CB_AGATE + A_SLOTS
CB_BK = CB_BQ + B_Q_HEADS
CB_BV = CB_BK + B_KV_HEADS
CB_BGATE = CB_BV + B_KV_HEADS
CB_CQ = CB_BGATE + B_Q_HEADS
CB_CFF = CB_CQ + C_HEADS
CB_CFB = CB_CFF + C_HEADS
CB_CI = CB_CFB + C_HEADS
CB_CGATE = CB_CI + C_HEADS
CB_MERGE = CB_CGATE + C_HEADS
N_CB = CB_MERGE + N_BRANCHES * D_MODEL // LANE
IN_COLS = N_CB * LANE
TILE_CB = 4

SEG_A, SEG_F, SEG_C = (0, CB_CFF), (CB_CFF, CB_CI), (CB_CI, N_CB)

PROJ_TILE_ROWS = 2048
PRE_NORM_ROWS = 512
MERGE_ROWS = 256
ATTN_QUERY_BLOCK = 128
ATTN_UNITS_PER_BLOCK = 16
ATTN_SCORES_AHEAD = 4
HGRN_BLOCK_ROWS = 1024
HGRN_HEADS_PER_STEP = 4
NO_ROPE, ROPE_K, ROPE_Q = 0, 1, 2

MASK_VALUE = -1e30
LOG2_E = 1.4426950408889634
HGRN_FACTOR_RANGE = 115.0

VMEM_LIMIT = 56 * 1024 * 1024


def _cparams(sem):
    return pltpu.CompilerParams(dimension_semantics=sem, vmem_limit_bytes=VMEM_LIMIT)


def _projection_plans():
    codes = [NO_ROPE] * SEG_A[1]
    dils = [1] * SEG_A[1]
    for g, (_, dil) in enumerate(DIL_GROUPS):
        for s in range(A_SLOTS):
            codes[CB_AQ + g * A_SLOTS + s] = ROPE_Q
            codes[CB_AK + g * A_SLOTS + s] = ROPE_K
            for base in (CB_AQ, CB_AK, CB_AV):
                dils[base + g * A_SLOTS + s] = dil
    for h in range(B_Q_HEADS):
        codes[CB_BQ + h] = ROPE_Q
    for h in range(B_KV_HEADS):
        codes[CB_BK + h] = ROPE_K
    plans = []
    for t in range(SEG_A[1] // TILE_CB):
        sl = slice(t * TILE_CB, (t + 1) * TILE_CB)
        assert len(set(dils[sl])) == 1
        plans.append((tuple(codes[sl]), dils[sl][0]))
    return tuple(plans)


def _rms(x, gain):
    return x * lax.rsqrt(jnp.mean(x * x, axis=-1, keepdims=True) + NORM_EPS) * gain


def _pre_norm_body(x_ref, g_ref, o_ref):
    o_ref[...] = _rms(x_ref[...], g_ref[...]).astype(o_ref.dtype)


def _pre_norm(x2, gain, *, tm):
    rows, d = x2.shape
    return pl.pallas_call(
        _pre_norm_body,
        grid=(rows // tm,),
        in_specs=[pl.BlockSpec((tm, d), lambda i: (i, 0)), pl.BlockSpec((1, d), lambda i: (0, 0))],
        out_specs=pl.BlockSpec((tm, d), lambda i: (i, 0)),
        out_shape=jax.ShapeDtypeStruct((rows, d), BF16),
        compiler_params=_cparams(("parallel",)),
        name="pre_norm",
    )(x2, gain.reshape(1, d))


def _rope_tables(seq):
    pos = jnp.arange(seq, dtype=F32)
    inv_freq = ROPE_THETA ** (-jnp.arange(0, ROT_DIM, 2, dtype=F32) / ROT_DIM)
    ang = pos[:, None] * inv_freq[None, :]
    cos, sin = jnp.cos(ang), jnp.sin(ang)
    rest = HEAD_DIM - ROT_DIM
    c_full = jnp.concatenate([cos, cos, jnp.ones((seq, rest), F32)], axis=-1)
    s_full = jnp.concatenate([-sin, sin, jnp.zeros((seq, rest), F32)], axis=-1)
    return jnp.stack([c_full, s_full])


def _rotate_half_matrix(n_heads):
    width = n_heads * LANE
    src = lax.broadcasted_iota(jnp.int32, (width, width), 0)
    dst = lax.broadcasted_iota(jnp.int32, (width, width), 1)
    lane = dst % LANE
    hit = ((lane < ROT_HALF) & (src == dst + ROT_HALF)) | (
        (lane >= ROT_HALF) & (lane < ROT_DIM) & (src == dst - ROT_HALF))
    return jnp.where(hit, 1.0, 0.0).astype(BF16)


def _apply_rope(t, cos_full, sin_signed, swap):
    n_heads = t.shape[1] // LANE
    swapped = jnp.dot(t.astype(BF16), swap, preferred_element_type=F32)
    tile = lambda a: jnp.concatenate([a] * n_heads, axis=1) if n_heads > 1 else a
    return t * tile(cos_full) + swapped * tile(sin_signed)


def _in_projection_body(*refs, plans, tm):
    if plans is None:
        x_ref, w_ref, o_ref = refs
    else:
        x_ref, w_ref, rope_ref, o_ref, stage_ref = refs
    acc = jnp.dot(x_ref[...], w_ref[...], preferred_element_type=F32)
    for c in range(TILE_CB):
        o_ref[c] = acc[:, c * LANE:(c + 1) * LANE].astype(o_ref.dtype)
    if plans is None:
        return

    j = pl.program_id(1)
    groups = {}
    for tile, plan in enumerate(plans):
        if plan != ((NO_ROPE,) * TILE_CB, 1):
            groups.setdefault(plan, []).append(tile)
    scale = HEAD_DIM ** -0.5 * LOG2_E
    pair = 2
    swap = _rotate_half_matrix(pair)
    for (codes, dil), tiles in groups.items():
        cond = functools.reduce(jnp.logical_or, [j == t for t in tiles])

        @pl.when(cond)
        def _(codes=codes, dil=dil):
            for c0 in range(0, TILE_CB, pair):
                code = codes[c0]
                assert all(codes[c] == code for c in range(c0, c0 + pair))
                if code == NO_ROPE and dil == 1:
                    continue
                t = acc[:, c0 * LANE:(c0 + pair) * LANE]
                if code != NO_ROPE:
                    t = _apply_rope(t, rope_ref[0], rope_ref[1], swap)
                if code == ROPE_Q:
                    t = t * scale
                for c in range(c0, c0 + pair):
                    part = t[:, (c - c0) * LANE:(c - c0 + 1) * LANE]
                    if dil == 1:
                        o_ref[c] = part.astype(o_ref.dtype)
                    else:
                        stage_ref[c] = part
            if dil > 1:
                cs = tm // dil
                for c in range(TILE_CB):
                    def phase(p, carry, c=c):
                        dst = pl.ds(pl.multiple_of(p * cs, cs), cs)
                        src = stage_ref[c, pl.ds(p, cs, stride=dil), :]
                        o_ref[c, dst, :] = src.astype(o_ref.dtype)
                        return carry

                    lax.fori_loop(0, dil, phase, 0)


def _in_projection(xn, w_all, layer, rope, *, seq, seg, out_dtype, plans, tm):
    rows, d = xn.shape
    cb0, cb1 = seg
    n_tiles = (cb1 - cb0) // TILE_CB
    tn = TILE_CB * LANE
    tile0 = cb0 // TILE_CB
    in_specs = [pl.BlockSpec((tm, d), lambda i, j: (i, 0)),
                pl.BlockSpec((None, d, tn), lambda i, j: (layer, 0, tile0 + j))]
    args = [xn, w_all]
    scratch = []
    if plans is not None:
        per_seq = seq // tm
        in_specs.append(pl.BlockSpec((2, tm, LANE), lambda i, j: (0, i % per_seq, 0)))
        args.append(rope)
        scratch.append(pltpu.VMEM((TILE_CB, tm, LANE), F32))
    return pl.pallas_call(
        functools.partial(_in_projection_body, plans=plans, tm=tm),
        grid=(rows // tm, n_tiles),
        in_specs=in_specs,
        out_specs=pl.BlockSpec((TILE_CB, tm, LANE), lambda i, j: (j, i, 0)),
        out_shape=jax.ShapeDtypeStruct((cb1 - cb0, rows, LANE), out_dtype),
        scratch_shapes=scratch,
        compiler_params=_cparams(("parallel", "arbitrary")),
        name=f"in_projection_{cb0}",
    )(*args)


def _band_bias(qb, hw):
    qi = lax.broadcasted_iota(jnp.int32, (qb, qb + 2 * hw), 0)
    kj = lax.broadcasted_iota(jnp.int32, (qb, qb + 2 * hw), 1)
    return jnp.where(jnp.abs(kj - hw - qi) <= hw, 0.0, MASK_VALUE)


def _first_grid_step():
    first = pl.program_id(0) == 0
    for axis in (1, 2):
        first = first & (pl.program_id(axis) == 0)
    return first


def _attend_tile(t, q_ref, k_refs, v_refs, kbuf, vbuf, band_ref, kbias_ref, emit, *,
                 dil, hw, cs, qb, n_sub, n_q, sink, unroll, ahead):
    nk = qb + 2 * hw
    nks = cs + 2 * hw
    n_u = cs // qb
    kp_ref, kc_ref, kn_ref = k_refs
    vp_ref, vc_ref, vn_ref = v_refs
    kbuf[:, 0:hw] = kp_ref[...]
    kbuf[:, hw:hw + cs] = kc_ref[...]
    kbuf[:, hw + cs:] = kn_ref[...]
    vbuf[:, 0:hw, 0:LANE] = vp_ref[...]
    vbuf[:, hw:hw + cs, 0:LANE] = vc_ref[...]
    vbuf[:, hw + cs:, 0:LANE] = vn_ref[...]

    @pl.when(_first_grid_step())
    def _():
        vbuf[:, :, LANE:] = jnp.ones((dil, nks, LANE), BF16)

    kpos = t * cs - hw + lax.broadcasted_iota(jnp.int32, (1, nks), 1)
    kbias_ref[...] = jnp.where((kpos >= 0) & (kpos < n_sub), 0.0, MASK_VALUE)

    n_blocks = dil * n_u
    group_size = min(unroll, n_blocks)
    assert n_blocks % group_size == 0

    def group(i, carry):
        units = []
        for j in range(group_size):
            idx = i * group_size + j
            u = idx % n_u
            ph = idx // n_u
            units.append((u, ph, pl.multiple_of(u * qb, qb)))

        def score(unit):
            u, ph, u0 = unit
            q = q_ref[:, ph, pl.ds(u0, qb), :].reshape(n_q * qb, LANE)
            s = lax.dot_general(q, kbuf[ph, pl.ds(u0, nk), :],
                                (((1,), (1,)), ((), ())), preferred_element_type=F32)
            return s.reshape(n_q, qb, nk) + band_ref[...] + kbias_ref[:, pl.ds(u0, nk)]

        def softmax(s):
            m = jnp.max(s, axis=-1, keepdims=True)
            if sink is not None:
                m = jnp.maximum(m, sink)
            return jnp.exp2(s - m).astype(BF16).reshape(n_q * qb, nk), m

        def finish(unit, p, m):
            u, ph, u0 = unit
            o2 = jnp.dot(p, vbuf[ph, pl.ds(u0, nk), :], preferred_element_type=F32)
            rows = pl.ds(u0, qb) if dil == 1 else pl.ds(ph + u * (qb * dil), qb, stride=dil)
            emit(rows, o2.reshape(n_q, qb, 2 * LANE), m)

        scores, probs = {}, {}
        for step in range(group_size + 2 * ahead):
            if step < group_size:
                scores[step] = score(units[step])
            j = step - ahead
            if 0 <= j < group_size:
                probs[j] = softmax(scores.pop(j))
            j = step - 2 * ahead
            if 0 <= j < group_size:
                finish(units[j], *probs.pop(j))
        return carry

    lax.fori_loop(0, n_blocks // group_size, group, 0)


def _tile_specs(cb, n, dil, cs, hw, n_t):
    per = cs // hw
    lead = cb if n is None else cb // n

    def spec(rows, tile_of, blk):
        return pl.BlockSpec((n, None, None, dil, rows, LANE),
                            lambda b, hd, t: (lead + hd, b, tile_of(t), 0, blk, 0))

    return (spec(hw, lambda t: jnp.maximum(t - 1, 0), per - 1),
            spec(cs, lambda t: t, 0),
            spec(hw, lambda t: jnp.minimum(t + 1, n_t - 1), 0))


def _windowed_attention_body(q_ref, kp_ref, kc_ref, kn_ref, vp_ref, vc_ref, vn_ref, sink_ref, o_ref,
                             kbuf, vbuf, band_ref, kbias_ref, *, hw, cs, qb, n_sub, n_q, unroll):
    band_ref[...] = _band_bias(qb, hw)
    sink = sink_ref[...][:, :, :1] * LOG2_E

    def emit(rows, o2, m):
        denom = o2[:, :, LANE:] + jnp.exp2(sink - m)
        o_ref[:, rows, :] = (o2[:, :, :LANE] / denom).astype(o_ref.dtype)

    _attend_tile(pl.program_id(2), q_ref, (kp_ref, kc_ref, kn_ref), (vp_ref, vc_ref, vn_ref),
                 kbuf, vbuf, band_ref, kbias_ref, emit, dil=1, hw=hw, cs=cs, qb=qb, n_sub=n_sub,
                 n_q=n_q, sink=sink, unroll=unroll, ahead=unroll)


def _windowed_attention(pa, sink, *, seq, tile, hw, qb, unroll):
    n_cb, rows, _ = pa.shape
    bsz = rows // seq
    n_t = seq // tile
    n_q = B_Q_HEADS // B_KV_HEADS
    view = pa.reshape(n_cb, bsz, n_t, 1, tile, LANE)
    in_specs = ([_tile_specs(CB_BQ, n_q, 1, tile, hw, n_t)[1]]
                + list(_tile_specs(CB_BK, None, 1, tile, hw, n_t))
                + list(_tile_specs(CB_BV, None, 1, tile, hw, n_t))
                + [pl.BlockSpec((n_q, 1, LANE), lambda b, hd, t: (hd, 0, 0))])
    out = pl.pallas_call(
        functools.partial(_windowed_attention_body, hw=hw, cs=tile, qb=qb, n_sub=seq, n_q=n_q,
                          unroll=unroll),
        grid=(bsz, B_KV_HEADS, n_t),
        in_specs=in_specs,
        out_specs=pl.BlockSpec((n_q, None, tile, LANE), lambda b, hd, t: (hd, b, t, 0)),
        out_shape=jax.ShapeDtypeStruct((B_Q_HEADS, bsz, seq, LANE), BF16),
        scratch_shapes=[pltpu.VMEM((1, tile + 2 * hw, LANE), BF16),
                        pltpu.VMEM((1, tile + 2 * hw, 2 * LANE), BF16),
                        pltpu.VMEM((qb, qb + 2 * hw), F32),
                        pltpu.VMEM((1, tile + 2 * hw), F32)],
        compiler_params=_cparams(("arbitrary", "arbitrary", "arbitrary")),
        name="windowed_attention",
    )(*([view] * 7 + [sink]))
    return out.reshape(B_Q_HEADS, rows, LANE)


def _dilated_attention_body(*refs, tile, hw, qb, seq, unroll):
    n_g = len(DIL_GROUPS)
    group_refs = [refs[7 * g:7 * g + 7] for g in range(n_g)]
    gate_ref, o_ref = refs[7 * n_g:7 * n_g + 2]
    scratch = refs[7 * n_g + 2:]
    bufs = [scratch[3 * g:3 * g + 3] for g in range(n_g)]
    band_ref, num_ref, max_ref, den_ref = scratch[3 * n_g:]
    band_ref[...] = _band_bias(qb, hw)

    for g, (_, dil) in enumerate(DIL_GROUPS):
        q_ref, kp_ref, kc_ref, kn_ref, vp_ref, vc_ref, vn_ref = group_refs[g]
        kbuf, vbuf, kbias_ref = bufs[g]

        def emit(rows, o2, m, g=g):
            num_ref[g, rows, :] = o2[0, :, :LANE]
            den_ref[g, rows, :] = o2[0, :, LANE:]
            max_ref[g, rows, :] = jnp.broadcast_to(m[0], (qb, LANE))

        _attend_tile(pl.program_id(2), q_ref, (kp_ref, kc_ref, kn_ref), (vp_ref, vc_ref, vn_ref),
                     kbuf, vbuf, band_ref, kbias_ref, emit, dil=dil, hw=hw, cs=tile // dil, qb=qb,
                     n_sub=seq // dil, n_q=1, sink=None, unroll=unroll,
                     ahead=ATTN_SCORES_AHEAD)

    top = functools.reduce(jnp.maximum, [max_ref[g] for g in range(n_g)])
    weights = [jnp.exp2(max_ref[g] - top) for g in range(n_g)]
    num = sum(w * num_ref[g] for g, w in enumerate(weights))
    den = sum(w * den_ref[g] for g, w in enumerate(weights))
    o_ref[...] = (num / den * _silu(gate_ref[...].astype(F32))).astype(o_ref.dtype)


def _dilated_attention(pa, *, seq, tile, qb, unroll):
    n_cb, rows, _ = pa.shape
    bsz = rows // seq
    n_t = seq // tile
    hws = {window // (2 * dil) for window, dil in DIL_GROUPS}
    assert len(hws) == 1
    hw = hws.pop()
    in_specs, args, scratch = [], [], []
    for g, (_, dil) in enumerate(DIL_GROUPS):
        cs = tile // dil
        assert cs % qb == 0 and cs % hw == 0
        view = pa.reshape(n_cb, bsz, n_t, dil, cs, LANE)
        in_specs += ([_tile_specs(CB_AQ + g * A_SLOTS, 1, dil, cs, hw, n_t)[1]]
                     + list(_tile_specs(CB_AK + g * A_SLOTS, None, dil, cs, hw, n_t))
                     + list(_tile_specs(CB_AV + g * A_SLOTS, None, dil, cs, hw, n_t)))
        args += [view] * 7
        scratch += [pltpu.VMEM((dil, cs + 2 * hw, LANE), BF16),
                    pltpu.VMEM((dil, cs + 2 * hw, 2 * LANE), BF16),
                    pltpu.VMEM((1, cs + 2 * hw), F32)]
    row_spec = pl.BlockSpec((None, None, tile, LANE), lambda b, hd, t: (hd, b, t, 0))
    in_specs.append(pl.BlockSpec((None, None, tile, LANE), lambda b, hd, t: (CB_AGATE + hd, b, t, 0)))
    args.append(pa.reshape(n_cb, bsz, seq, LANE))
    n_g = len(DIL_GROUPS)
    scratch += [pltpu.VMEM((qb, qb + 2 * hw), F32),
                pltpu.VMEM((n_g, tile, LANE), F32),
                pltpu.VMEM((n_g, tile, LANE), F32),
                pltpu.VMEM((n_g, tile, LANE), F32)]
    out = pl.pallas_call(
        functools.partial(_dilated_attention_body, tile=tile, hw=hw, qb=qb, seq=seq, unroll=unroll),
        grid=(bsz, A_SLOTS, n_t),
        in_specs=in_specs,
        out_specs=row_spec,
        out_shape=jax.ShapeDtypeStruct((A_SLOTS, bsz, seq, LANE), BF16),
        scratch_shapes=scratch,
        compiler_params=_cparams(("arbitrary", "arbitrary", "arbitrary")),
        name="dilated_attention",
    )(*args)
    return out.reshape(A_SLOTS, rows, LANE)


def _hgrn_body(*refs, layer, reverse, tb, finalize, hp):
    q_ref, f_ref, v_ref, lbp_ref = refs[:4]
    rest = list(refs[4:])
    if finalize:
        of_ref, gate_ref, gain_ref = rest[:3]
        rest = rest[3:]
    o_ref, st_ref, qs_ref, ks_ref, bs_ref, sc_ref, rest_ref = rest
    c = C_CHUNK
    n = tb // c

    @pl.when(pl.program_id(2) == 0)
    def _():
        st_ref[...] = jnp.zeros_like(st_ref)

    row = lax.broadcasted_iota(jnp.int32, (c, c), 0)
    col = lax.broadcasted_iota(jnp.int32, (c, c), 1)
    causal = (row <= col) if reverse else (row >= col)
    tri = jnp.where(causal, 1.0, 0.0).astype(BF16)

    def widen(a):
        return jnp.concatenate([a[i * c:(i + 1) * c] for i in range(n)], axis=1)

    def gates(h):
        fz = f_ref[h]
        e = jnp.exp2(jnp.abs(fz) * -LOG2_E)
        log_sig = jnp.minimum(fz, 0.0) * LOG2_E - jnp.log2(1.0 + e)
        sig_neg = jnp.where(fz >= 0.0, e, 1.0) / (1.0 + e)
        if layer == 0:
            log_f, kk = log_sig, sig_neg
        else:
            lbp = lbp_ref[:, h]
            e_l = jnp.exp(lbp - jnp.max(lbp, axis=0, keepdims=True))
            sm = e_l / jnp.sum(e_l, axis=0, keepdims=True)
            lb = sm[1]
            for i in range(2, layer + 1):
                lb = lb + sm[i]
            one_m_lb = 1.0 - lb
            tiny = jnp.float32(1e-37)
            log_lb = jnp.where(lb > 0.0, jnp.log2(jnp.maximum(lb, tiny)), -jnp.inf)
            log_1m_lb = jnp.where(one_m_lb > 0.0, jnp.log2(jnp.maximum(one_m_lb, tiny)), -jnp.inf)
            cand = log_1m_lb + log_sig
            log_f = (jnp.maximum(log_lb, cand)
                     + jnp.log2(1.0 + jnp.exp2(-jnp.abs(log_lb - cand))))
            kk = one_m_lb * sig_neg
        qh = _silu(q_ref[h].astype(F32))

        hi = log_f.astype(BF16)
        lo = (log_f - hi.astype(F32)).astype(BF16)
        b_wide = (jnp.dot(tri, widen(hi), preferred_element_type=F32)
                  + jnp.dot(tri, widen(lo), preferred_element_type=F32))
        b3 = jnp.stack([b_wide[:, i * LANE:(i + 1) * LANE] for i in range(n)])
        return qh.reshape(n, c, LANE), kk.reshape(n, c, LANE), b3

    heads = [gates(h) for h in range(hp)]
    mids = [b3 - b3[:, c // 2 - 1:c // 2] for _, _, b3 in heads]
    span = functools.reduce(jnp.maximum, [jnp.max(jnp.abs(d3)) for d3 in mids])

    def finish(h, o):
        o = o.reshape(tb, LANE)
        if finalize:
            o = _rms(o + of_ref[h], gain_ref[h]) * _silu(gate_ref[h].astype(F32))
        o_ref[h] = o.astype(o_ref.dtype)

    def intra(h, sc):
        scores = jnp.where(causal, sc, 0.0).astype(BF16)
        return jnp.einsum('nts,nsv->ntv', scores, v_ref[h].reshape(n, c, LANE),
                          preferred_element_type=F32)

    partial_out, updates = [], []
    for h, ((qh3, kk3, b3), d3) in enumerate(zip(heads, mids)):
        grow = jnp.exp2(jnp.clip(d3, -HGRN_FACTOR_RANGE, HGRN_FACTOR_RANGE))
        qt = (qh3 * grow).astype(BF16)
        kt = (kk3 / grow).astype(BF16)
        sc = jnp.einsum('ntk,nsk->nts', qt, kt, preferred_element_type=F32)
        v3 = v_ref[h].reshape(n, c, LANE)
        b_end = b3[:, 0:1] if reverse else b3[:, c - 1:c]
        k_in = (kk3 * jnp.exp2(b_end - b3)).astype(BF16)
        kv = jnp.einsum('nsv,nsk->nvk', v3, k_in, preferred_element_type=F32)
        partial_out.append(intra(h, sc))
        updates.append((kv, jnp.exp2(b_end)))
    incoming = []
    for h, (kv, decay_end) in enumerate(updates):
        st = st_ref[h]
        before = [None] * n
        for i in (range(n - 1, -1, -1) if reverse else range(n)):
            before[i] = st.astype(BF16)
            st = st * decay_end[i] + kv[i]
        st_ref[h] = st
        incoming.append(jnp.stack(before))
    for h, (qh3, kk3, b3) in enumerate(heads):
        inter = jnp.einsum('ntk,nvk->ntv', (qh3 * jnp.exp2(b3)).astype(BF16), incoming[h],
                           preferred_element_type=F32)
        rest_ref[h] = inter
        finish(h, partial_out[h] + inter)

    @pl.when(span > HGRN_FACTOR_RANGE)
    def _():
        for h, (qh3, kk3, b3) in enumerate(heads):
            qs_ref[h] = qh3
            ks_ref[h] = kk3
            bs_ref[h] = b3

        def per_chunk(i, carry):
            h, ci = i // n, i % n

            def pair(s, sc):
                rel = jnp.minimum(bs_ref[h, ci] - bs_ref[h, ci, pl.ds(s, 1), :], 0.0)
                w = qs_ref[h, ci] * ks_ref[h, ci, pl.ds(s, 1), :] * jnp.exp2(rel)
                return jnp.where(col == s, jnp.sum(w, axis=-1, keepdims=True), sc)

            sc_ref[h, ci] = lax.fori_loop(0, c, pair, jnp.zeros((c, c), F32))
            return carry

        lax.fori_loop(0, hp * n, per_chunk, 0)
        for h in range(hp):
            finish(h, intra(h, sc_ref[h]) + rest_ref[h])


def _hgrn_scan(pa, pf, pc, lower_bounds, *, seq, layer, reverse, tb, hp, o_fwd=None, gain=None):
    bsz = pa.shape[1] // seq
    tb = min(tb, seq)
    n_t = seq // tb
    n_chunks = tb // C_CHUNK
    finalize = o_fwd is not None
    assert C_HEADS % hp == 0

    def blk(cb):
        assert cb % hp == 0
        return pl.BlockSpec((hp, None, tb, LANE),
                            lambda b, hd, t: (cb // hp + hd, b, (n_t - 1 - t) if reverse else t, 0))

    def view(p):
        return p.reshape(p.shape[0], bsz, seq, LANE)

    f_cb = (CB_CFB if reverse else CB_CFF) - SEG_F[0]
    in_specs = [blk(CB_CQ - SEG_A[0]), blk(f_cb), blk(CB_CI - SEG_C[0]),
                pl.BlockSpec((DEPTH, hp, 1, LANE), lambda b, hd, t: (0, hd, 0, 0))]
    args = [view(pa), view(pf), view(pc), lower_bounds.reshape(DEPTH, C_HEADS, 1, LANE)]
    if finalize:
        in_specs += [blk(0), blk(CB_CGATE - SEG_C[0]),
                     pl.BlockSpec((hp, 1, LANE), lambda b, hd, t: (hd, 0, 0))]
        args += [o_fwd, view(pc), gain.reshape(C_HEADS, 1, LANE)]
    out_dtype = BF16 if finalize else F32
    return pl.pallas_call(
        functools.partial(_hgrn_body, layer=layer, reverse=reverse, tb=tb, finalize=finalize, hp=hp),
        grid=(bsz, C_HEADS // hp, n_t),
        in_specs=in_specs,
        out_specs=blk(0),
        out_shape=jax.ShapeDtypeStruct((C_HEADS, bsz, seq, LANE), out_dtype),
        scratch_shapes=[pltpu.VMEM((hp, LANE, LANE), F32),
                        pltpu.VMEM((hp, n_chunks, C_CHUNK, LANE), F32),
                        pltpu.VMEM((hp, n_chunks, C_CHUNK, LANE), F32),
                        pltpu.VMEM((hp, n_chunks, C_CHUNK, LANE), F32),
                        pltpu.VMEM((hp, n_chunks, C_CHUNK, C_CHUNK), F32),
                        pltpu.VMEM((hp, n_chunks, C_CHUNK, LANE), F32)],
        compiler_params=_cparams(("parallel", "parallel", "arbitrary")),
        name="hgrn_scan_bwd" if reverse else "hgrn_scan_fwd",
    )(*args)


def _silu(g):
    return g / (1.0 + jnp.exp(-g))


def _sigmoid(g):
    return 1.0 / (1.0 + jnp.exp(-g))


def _lane_concat(ref, n):
    return jnp.concatenate([ref[i] for i in range(n)], axis=-1)


def _merge_out_body(*refs, emit_next):
    ya_ref, ob_ref, bg0_ref, bg1_ref, yc_ref = refs[:5]
    m_refs = refs[5:17]
    x_ref, wa_ref, wb_ref, wc_ref, wo_ref, pg_ref = refs[17:23]
    if emit_next:
        ng_ref, out_ref, xn_ref = refs[23:]
    else:
        (out_ref,) = refs[23:]

    ya = _lane_concat(ya_ref, A_SLOTS)
    yb = []
    for hd in range(B_Q_HEADS):
        g_ref = bg0_ref if hd < 4 else bg1_ref
        yb.append((ob_ref[hd].astype(F32) * _silu(g_ref[hd % 4].astype(F32))).astype(BF16))
    yb = jnp.concatenate(yb, axis=-1)
    yc = _lane_concat(yc_ref, C_HEADS)

    proj = (jnp.dot(ya, wa_ref[...], preferred_element_type=F32),
            jnp.dot(yb, wb_ref[...], preferred_element_type=F32),
            jnp.dot(yc, wc_ref[...], preferred_element_type=F32))
    per_branch = D_MODEL // (TILE_CB * LANE)
    y = []
    for j in range(per_branch):
        cols = slice(j * TILE_CB * LANE, (j + 1) * TILE_CB * LANE)
        acc = None
        for br in range(N_BRANCHES):
            gate = _sigmoid(_lane_concat(m_refs[br * per_branch + j], TILE_CB).astype(F32))
            term = gate * proj[br][:, cols]
            acc = term if acc is None else acc + term
        y.append(acc.astype(BF16))
    y = jnp.concatenate(y, axis=-1)
    out = jnp.dot(y, wo_ref[...], preferred_element_type=F32)
    x_new = x_ref[...] + _rms(out, pg_ref[...])
    out_ref[...] = x_new
    if emit_next:
        xn_ref[...] = _rms(x_new, ng_ref[...]).astype(xn_ref.dtype)


def _merge_out(x2, pa, pc, ya, ob, yc, wa, wb, wc, wo, layer, post_gain, next_gain, *, tm):
    rows, d = x2.shape
    emit_next = next_gain is not None

    def heads(n, first=0):
        return pl.BlockSpec((n, tm, LANE), lambda i: (first // n, i, 0))

    def const(shape):
        return pl.BlockSpec(shape, lambda i: (0,) * len(shape), pipeline_mode=pl.Buffered(1))

    def weight(w):
        return pl.BlockSpec((None,) + w.shape[1:], lambda i: (layer, 0, 0),
                            pipeline_mode=pl.Buffered(1))

    n_merge = N_BRANCHES * D_MODEL // (TILE_CB * LANE)
    in_specs = ([heads(A_SLOTS), heads(B_Q_HEADS),
                 heads(4, CB_BGATE), heads(4, CB_BGATE + 4), heads(C_HEADS)]
                + [heads(TILE_CB, CB_MERGE - SEG_C[0] + TILE_CB * j) for j in range(n_merge)]
                + [pl.BlockSpec((tm, d), lambda i: (i, 0)),
                   weight(wa), weight(wb), weight(wc), weight(wo),
                   const((1, d))])
    args = ([ya, ob, pa, pa, yc] + [pc] * n_merge
            + [x2, wa, wb, wc, wo, post_gain.reshape(1, d)])
    row_spec = pl.BlockSpec((tm, d), lambda i: (i, 0))
    out_specs, out_shape = row_spec, jax.ShapeDtypeStruct((rows, d), F32)
    if emit_next:
        in_specs.append(const((1, d)))
        args.append(next_gain.reshape(1, d))
        out_specs = [row_spec, row_spec]
        out_shape = [out_shape, jax.ShapeDtypeStruct((rows, d), BF16)]
    return pl.pallas_call(
        functools.partial(_merge_out_body, emit_next=emit_next),
        grid=(rows // tm,),
        in_specs=in_specs,
        out_specs=out_specs,
        out_shape=out_shape,
        compiler_params=_cparams(("parallel",)),
        name="merge_out",
    )(*args)


def kernel(x, pre_norm, post_norm, w_in, sink_logits, hgrn_lower_bounds, hgrn_norm,
           w_branch_a, w_branch_b, w_branch_c, w_out):
    bsz, seq, d = x.shape
    rows = bsz * seq
    assert d == D_MODEL and w_in.shape[-1] == IN_COLS
    tile = min(PROJ_TILE_ROWS, seq)
    assert seq % tile == 0
    rope = _rope_tables(seq)
    plans = _projection_plans()
    x2 = x.reshape(rows, d)
    xn = _pre_norm(x2, pre_norm[0].astype(F32), tm=min(PRE_NORM_ROWS, rows))
    w, wa, wb, wc, wo = (t.astype(BF16) for t in (w_in, w_branch_a, w_branch_b, w_branch_c, w_out))
    for layer in range(DEPTH):
        pa = _in_projection(xn, w, layer, rope, seq=seq, seg=SEG_A, out_dtype=BF16, plans=plans,
                            tm=tile)
        pf = _in_projection(xn, w, layer, None, seq=seq, seg=SEG_F, out_dtype=F32, plans=None,
                            tm=tile)
        pc = _in_projection(xn, w, layer, None, seq=seq, seg=SEG_C, out_dtype=BF16, plans=None,
                            tm=tile)

        ya = _dilated_attention(pa, seq=seq, tile=tile, qb=ATTN_QUERY_BLOCK,
                                unroll=ATTN_UNITS_PER_BLOCK)
        sink = jnp.broadcast_to(sink_logits[layer].astype(F32)[:, None, None], (B_Q_HEADS, 1, LANE))
        ob = _windowed_attention(pa, sink, seq=seq, tile=tile, hw=B_HALF_WINDOW,
                                 qb=ATTN_QUERY_BLOCK, unroll=ATTN_UNITS_PER_BLOCK)

        lower = hgrn_lower_bounds.astype(F32)
        scan = functools.partial(_hgrn_scan, pa, pf, pc, lower, seq=seq, layer=layer,
                                 tb=HGRN_BLOCK_ROWS, hp=HGRN_HEADS_PER_STEP)
        o_fwd = scan(reverse=False)
        yc = scan(reverse=True, o_fwd=o_fwd, gain=hgrn_norm[layer].astype(F32))
        yc = yc.reshape(C_HEADS, rows, LANE)

        next_gain = pre_norm[layer + 1].astype(F32) if layer + 1 < DEPTH else None
        res = _merge_out(x2, pa, pc, ya, ob, yc, wa, wb, wc, wo, layer,
                         post_norm[layer].astype(F32), next_gain, tm=min(MERGE_ROWS, rows))
        x2, xn = res if next_gain is not None else (res, None)
    return x2.reshape(bsz, seq, d)
```

```python
import functools

import jax
import jax.numpy as jnp
from jax import lax
from jax.experimental import pallas as pl
from jax.experimental.pallas import tpu as pltpu

F32 = jnp.float32
BF16 = jnp.bfloat16

LANE = 128
D_MODEL = 2048
DEPTH = 2
HEAD_DIM = 128
ROT_DIM = HEAD_DIM // 4
ROT_HALF = ROT_DIM // 2
ROPE_THETA = 500000.0
NORM_EPS = 1e-6

DIL_GROUPS = ((128, 1), (512, 4), (2048, 16))
A_SLOTS = 4
A_HEADS = A_SLOTS * len(DIL_GROUPS)
B_Q_HEADS = 8
B_KV_HEADS = 2
B_HALF_WINDOW = 128
C_HEADS = 8
C_CHUNK = 64
N_BRANCHES = 3

CB_AQ = 0
CB_AK = CB_AQ + A_HEADS
CB_AV = CB_AK + A_HEADS
CB_AGATE = CB_AV + A_HEADS
CB_BQ = CB_AGATE + A_SLOTS
CB_BK = CB_BQ + B_Q_HEADS
CB_BV = CB_BK + B_KV_HEADS
CB_BGATE = CB_BV + B_KV_HEADS
CB_CQ = CB_BGATE + B_Q_HEADS
CB_CFF = CB_CQ + C_HEADS
CB_CFB = CB_CFF + C_HEADS
CB_CI = CB_CFB + C_HEADS
CB_CGATE = CB_CI + C_HEADS
CB_MERGE = CB_CGATE + C_HEADS
N_CB = CB_MERGE + N_BRANCHES * D_MODEL // LANE
IN_COLS = N_CB * LANE
TILE_CB = 4

SEG_A, SEG_F, SEG_C = (0, CB_CFF), (CB_CFF, CB_CI), (CB_CI, N_CB)

PROJ_TILE_ROWS = 2048
PRE_NORM_ROWS = 512
MERGE_ROWS = 256
ATTN_QUERY_BLOCK = 128
ATTN_UNITS_PER_BLOCK = 16
ATTN_SCORES_AHEAD = 4
HGRN_BLOCK_ROWS = 1024
HGRN_HEADS_PER_STEP = 4
NO_ROPE, ROPE_K, ROPE_Q = 0, 1, 2

MASK_VALUE = -1e30
LOG2_E = 1.4426950408889634
HGRN_FACTOR_RANGE = 115.0

VMEM_LIMIT = 56 * 1024 * 1024


def _cparams(sem):
    return pltpu.CompilerParams(dimension_semantics=sem, vmem_limit_bytes=VMEM_LIMIT)


def _projection_plans():
    codes = [NO_ROPE] * SEG_A[1]
    dils = [1] * SEG_A[1]
    for g, (_, dil) in enumerate(DIL_GROUPS):
        for s in range(A_SLOTS):
            codes[CB_AQ + g * A_SLOTS + s] = ROPE_Q
            codes[CB_AK + g * A_SLOTS + s] = ROPE_K
            for base in (CB_AQ, CB_AK, CB_AV):
                dils[base + g * A_SLOTS + s] = dil
    for h in range(B_Q_HEADS):
        codes[CB_BQ + h] = ROPE_Q
    for h in range(B_KV_HEADS):
        codes[CB_BK + h] = ROPE_K
    plans = []
    for t in range(SEG_A[1] // TILE_CB):
        sl = slice(t * TILE_CB, (t + 1) * TILE_CB)
        assert len(set(dils[sl])) == 1
        plans.append((tuple(codes[sl]), dils[sl][0]))
    return tuple(plans)


def _rms(x, gain):
    return x * lax.rsqrt(jnp.mean(x * x, axis=-1, keepdims=True) + NORM_EPS) * gain


def _pre_norm_body(x_ref, g_ref, o_ref):
    o_ref[...] = _rms(x_ref[...], g_ref[...]).astype(o_ref.dtype)


def _pre_norm(x2, gain, *, tm):
    rows, d = x2.shape
    return pl.pallas_call(
        _pre_norm_body,
        grid=(rows // tm,),
        in_specs=[pl.BlockSpec((tm, d), lambda i: (i, 0)), pl.BlockSpec((1, d), lambda i: (0, 0))],
        out_specs=pl.BlockSpec((tm, d), lambda i: (i, 0)),
        out_shape=jax.ShapeDtypeStruct((rows, d), BF16),
        compiler_params=_cparams(("parallel",)),
        name="pre_norm",
    )(x2, gain.reshape(1, d))


def _rope_tables(seq):
    pos = jnp.arange(seq, dtype=F32)
    inv_freq = ROPE_THETA ** (-jnp.arange(0, ROT_DIM, 2, dtype=F32) / ROT_DIM)
    ang = pos[:, None] * inv_freq[None, :]
    cos, sin = jnp.cos(ang), jnp.sin(ang)
    rest = HEAD_DIM - ROT_DIM
    c_full = jnp.concatenate([cos, cos, jnp.ones((seq, rest), F32)], axis=-1)
    s_full = jnp.concatenate([-sin, sin, jnp.zeros((seq, rest), F32)], axis=-1)
    return jnp.stack([c_full, s_full])


def _rotate_half_matrix(n_heads):
    width = n_heads * LANE
    src = lax.broadcasted_iota(jnp.int32, (width, width), 0)
    dst = lax.broadcasted_iota(jnp.int32, (width, width), 1)
    lane = dst % LANE
    hit = ((lane < ROT_HALF) & (src == dst + ROT_HALF)) | (
        (lane >= ROT_HALF) & (lane < ROT_DIM) & (src == dst - ROT_HALF))
    return jnp.where(hit, 1.0, 0.0).astype(BF16)


def _apply_rope(t, cos_full, sin_signed, swap):
    n_heads = t.shape[1] // LANE
    swapped = jnp.dot(t.astype(BF16), swap, preferred_element_type=F32)
    tile = lambda a: jnp.concatenate([a] * n_heads, axis=1) if n_heads > 1 else a
    return t * tile(cos_full) + swapped * tile(sin_signed)


def _in_projection_body(*refs, plans, tm):
    if plans is None:
        x_ref, w_ref, o_ref = refs
    else:
        x_ref, w_ref, rope_ref, o_ref, stage_ref, stage2_ref = refs
    acc = jnp.dot(x_ref[...], w_ref[...], preferred_element_type=F32)
    for c in range(TILE_CB):
        o_ref[c] = acc[:, c * LANE:(c + 1) * LANE].astype(o_ref.dtype)
    if plans is None:
        return

    j = pl.program_id(1)
    groups = {}
    for tile, plan in enumerate(plans):
        if plan != ((NO_ROPE,) * TILE_CB, 1):
            groups.setdefault(plan, []).append(tile)
    scale = HEAD_DIM ** -0.5 * LOG2_E
    pair = 2
    swap = _rotate_half_matrix(pair)
    for (codes, dil), tiles in groups.items():
        cond = functools.reduce(jnp.logical_or, [j == t for t in tiles])

        @pl.when(cond)
        def _(codes=codes, dil=dil):
            for c0 in range(0, TILE_CB, pair):
                code = codes[c0]
                assert all(codes[c] == code for c in range(c0, c0 + pair))
                if code == NO_ROPE and dil == 1:
                    continue
                t = acc[:, c0 * LANE:(c0 + pair) * LANE]
                if code != NO_ROPE:
                    t = _apply_rope(t, rope_ref[0], rope_ref[1], swap)
                if code == ROPE_Q:
                    t = t * scale
                for c in range(c0, c0 + pair):
                    part = t[:, (c - c0) * LANE:(c - c0 + 1) * LANE]
                    if dil == 1:
                        o_ref[c] = part.astype(o_ref.dtype)
                    else:
                        stage_ref[c] = part
            if dil > 1:
                cs = tm // dil
                first = min(dil, 4)
                second = dil // first
                span = tm // first
                for c in range(TILE_CB):
                    src_ref = stage_ref
                    if second > 1:
                        for p1 in range(first):
                            stage2_ref[c, p1 * span:(p1 + 1) * span, :] = (
                                stage_ref[c, pl.ds(p1, span, stride=first), :])
                        src_ref = stage2_ref

                    def phase(p, carry, c=c, src_ref=src_ref):
                        dst = pl.ds(pl.multiple_of(p * cs, cs), cs)
                        if second > 1:
                            start = (p % first) * span + p // first
                            src = src_ref[c, pl.ds(start, cs, stride=second), :]
                        else:
                            src = src_ref[c, pl.ds(p, cs, stride=dil), :]
                        o_ref[c, dst, :] = src.astype(o_ref.dtype)
                        return carry

                    lax.fori_loop(0, dil, phase, 0)


def _in_projection(xn, w_all, layer, rope, *, seq, seg, out_dtype, plans, tm):
    rows, d = xn.shape
    cb0, cb1 = seg
    n_tiles = (cb1 - cb0) // TILE_CB
    tn = TILE_CB * LANE
    tile0 = cb0 // TILE_CB
    in_specs = [pl.BlockSpec((tm, d), lambda i, j: (i, 0)),
                pl.BlockSpec((None, d, tn), lambda i, j: (layer, 0, tile0 + j))]
    args = [xn, w_all]
    scratch = []
    if plans is not None:
        per_seq = seq // tm
        in_specs.append(pl.BlockSpec((2, tm, LANE), lambda i, j: (0, i % per_seq, 0)))
        args.append(rope)
        scratch += [pltpu.VMEM((TILE_CB, tm, LANE), F32)] * 2
    return pl.pallas_call(
        functools.partial(_in_projection_body, plans=plans, tm=tm),
        grid=(rows // tm, n_tiles),
        in_specs=in_specs,
        out_specs=pl.BlockSpec((TILE_CB, tm, LANE), lambda i, j: (j, i, 0)),
        out_shape=jax.ShapeDtypeStruct((cb1 - cb0, rows, LANE), out_dtype),
        scratch_shapes=scratch,
        compiler_params=_cparams(("parallel", "arbitrary")),
        name=f"in_projection_{cb0}",
    )(*args)


def _band_bias(qb, hw):
    qi = lax.broadcasted_iota(jnp.int32, (qb, qb + 2 * hw), 0)
    kj = lax.broadcasted_iota(jnp.int32, (qb, qb + 2 * hw), 1)
    return jnp.where(jnp.abs(kj - hw - qi) <= hw, 0.0, MASK_VALUE)


def _first_grid_step():
    first = pl.program_id(0) == 0
    for axis in (1, 2):
        first = first & (pl.program_id(axis) == 0)
    return first


def _attend_tile(t, q_ref, k_refs, v_refs, kbuf, vbuf, band_ref, kbias_ref, emit, *,
                 dil, hw, cs, qb, n_sub, n_q, sink, unroll, ahead):
    nk = qb + 2 * hw
    nks = cs + 2 * hw
    n_u = cs // qb
    kp_ref, kc_ref, kn_ref = k_refs
    vp_ref, vc_ref, vn_ref = v_refs
    kbuf[:, 0:hw] = kp_ref[...]
    kbuf[:, hw:hw + cs] = kc_ref[...]
    kbuf[:, hw + cs:] = kn_ref[...]
    vbuf[:, 0:hw, 0:LANE] = vp_ref[...]
    vbuf[:, hw:hw + cs, 0:LANE] = vc_ref[...]
    vbuf[:, hw + cs:, 0:LANE] = vn_ref[...]

    @pl.when(_first_grid_step())
    def _():
        vbuf[:, :, LANE:] = jnp.ones((dil, nks, LANE), BF16)

    kpos = t * cs - hw + lax.broadcasted_iota(jnp.int32, (1, nks), 1)
    kbias_ref[...] = jnp.where((kpos >= 0) & (kpos < n_sub), 0.0, MASK_VALUE)

    n_blocks = dil * n_u
    group_size = min(unroll, n_blocks)
    assert n_blocks % group_size == 0

    def group(i, carry):
        units = []
        for j in range(group_size):
            idx = i * group_size + j
            u = idx % n_u
            ph = idx // n_u
            units.append((u, ph, pl.multiple_of(u * qb, qb)))

        def score(unit):
            u, ph, u0 = unit
            q = q_ref[:, ph, pl.ds(u0, qb), :].reshape(n_q * qb, LANE)
            s = lax.dot_general(q, kbuf[ph, pl.ds(u0, nk), :],
                                (((1,), (1,)), ((), ())), preferred_element_type=F32)
            return s.reshape(n_q, qb, nk) + band_ref[...] + kbias_ref[:, pl.ds(u0, nk)]

        def softmax(s):
            m = jnp.max(s, axis=-1, keepdims=True)
            if sink is not None:
                m = jnp.maximum(m, sink)
            return jnp.exp2(s - m).astype(BF16).reshape(n_q * qb, nk), m

        def finish(unit, p, m):
            u, ph, u0 = unit
            o2 = jnp.dot(p, vbuf[ph, pl.ds(u0, nk), :], preferred_element_type=F32)
            rows = pl.ds(u0, qb) if dil == 1 else pl.ds(ph + u * (qb * dil), qb, stride=dil)
            emit(rows, o2.reshape(n_q, qb, 2 * LANE), m)

        scores, probs = {}, {}
        for step in range(group_size + 2 * ahead):
            if step < group_size:
                scores[step] = score(units[step])
            j = step - ahead
            if 0 <= j < group_size:
                probs[j] = softmax(scores.pop(j))
            j = step - 2 * ahead
            if 0 <= j < group_size:
                finish(units[j], *probs.pop(j))
        return carry

    lax.fori_loop(0, n_blocks // group_size, group, 0)


def _tile_specs(cb, n, dil, cs, hw, n_t):
    per = cs // hw
    lead = cb if n is None else cb // n

    def spec(rows, tile_of, blk):
        return pl.BlockSpec((n, None, None, dil, rows, LANE),
                            lambda b, hd, t: (lead + hd, b, tile_of(t), 0, blk, 0))

    return (spec(hw, lambda t: jnp.maximum(t - 1, 0), per - 1),
            spec(cs, lambda t: t, 0),
            spec(hw, lambda t: jnp.minimum(t + 1, n_t - 1), 0))


def _windowed_attention_body(q_ref, kp_ref, kc_ref, kn_ref, vp_ref, vc_ref, vn_ref, sink_ref, o_ref,
                             kbuf, vbuf, band_ref, kbias_ref, *, hw, cs, qb, n_sub, n_q, unroll):
    band_ref[...] = _band_bias(qb, hw)
    sink = sink_ref[...][:, :, :1] * LOG2_E

    def emit(rows, o2, m):
        denom = o2[:, :, LANE:] + jnp.exp2(sink - m)
        o_ref[:, rows, :] = (o2[:, :, :LANE] / denom).astype(o_ref.dtype)

    _attend_tile(pl.program_id(2), q_ref, (kp_ref, kc_ref, kn_ref), (vp_ref, vc_ref, vn_ref),
                 kbuf, vbuf, band_ref, kbias_ref, emit, dil=1, hw=hw, cs=cs, qb=qb, n_sub=n_sub,
                 n_q=n_q, sink=sink, unroll=unroll, ahead=unroll)


def _windowed_attention(pa, sink, *, seq, tile, hw, qb, unroll):
    n_cb, rows, _ = pa.shape
    bsz = rows // seq
    n_t = seq // tile
    n_q = B_Q_HEADS // B_KV_HEADS
    view = pa.reshape(n_cb, bsz, n_t, 1, tile, LANE)
    in_specs = ([_tile_specs(CB_BQ, n_q, 1, tile, hw, n_t)[1]]
                + list(_tile_specs(CB_BK, None, 1, tile, hw, n_t))
                + list(_tile_specs(CB_BV, None, 1, tile, hw, n_t))
                + [pl.BlockSpec((n_q, 1, LANE), lambda b, hd, t: (hd, 0, 0))])
    out = pl.pallas_call(
        functools.partial(_windowed_attention_body, hw=hw, cs=tile, qb=qb, n_sub=seq, n_q=n_q,
                          unroll=unroll),
        grid=(bsz, B_KV_HEADS, n_t),
        in_specs=in_specs,
        out_specs=pl.BlockSpec((n_q, None, tile, LANE), lambda b, hd, t: (hd, b, t, 0)),
        out_shape=jax.ShapeDtypeStruct((B_Q_HEADS, bsz, seq, LANE), BF16),
        scratch_shapes=[pltpu.VMEM((1, tile + 2 * hw, LANE), BF16),
                        pltpu.VMEM((1, tile + 2 * hw, 2 * LANE), BF16),
                        pltpu.VMEM((qb, qb + 2 * hw), F32),
                        pltpu.VMEM((1, tile + 2 * hw), F32)],
        compiler_params=_cparams(("arbitrary", "arbitrary", "arbitrary")),
        name="windowed_attention",
    )(*([view] * 7 + [sink]))
    return out.reshape(B_Q_HEADS, rows, LANE)


def _dilated_attention_body(*refs, tile, hw, qb, seq, unroll):
    n_g = len(DIL_GROUPS)
    group_refs = [refs[7 * g:7 * g + 7] for g in range(n_g)]
    gate_ref, o_ref = refs[7 * n_g:7 * n_g + 2]
    scratch = refs[7 * n_g + 2:]
    bufs = [scratch[3 * g:3 * g + 3] for g in range(n_g)]
    band_ref, num_ref, max_ref, den_ref = scratch[3 * n_g:]
    band_ref[...] = _band_bias(qb, hw)

    for g, (_, dil) in enumerate(DIL_GROUPS):
        q_ref, kp_ref, kc_ref, kn_ref, vp_ref, vc_ref, vn_ref = group_refs[g]
        kbuf, vbuf, kbias_ref = bufs[g]

        def emit(rows, o2, m, g=g):
            num_ref[g, rows, :] = o2[0, :, :LANE]
            den_ref[g, rows, :] = o2[0, :, LANE:]
            max_ref[g, rows, :] = jnp.broadcast_to(m[0], (qb, LANE))

        _attend_tile(pl.program_id(2), q_ref, (kp_ref, kc_ref, kn_ref), (vp_ref, vc_ref, vn_ref),
                     kbuf, vbuf, band_ref, kbias_ref, emit, dil=dil, hw=hw, cs=tile // dil, qb=qb,
                     n_sub=seq // dil, n_q=1, sink=None, unroll=unroll,
                     ahead=ATTN_SCORES_AHEAD)

    top = functools.reduce(jnp.maximum, [max_ref[g] for g in range(n_g)])
    weights = [jnp.exp2(max_ref[g] - top) for g in range(n_g)]
    num = sum(w * num_ref[g] for g, w in enumerate(weights))
    den = sum(w * den_ref[g] for g, w in enumerate(weights))
    o_ref[...] = (num / den * _silu(gate_ref[...].astype(F32))).astype(o_ref.dtype)


def _dilated_attention(pa, *, seq, tile, qb, unroll):
    n_cb, rows, _ = pa.shape
    bsz = rows // seq
    n_t = seq // tile
    hws = {window // (2 * dil) for window, dil in DIL_GROUPS}
    assert len(hws) == 1
    hw = hws.pop()
    in_specs, args, scratch = [], [], []
    for g, (_, dil) in enumerate(DIL_GROUPS):
        cs = tile // dil
        assert cs % qb == 0 and cs % hw == 0
        view = pa.reshape(n_cb, bsz, n_t, dil, cs, LANE)
        in_specs += ([_tile_specs(CB_AQ + g * A_SLOTS, 1, dil, cs, hw, n_t)[1]]
                     + list(_tile_specs(CB_AK + g * A_SLOTS, None, dil, cs, hw, n_t))
                     + list(_tile_specs(CB_AV + g * A_SLOTS, None, dil, cs, hw, n_t)))
        args += [view] * 7
        scratch += [pltpu.VMEM((dil, cs + 2 * hw, LANE), BF16),
                    pltpu.VMEM((dil, cs + 2 * hw, 2 * LANE), BF16),
                    pltpu.VMEM((1, cs + 2 * hw), F32)]
    row_spec = pl.BlockSpec((None, None, tile, LANE), lambda b, hd, t: (hd, b, t, 0))
    in_specs.append(pl.BlockSpec((None, None, tile, LANE), lambda b, hd, t: (CB_AGATE + hd, b, t, 0)))
    args.append(pa.reshape(n_cb, bsz, seq, LANE))
    n_g = len(DIL_GROUPS)
    scratch += [pltpu.VMEM((qb, qb + 2 * hw), F32),
                pltpu.VMEM((n_g, tile, LANE), F32),
                pltpu.VMEM((n_g, tile, LANE), F32),
                pltpu.VMEM((n_g, tile, LANE), F32)]
    out = pl.pallas_call(
        functools.partial(_dilated_attention_body, tile=tile, hw=hw, qb=qb, seq=seq, unroll=unroll),
        grid=(bsz, A_SLOTS, n_t),
        in_specs=in_specs,
        out_specs=row_spec,
        out_shape=jax.ShapeDtypeStruct((A_SLOTS, bsz, seq, LANE), BF16),
        scratch_shapes=scratch,
        compiler_params=_cparams(("arbitrary", "arbitrary", "arbitrary")),
        name="dilated_attention",
    )(*args)
    return out.reshape(A_SLOTS, rows, LANE)


def _hgrn_body(*refs, layer, reverse, tb, finalize, hp):
    q_ref, f_ref, v_ref, lbp_ref = refs[:4]
    rest = list(refs[4:])
    if finalize:
        of_ref, gate_ref, gain_ref = rest[:3]
        rest = rest[3:]
    o_ref, st_ref, qs_ref, ks_ref, bs_ref, sc_ref, rest_ref = rest
    c = C_CHUNK
    n = tb // c

    @pl.when(pl.program_id(2) == 0)
    def _():
        st_ref[...] = jnp.zeros_like(st_ref)

    row = lax.broadcasted_iota(jnp.int32, (c, c), 0)
    col = lax.broadcasted_iota(jnp.int32, (c, c), 1)
    causal = (row <= col) if reverse else (row >= col)
    tri = jnp.where(causal, 1.0, 0.0).astype(BF16)

    def widen(a):
        return jnp.concatenate([a[i * c:(i + 1) * c] for i in range(n)], axis=1)

    def gates(h):
        fz = f_ref[h]
        e = jnp.exp2(jnp.abs(fz) * -LOG2_E)
        log_sig = jnp.minimum(fz, 0.0) * LOG2_E - jnp.log2(1.0 + e)
        sig_neg = jnp.where(fz >= 0.0, e, 1.0) / (1.0 + e)
        if layer == 0:
            log_f, kk = log_sig, sig_neg
        else:
            lbp = lbp_ref[:, h]
            e_l = jnp.exp(lbp - jnp.max(lbp, axis=0, keepdims=True))
            sm = e_l / jnp.sum(e_l, axis=0, keepdims=True)
            lb = sm[1]
            for i in range(2, layer + 1):
                lb = lb + sm[i]
            one_m_lb = 1.0 - lb
            tiny = jnp.float32(1e-37)
            log_lb = jnp.where(lb > 0.0, jnp.log2(jnp.maximum(lb, tiny)), -jnp.inf)
            log_1m_lb = jnp.where(one_m_lb > 0.0, jnp.log2(jnp.maximum(one_m_lb, tiny)), -jnp.inf)
            cand = log_1m_lb + log_sig
            log_f = (jnp.maximum(log_lb, cand)
                     + jnp.log2(1.0 + jnp.exp2(-jnp.abs(log_lb - cand))))
            kk = one_m_lb * sig_neg
        qh = _silu(q_ref[h].astype(F32))

        hi = log_f.astype(BF16)
        lo = (log_f - hi.astype(F32)).astype(BF16)
        b_wide = (jnp.dot(tri, widen(hi), preferred_element_type=F32)
                  + jnp.dot(tri, widen(lo), preferred_element_type=F32))
        b3 = jnp.stack([b_wide[:, i * LANE:(i + 1) * LANE] for i in range(n)])
        return qh.reshape(n, c, LANE), kk.reshape(n, c, LANE), b3

    heads = [gates(h) for h in range(hp)]
    mids = [b3 - b3[:, c // 2 - 1:c // 2] for _, _, b3 in heads]
    span = functools.reduce(jnp.maximum, [jnp.max(jnp.abs(d3)) for d3 in mids])

    def finish(h, o):
        o = o.reshape(tb, LANE)
        if finalize:
            o = _rms(o + of_ref[h], gain_ref[h]) * _silu(gate_ref[h].astype(F32))
        o_ref[h] = o.astype(o_ref.dtype)

    def intra(h, sc):
        scores = jnp.where(causal, sc, 0.0).astype(BF16)
        return jnp.einsum('nts,nsv->ntv', scores, v_ref[h].reshape(n, c, LANE),
                          preferred_element_type=F32)

    partial_out, updates = [], []
    for h, ((qh3, kk3, b3), d3) in enumerate(zip(heads, mids)):
        grow = jnp.exp2(jnp.clip(d3, -HGRN_FACTOR_RANGE, HGRN_FACTOR_RANGE))
        qt = (qh3 * grow).astype(BF16)
        kt = (kk3 / grow).astype(BF16)
        sc = jnp.einsum('ntk,nsk->nts', qt, kt, preferred_element_type=F32)
        v3 = v_ref[h].reshape(n, c, LANE)
        b_end = b3[:, 0:1] if reverse else b3[:, c - 1:c]
        k_in = (kk3 * jnp.exp2(b_end - b3)).astype(BF16)
        kv = jnp.einsum('nsv,nsk->nvk', v3, k_in, preferred_element_type=F32)
        partial_out.append(intra(h, sc))
        updates.append((kv, jnp.exp2(b_end)))
    incoming = []
    for h, (kv, decay_end) in enumerate(updates):
        st = st_ref[h]
        before = [None] * n
        for i in (range(n - 1, -1, -1) if reverse else range(n)):
            before[i] = st.astype(BF16)
            st = st * decay_end[i] + kv[i]
        st_ref[h] = st
        incoming.append(jnp.stack(before))
    for h, (qh3, kk3, b3) in enumerate(heads):
        inter = jnp.einsum('ntk,nvk->ntv', (qh3 * jnp.exp2(b3)).astype(BF16), incoming[h],
                           preferred_element_type=F32)
        rest_ref[h] = inter
        finish(h, partial_out[h] + inter)

    @pl.when(span > HGRN_FACTOR_RANGE)
    def _():
        for h, (qh3, kk3, b3) in enumerate(heads):
            qs_ref[h] = qh3
            ks_ref[h] = kk3
            bs_ref[h] = b3

        def per_chunk(i, carry):
            h, ci = i // n, i % n

            def pair(s, sc):
                rel = jnp.minimum(bs_ref[h, ci] - bs_ref[h, ci, pl.ds(s, 1), :], 0.0)
                w = qs_ref[h, ci] * ks_ref[h, ci, pl.ds(s, 1), :] * jnp.exp2(rel)
                return jnp.where(col == s, jnp.sum(w, axis=-1, keepdims=True), sc)

            sc_ref[h, ci] = lax.fori_loop(0, c, pair, jnp.zeros((c, c), F32))
            return carry

        lax.fori_loop(0, hp * n, per_chunk, 0)
        for h in range(hp):
            finish(h, intra(h, sc_ref[h]) + rest_ref[h])


def _hgrn_scan(pa, pf, pc, lower_bounds, *, seq, layer, reverse, tb, hp, o_fwd=None, gain=None):
    bsz = pa.shape[1] // seq
    tb = min(tb, seq)
    n_t = seq // tb
    n_chunks = tb // C_CHUNK
    finalize = o_fwd is not None
    assert C_HEADS % hp == 0

    def blk(cb):
        assert cb % hp == 0
        return pl.BlockSpec((hp, None, tb, LANE),
                            lambda b, hd, t: (cb // hp + hd, b, (n_t - 1 - t) if reverse else t, 0))

    def view(p):
        return p.reshape(p.shape[0], bsz, seq, LANE)

    f_cb = (CB_CFB if reverse else CB_CFF) - SEG_F[0]
    in_specs = [blk(CB_CQ - SEG_A[0]), blk(f_cb), blk(CB_CI - SEG_C[0]),
                pl.BlockSpec((DEPTH, hp, 1, LANE), lambda b, hd, t: (0, hd, 0, 0))]
    args = [view(pa), view(pf), view(pc), lower_bounds.reshape(DEPTH, C_HEADS, 1, LANE)]
    if finalize:
        in_specs += [blk(0), blk(CB_CGATE - SEG_C[0]),
                     pl.BlockSpec((hp, 1, LANE), lambda b, hd, t: (hd, 0, 0))]
        args += [o_fwd, view(pc), gain.reshape(C_HEADS, 1, LANE)]
    out_dtype = BF16 if finalize else F32
    return pl.pallas_call(
        functools.partial(_hgrn_body, layer=layer, reverse=reverse, tb=tb, finalize=finalize, hp=hp),
        grid=(bsz, C_HEADS // hp, n_t),
        in_specs=in_specs,
        out_specs=blk(0),
        out_shape=jax.ShapeDtypeStruct((C_HEADS, bsz, seq, LANE), out_dtype),
        scratch_shapes=[pltpu.VMEM((hp, LANE, LANE), F32),
                        pltpu.VMEM((hp, n_chunks, C_CHUNK, LANE), F32),
                        pltpu.VMEM((hp, n_chunks, C_CHUNK, LANE), F32),
                        pltpu.VMEM((hp, n_chunks, C_CHUNK, LANE), F32),
                        pltpu.VMEM((hp, n_chunks, C_CHUNK, C_CHUNK), F32),
                        pltpu.VMEM((hp, n_chunks, C_CHUNK, LANE), F32)],
        compiler_params=_cparams(("parallel", "parallel", "arbitrary")),
        name="hgrn_scan_bwd" if reverse else "hgrn_scan_fwd",
    )(*args)


def _silu(g):
    return g / (1.0 + jnp.exp(-g))


def _sigmoid(g):
    return 1.0 / (1.0 + jnp.exp(-g))


def _lane_concat(ref, n):
    return jnp.concatenate([ref[i] for i in range(n)], axis=-1)


def _merge_out_body(*refs, emit_next):
    ya_ref, ob_ref, bg0_ref, bg1_ref, yc_ref = refs[:5]
    m_refs = refs[5:17]
    x_ref, wa_ref, wb_ref, wc_ref, wo_ref, pg_ref = refs[17:23]
    if emit_next:
        ng_ref, out_ref, xn_ref = refs[23:]
    else:
        (out_ref,) = refs[23:]

    ya = _lane_concat(ya_ref, A_SLOTS)
    yb = []
    for hd in range(B_Q_HEADS):
        g_ref = bg0_ref if hd < 4 else bg1_ref
        yb.append((ob_ref[hd].astype(F32) * _silu(g_ref[hd % 4].astype(F32))).astype(BF16))
    yb = jnp.concatenate(yb, axis=-1)
    yc = _lane_concat(yc_ref, C_HEADS)

    proj = (jnp.dot(ya, wa_ref[...], preferred_element_type=F32),
            jnp.dot(yb, wb_ref[...], preferred_element_type=F32),
            jnp.dot(yc, wc_ref[...], preferred_element_type=F32))
    per_branch = D_MODEL // (TILE_CB * LANE)
    y = []
    for j in range(per_branch):
        cols = slice(j * TILE_CB * LANE, (j + 1) * TILE_CB * LANE)
        acc = None
        for br in range(N_BRANCHES):
            gate = _sigmoid(_lane_concat(m_refs[br * per_branch + j], TILE_CB).astype(F32))
            term = gate * proj[br][:, cols]
            acc = term if acc is None else acc + term
        y.append(acc.astype(BF16))
    y = jnp.concatenate(y, axis=-1)
    out = jnp.dot(y, wo_ref[...], preferred_element_type=F32)
    x_new = x_ref[...] + _rms(out, pg_ref[...])
    out_ref[...] = x_new
    if emit_next:
        xn_ref[...] = _rms(x_new, ng_ref[...]).astype(xn_ref.dtype)


def _merge_out(x2, pa, pc, ya, ob, yc, wa, wb, wc, wo, layer, post_gain, next_gain, *, tm):
    rows, d = x2.shape
    emit_next = next_gain is not None

    def heads(n, first=0):
        return pl.BlockSpec((n, tm, LANE), lambda i: (first // n, i, 0))

    def const(shape):
        return pl.BlockSpec(shape, lambda i: (0,) * len(shape), pipeline_mode=pl.Buffered(1))

    def weight(w):
        return pl.BlockSpec((None,) + w.shape[1:], lambda i: (layer, 0, 0),
                            pipeline_mode=pl.Buffered(1))

    n_merge = N_BRANCHES * D_MODEL // (TILE_CB * LANE)
    in_specs = ([heads(A_SLOTS), heads(B_Q_HEADS),
                 heads(4, CB_BGATE), heads(4, CB_BGATE + 4), heads(C_HEADS)]
                + [heads(TILE_CB, CB_MERGE - SEG_C[0] + TILE_CB * j) for j in range(n_merge)]
                + [pl.BlockSpec((tm, d), lambda i: (i, 0)),
                   weight(wa), weight(wb), weight(wc), weight(wo),
                   const((1, d))])
    args = ([ya, ob, pa, pa, yc] + [pc] * n_merge
            + [x2, wa, wb, wc, wo, post_gain.reshape(1, d)])
    row_spec = pl.BlockSpec((tm, d), lambda i: (i, 0))
    out_specs, out_shape = row_spec, jax.ShapeDtypeStruct((rows, d), F32)
    if emit_next:
        in_specs.append(const((1, d)))
        args.append(next_gain.reshape(1, d))
        out_specs = [row_spec, row_spec]
        out_shape = [out_shape, jax.ShapeDtypeStruct((rows, d), BF16)]
    return pl.pallas_call(
        functools.partial(_merge_out_body, emit_next=emit_next),
        grid=(rows // tm,),
        in_specs=in_specs,
        out_specs=out_specs,
        out_shape=out_shape,
        compiler_params=_cparams(("parallel",)),
        name="merge_out",
    )(*args)


def kernel(x, pre_norm, post_norm, w_in, sink_logits, hgrn_lower_bounds, hgrn_norm,
           w_branch_a, w_branch_b, w_branch_c, w_out):
    bsz, seq, d = x.shape
    rows = bsz * seq
    assert d == D_MODEL and w_in.shape[-1] == IN_COLS
    tile = min(PROJ_TILE_ROWS, seq)
    assert seq % tile == 0
    rope = _rope_tables(seq)
    plans = _projection_plans()
    x2 = x.reshape(rows, d)
    xn = _pre_norm(x2, pre_norm[0].astype(F32), tm=min(PRE_NORM_ROWS, rows))
    w, wa, wb, wc, wo = (t.astype(BF16) for t in (w_in, w_branch_a, w_branch_b, w_branch_c, w_out))
    for layer in range(DEPTH):
        pa = _in_projection(xn, w, layer, rope, seq=seq, seg=SEG_A, out_dtype=BF16, plans=plans,
                            tm=tile)
        pf = _in_projection(xn, w, layer, None, seq=seq, seg=SEG_F, out_dtype=F32, plans=None,
                            tm=tile)
        pc = _in_projection(xn, w, layer, None, seq=seq, seg=SEG_C, out_dtype=BF16, plans=None,
                            tm=tile)

        ya = _dilated_attention(pa, seq=seq, tile=tile, qb=ATTN_QUERY_BLOCK,
                                unroll=ATTN_UNITS_PER_BLOCK)
        sink = jnp.broadcast_to(sink_logits[layer].astype(F32)[:, None, None], (B_Q_HEADS, 1, LANE))
        ob = _windowed_attention(pa, sink, seq=seq, tile=tile, hw=B_HALF_WINDOW,
                                 qb=ATTN_QUERY_BLOCK, unroll=ATTN_UNITS_PER_BLOCK)

        lower = hgrn_lower_bounds.astype(F32)
        scan = functools.partial(_hgrn_scan, pa, pf, pc, lower, seq=seq, layer=layer,
                                 tb=HGRN_BLOCK_ROWS, hp=HGRN_HEADS_PER_STEP)
        o_fwd = scan(reverse=False)
        yc = scan(reverse=True, o_fwd=o_fwd, gain=hgrn_norm[layer].astype(F32))
        yc = yc.reshape(C_HEADS, rows, LANE)

        next_gain = pre_norm[layer + 1].astype(F32) if layer + 1 < DEPTH else None
        res = _merge_out(x2, pa, pc, ya, ob, yc, wa, wb, wc, wo, layer,
                         post_norm[layer].astype(F32), next_gain, tm=min(MERGE_ROWS, rows))
        x2, xn = res if next_gain is not None else (res, None)
    return x2.reshape(bsz, seq, d)
```
